```python
import jax, jax.numpy as jnp
from jax import lax
import numpy as np

D_MODEL = 1024
BATCH = 8
SEQ = 2048
DEPTH = 1
DEC_BATCH = 128
DEC_SEQ = 1
PAST_LEN = 16384
PAGE_SIZE = 128

N_META = 16
RET_HEADS = 4
RET_HEAD_DIM = 128
RET_W = RET_HEADS * RET_HEAD_DIM
CONV_CH = D_MODEL - RET_W
CONV_WIDTH = 31
CHUNK = 128
ROPE_BASE = 10000.0
D_FF = ((8 * D_MODEL // 3 + 255) // 256) * 256
IN_COLS = 4 * RET_W + 2 * CONV_CH
EPS = 1e-6

kernel_name = "hymba_retnet_conformer_decode_step"


def rms_norm(x, w):
    x32 = x.astype(jnp.float32)
    y = x32 * lax.rsqrt(jnp.mean(x32 * x32, axis=-1, keepdims=True) + EPS)
    return (y * w.astype(jnp.float32)).astype(x.dtype)


def log_gammas():
    return jnp.log(1.0 - 2.0 ** (-5.0 - jnp.arange(RET_HEADS, dtype=jnp.float32)))


def rotary(x, pos):
    half = RET_HEAD_DIM // 2
    inv = 1.0 / (ROPE_BASE ** jnp.linspace(0.0, 1.0, half, dtype=jnp.float32))
    ang = pos.astype(jnp.float32)[:, None] * inv[None, :]
    cos = jnp.cos(ang)[None, :, None, :]
    sin = jnp.sin(ang)[None, :, None, :]
    x1, x2 = x[..., :half], x[..., half:]
    return jnp.concatenate([x1 * cos - x2 * sin, x1 * sin + x2 * cos], axis=-1)


def retention_chunkwise(q, k, v):
    B, L, H, Dh = q.shape
    pad = CHUNK - N_META
    padf = lambda t: jnp.pad(t, ((0, 0), (pad, 0), (0, 0), (0, 0)))
    Lp = L + pad
    nc = Lp // CHUNK
    qc = padf(q).reshape(B, nc, CHUNK, H, Dh)
    kc = padf(k).reshape(B, nc, CHUNK, H, Dh)
    vc = padf(v).reshape(B, nc, CHUNK, H, Dh)
    lg = log_gammas()
    idx = jnp.arange(CHUNK, dtype=jnp.float32)
    diff = idx[:, None] - idx[None, :]
    mask = jnp.where(diff[None] >= 0, jnp.exp(lg[:, None, None] * jnp.maximum(diff, 0.0)[None]), 0.0)
    dec_end = jnp.exp(lg[:, None] * (CHUNK - 1 - idx)[None, :])
    dec_start = jnp.exp(lg[:, None] * (idx + 1.0)[None, :])
    chunk_dec = jnp.exp(lg * CHUNK)
    scores = jnp.einsum('bnihd,bnjhd->bnhij', qc, kc) * mask[None, None]
    o_intra = jnp.einsum('bnhij,bnjhe->bnihe', scores, vc)
    kv = jnp.einsum('bnjhd,bnjhe->bnhde', kc * dec_end.T[None, None, :, :, None], vc)

    def step(R, kv_n):
        return chunk_dec[None, :, None, None] * R + kv_n, R

    R_final, R_prev = lax.scan(step, jnp.zeros((B, H, Dh, Dh), jnp.float32), jnp.moveaxis(kv, 1, 0))
    R_prev = jnp.moveaxis(R_prev, 0, 1)
    o_inter = jnp.einsum('bnihd,bnhde->bnihe', qc, R_prev) * dec_start.T[None, None, :, :, None]
    o = (o_intra + o_inter).reshape(B, Lp, H, Dh)[:, pad:]
    return o, R_final


def retention_recurrent(q, k, v, S0):
    g = jnp.exp(log_gammas())[None, :, None, None]

    def step(S, qkv):
        q_t, k_t, v_t = qkv
        S = g * S + jnp.einsum('bhd,bhe->bhde', k_t, v_t)
        return S, jnp.einsum('bhd,bhde->bhe', q_t, S)

    S, o = lax.scan(step, S0, (jnp.moveaxis(q, 1, 0), jnp.moveaxis(k, 1, 0), jnp.moveaxis(v, 1, 0)))
    return jnp.moveaxis(o, 0, 1), S


def conformer_conv(a, b, buf, conv_w, conv_b, ln_g, ln_b):
    u = a * jax.nn.sigmoid(b)
    cat = jnp.concatenate([buf.astype(u.dtype), u], axis=1)
    y = lax.conv_general_dilated(cat, conv_w[:, None, :].astype(u.dtype), window_strides=(1,),
                                 padding='VALID', dimension_numbers=('NWC', 'WIO', 'NWC'),
                                 feature_group_count=CONV_CH) + conv_b
    new_buf = cat[:, -(CONV_WIDTH - 1):]
    y32 = y.astype(jnp.float32)
    mu = jnp.mean(y32, axis=-1, keepdims=True)
    var = jnp.mean(jnp.square(y32 - mu), axis=-1, keepdims=True)
    yn = (y32 - mu) * lax.rsqrt(var + EPS) * ln_g.astype(jnp.float32) + ln_b.astype(jnp.float32)
    return jax.nn.silu(yn).astype(a.dtype), new_buf


def layer(h, pos, ret_state, conv_buf, n_mix_pre, n_mix_post, n_ffn_pre, n_ffn_post,
          w_in, conv_w, conv_b, ln_g, ln_b, w_out, w_ffn_in, w_ffn_out):
    B, T, _ = h.shape
    xn = rms_norm(h, n_mix_pre)
    proj = xn @ w_in
    q, k, v, g, a, b = jnp.split(proj, [RET_W, 2 * RET_W, 3 * RET_W, 4 * RET_W, 4 * RET_W + CONV_CH], axis=-1)
    heads = lambda t: t.astype(jnp.float32).reshape(B, T, RET_HEADS, RET_HEAD_DIM)
    qh = rotary(heads(q), pos)
    kh = rotary(heads(k), pos) * (RET_HEAD_DIM ** -0.5)
    vh = heads(v)
    if ret_state is None:
        o, new_ret = retention_chunkwise(qh, kh, vh)
    else:
        o, new_ret = retention_recurrent(qh, kh, vh, ret_state.astype(jnp.float32))
    o = o * lax.rsqrt(jnp.mean(o * o, axis=-1, keepdims=True) + EPS)
    ret_out = (o.reshape(B, T, RET_W) * jax.nn.silu(g.astype(jnp.float32))).astype(h.dtype)
    conv_out, new_buf = conformer_conv(a, b, conv_buf, conv_w, conv_b, ln_g, ln_b)
    mix = jnp.concatenate([ret_out, conv_out], axis=-1) @ w_out
    h = h + rms_norm(mix, n_mix_post)
    xf = rms_norm(h, n_ffn_pre)
    gate, up = jnp.split(xf @ w_ffn_in, [D_FF], axis=-1)
    h = h + rms_norm((jax.nn.silu(gate) * up) @ w_ffn_out, n_ffn_post)
    return h, new_ret, new_buf


def setup_inputs(seed: int = 0) -> dict:
    key = jax.random.key(seed)
    ks = jax.random.split(key, 20)
    f32 = jnp.float32
    nrm = lambda k, shape, s: jax.random.normal(k, shape, f32) * s
    gain = lambda k: 1.0 + nrm(k, (DEPTH, D_MODEL), 0.05)
    return {
        "x_prompt": nrm(ks[0], (BATCH, SEQ, D_MODEL), 1.0),
        "x_sample": nrm(ks[1], (DEC_BATCH, DEC_SEQ, D_MODEL), 1.0),
        "state_ret": nrm(ks[2], (DEPTH, DEC_BATCH, RET_HEADS, RET_HEAD_DIM, RET_HEAD_DIM), 0.1),
        "state_conv": nrm(ks[3], (DEPTH, DEC_BATCH, CONV_WIDTH - 1, CONV_CH), 0.5),
        "meta_tokens": nrm(ks[4], (N_META, D_MODEL), 1.0),
        "norm_mix_pre": gain(ks[5]),
        "norm_mix_post": gain(ks[6]),
        "norm_ffn_pre": gain(ks[7]),
        "norm_ffn_post": gain(ks[8]),
        "w_in": nrm(ks[9], (DEPTH, D_MODEL, IN_COLS), D_MODEL ** -0.5),
        "conv_w": nrm(ks[10], (DEPTH, CONV_WIDTH, CONV_CH), CONV_WIDTH ** -0.5),
        "conv_b": nrm(ks[11], (DEPTH, CONV_CH), 0.02),
        "conv_ln_g": 1.0 + nrm(ks[12], (DEPTH, CONV_CH), 0.05),
        "conv_ln_b": nrm(ks[13], (DEPTH, CONV_CH), 0.02),
        "w_out": nrm(ks[14], (DEPTH, RET_W + CONV_CH, D_MODEL), (RET_W + CONV_CH) ** -0.5),
        "w_ffn_in": nrm(ks[15], (DEPTH, D_MODEL, 2 * D_FF), D_MODEL ** -0.5),
        "w_ffn_out": nrm(ks[16], (DEPTH, D_FF, D_MODEL), D_FF ** -0.5),
    }


def reference(x_prompt, x_sample, state_ret, state_conv, meta_tokens, norm_mix_pre, norm_mix_post,
              norm_ffn_pre, norm_ffn_post, w_in, conv_w, conv_b, conv_ln_g, conv_ln_b, w_out,
              w_ffn_in, w_ffn_out):
    B = x_prompt.shape[0]
    meta = jnp.broadcast_to(meta_tokens.astype(x_prompt.dtype)[None], (B, N_META, D_MODEL))
    hp = jnp.concatenate([meta, x_prompt], axis=1)
    pos_p = jnp.arange(SEQ + N_META)
    hs = x_sample
    pos_s = PAST_LEN + jnp.arange(x_sample.shape[1])
    ret_p_list, conv_p_list, ret_s_list, conv_s_list = [], [], [], []
    for l in range(DEPTH):
        lw = (norm_mix_pre[l], norm_mix_post[l], norm_ffn_pre[l], norm_ffn_post[l], w_in[l], conv_w[l],
              conv_b[l], conv_ln_g[l], conv_ln_b[l], w_out[l], w_ffn_in[l], w_ffn_out[l])
        zero_buf = jnp.zeros((B, CONV_WIDTH - 1, CONV_CH), hp.dtype)
        hp, r_p, c_p = layer(hp, pos_p, None, zero_buf, *lw)
        hs, r_s, c_s = layer(hs, pos_s, state_ret[l], state_conv[l], *lw)
        ret_p_list.append(r_p); conv_p_list.append(c_p)
        ret_s_list.append(r_s); conv_s_list.append(c_s)
    y_prompt = hp[:, N_META:]
    y_sample = hs
    ret_prompt = jnp.stack(ret_p_list)
    conv_prompt = jnp.stack(conv_p_list)
    ret_sample = jnp.stack(ret_s_list)
    conv_sample = jnp.stack(conv_s_list)
    return (y_prompt, y_sample, ret_prompt, conv_prompt, ret_sample, conv_sample)
```

```python
import functools

import jax
import jax.numpy as jnp
from jax import lax
from jax.experimental import pallas as pl
from jax.experimental.pallas import tpu as pltpu

F32 = jnp.float32
BF16 = jnp.bfloat16

D_MODEL = 1024
N_META = 16
RET_HEADS = 4
HEAD_DIM = 128
RET_W = RET_HEADS * HEAD_DIM
CONV_CH = D_MODEL - RET_W
CONV_WIDTH = 31
CONV_TAIL = CONV_WIDTH - 1
CHUNK = 128
ROPE_BASE = 10000.0
D_FF = 2816
IN_COLS = 4 * RET_W + 2 * CONV_CH
EPS = 1e-6
K_SCALE = HEAD_DIM ** -0.5

SUBLANES = 8
TAIL_ROWS = 32
TAIL_SKIP = TAIL_ROWS - CONV_TAIL

PROMPT_TILE = 512
CONV_ROW_BLOCK = 32
SAMPLE_BLOCK = 8
FFN_TILE = 256
MIB = 1024 * 1024


def _rms(x, w):
    return x * lax.rsqrt(jnp.mean(x * x, axis=-1, keepdims=True) + EPS) * w


def _rot(xh, cos2, sin2):
    return xh * cos2 + pltpu.roll(xh, HEAD_DIM // 2, 1) * sin2


def _head(i):
    return slice(i * HEAD_DIM, (i + 1) * HEAD_DIM)


def _dot(a, b):
    return jnp.dot(a, b, preferred_element_type=F32)


def _dot_nt(a, b):
    return lax.dot_general(a, b, (((1,), (1,)), ((), ())), preferred_element_type=F32)


def _dot_tn(a, b):
    return lax.dot_general(a, b, (((0,), (0,)), ((), ())), preferred_element_type=F32)


def _layer_norm_swish(y, g, b):
    mu = jnp.mean(y, axis=-1, keepdims=True)
    var = jnp.mean(jnp.square(y - mu), axis=-1, keepdims=True)
    yn = (y - mu) * lax.rsqrt(var + EPS) * g + b
    return jax.nn.silu(yn)


def _small_inproj_kernel(x_ref, nw_ref, w_ref, cos_ref, sin_ref, de_ref,
                         v_ref, g_ref, u_ref, qt_ref, kt_ref, r0_ref, tail0_ref,
                         p_scr):
    n = v_ref.shape[0]
    xn = _rms(x_ref[...], nw_ref[...]).astype(BF16)
    p_scr[...] = _dot(xn, w_ref[...])
    dec = slice(0, n)
    met = slice(n, n + CHUNK)
    for h in range(RET_HEADS):
        q = _rot(p_scr[dec, _head(h)], cos_ref[dec, :], sin_ref[dec, :])
        k = _rot(p_scr[dec, _head(RET_HEADS + h)], cos_ref[dec, :], sin_ref[dec, :]) * K_SCALE
        qt_ref[_head(h), :] = q.T
        kt_ref[_head(h), :] = k.T
        km = _rot(p_scr[met, _head(RET_HEADS + h)], cos_ref[met, :], sin_ref[met, :]) * K_SCALE
        vm = p_scr[met, _head(2 * RET_HEADS + h)]
        r0_ref[h] = _dot_tn((km * de_ref[h]).astype(BF16), vm.astype(BF16))
    v_ref[...] = p_scr[dec, 2 * RET_W:3 * RET_W]
    g_ref[...] = p_scr[dec, 3 * RET_W:4 * RET_W]
    a = p_scr[:, 4 * RET_W:4 * RET_W + CONV_CH]
    b = p_scr[:, 4 * RET_W + CONV_CH:]
    u = a * jax.nn.sigmoid(b)
    u_ref[...] = u[dec, :]
    tail0_ref[...] = u[n + CHUNK - TAIL_ROWS:n + CHUNK, :]


def _small_inproj(xs, nw, w_in, cos_s, sin_s, dec_end, n_dec):
    rows = xs.shape[0]
    full = lambda shape: pl.BlockSpec(shape, lambda i: (0,) * len(shape))
    out_shapes = (
        jax.ShapeDtypeStruct((n_dec, RET_W), F32),
        jax.ShapeDtypeStruct((n_dec, RET_W), F32),
        jax.ShapeDtypeStruct((n_dec, CONV_CH), F32),
        jax.ShapeDtypeStruct((RET_W, n_dec), F32),
        jax.ShapeDtypeStruct((RET_W, n_dec), F32),
        jax.ShapeDtypeStruct((RET_HEADS, HEAD_DIM, HEAD_DIM), F32),
        jax.ShapeDtypeStruct((TAIL_ROWS, CONV_CH), F32),
    )
    return pl.pallas_call(
        _small_inproj_kernel,
        grid=(1,),
        in_specs=[full(xs.shape), full(nw.shape), full(w_in.shape), full(cos_s.shape),
                  full(sin_s.shape), full(dec_end.shape)],
        out_specs=tuple(full(s.shape) for s in out_shapes),
        out_shape=out_shapes,
        scratch_shapes=[pltpu.VMEM((rows, IN_COLS), F32)],
        compiler_params=pltpu.CompilerParams(
            dimension_semantics=("arbitrary",), vmem_limit_bytes=40 * MIB),
        name="small_inproj",
    )(xs, nw, w_in, cos_s, sin_s, dec_end)


def _prompt_mix_kernel(x_ref, nw_ref, w_ref, cos_ref, sin_ref, mask_ref, ds_ref, de_ref, cd_ref,
                       r0_ref, tail0_ref, cw_ref, cb_ref, lg_ref, lb_ref,
                       mix_ref, retp_ref, convp_ref,
                       p_scr, r_scr, u_scr, y_scr):
    tile = x_ref.shape[0]
    j = pl.program_id(1)

    @pl.when(j == 0)
    def _():
        r_scr[...] = r0_ref[...]
        u_scr[0:TAIL_ROWS, :] = tail0_ref[...]

    xn = _rms(x_ref[...], nw_ref[...]).astype(BF16)
    p_scr[...] = _dot(xn, w_ref[...])

    a = p_scr[:, 4 * RET_W:4 * RET_W + CONV_CH]
    b = p_scr[:, 4 * RET_W + CONV_CH:]
    u_scr[TAIL_ROWS:TAIL_ROWS + tile, :] = a * jax.nn.sigmoid(b)
    for c in range(CONV_CH // HEAD_DIM):
        lanes = _head(c)
        taps = [cw_ref[t:t + 1, lanes] for t in range(CONV_WIDTH)]
        bias = cb_ref[:, lanes]
        for rb in range(tile // CONV_ROW_BLOCK):
            base = TAIL_SKIP + rb * CONV_ROW_BLOCK
            acc = None
            for s in range(SUBLANES):
                steps = (CONV_WIDTH - 1 - s) // SUBLANES + 1
                span = CONV_ROW_BLOCK + SUBLANES * (steps - 1)
                win = u_scr[base + s:base + s + span, lanes]
                for m in range(steps):
                    term = taps[s + SUBLANES * m] * win[SUBLANES * m:SUBLANES * m + CONV_ROW_BLOCK, :]
                    acc = term if acc is None else acc + term
            y_scr[rb * CONV_ROW_BLOCK:(rb + 1) * CONV_ROW_BLOCK, lanes] = acc + bias
    mix_ref[:, RET_W:] = _layer_norm_swish(y_scr[...], lg_ref[...], lb_ref[...]).astype(BF16)
    u_scr[0:TAIL_ROWS, :] = u_scr[tile:tile + TAIL_ROWS, :]

    for c in range(tile // CHUNK):
        rows = slice(c * CHUNK, (c + 1) * CHUNK)
        cos2 = cos_ref[rows, :]
        sin2 = sin_ref[rows, :]
        for h in range(RET_HEADS):
            q = _rot(p_scr[rows, _head(h)], cos2, sin2)
            k = _rot(p_scr[rows, _head(RET_HEADS + h)], cos2, sin2) * K_SCALE
            vb = p_scr[rows, _head(2 * RET_HEADS + h)].astype(BF16)
            gate = p_scr[rows, _head(3 * RET_HEADS + h)]
            qb = q.astype(BF16)
            scores = _dot_nt(qb, k.astype(BF16)) * mask_ref[h]
            r_prev = r_scr[h]
            o = _dot(scores.astype(BF16), vb) + _dot(qb, r_prev.astype(BF16)) * ds_ref[h]
            kv = _dot_tn((k * de_ref[h]).astype(BF16), vb)
            r_scr[h] = cd_ref[h] * r_prev + kv
            o = o * lax.rsqrt(jnp.mean(o * o, axis=-1, keepdims=True) + EPS)
            mix_ref[rows, _head(h)] = (o * jax.nn.silu(gate)).astype(BF16)

    @pl.when(j == pl.num_programs(1) - 1)
    def _():
        retp_ref[...] = r_scr[...]
        convp_ref[...] = u_scr[TAIL_SKIP:TAIL_ROWS, :]


def _prompt_mix(x2d, nw, w_in, cos_p, sin_p, mask, dec_start, dec_end, chunk_dec, r0, tail0,
                conv_w, conv_b, ln_g, ln_b, batch, seq):
    tile = PROMPT_TILE
    nt = seq // tile
    const = lambda shape: pl.BlockSpec(shape, lambda b, j: (0,) * len(shape))
    out_shapes = (
        jax.ShapeDtypeStruct((batch * seq, D_MODEL), BF16),
        jax.ShapeDtypeStruct((batch, RET_HEADS, HEAD_DIM, HEAD_DIM), F32),
        jax.ShapeDtypeStruct((batch, CONV_TAIL, CONV_CH), F32),
    )
    return pl.pallas_call(
        _prompt_mix_kernel,
        grid=(batch, nt),
        in_specs=[
            pl.BlockSpec((tile, D_MODEL), lambda b, j: (b * nt + j, 0)),
            const(nw.shape), const(w_in.shape),
            pl.BlockSpec((tile, HEAD_DIM), lambda b, j: (j, 0)),
            pl.BlockSpec((tile, HEAD_DIM), lambda b, j: (j, 0)),
            const(mask.shape), const(dec_start.shape), const(dec_end.shape), const(chunk_dec.shape),
            const(r0.shape), const(tail0.shape), const(conv_w.shape), const(conv_b.shape),
            const(ln_g.shape), const(ln_b.shape),
        ],
        out_specs=(
            pl.BlockSpec((tile, D_MODEL), lambda b, j: (b * nt + j, 0)),
            pl.BlockSpec((None, RET_HEADS, HEAD_DIM, HEAD_DIM), lambda b, j: (b, 0, 0, 0)),
            pl.BlockSpec((None, CONV_TAIL, CONV_CH), lambda b, j: (b, 0, 0)),
        ),
        out_shape=out_shapes,
        scratch_shapes=[
            pltpu.VMEM((tile, IN_COLS), F32),
            pltpu.VMEM((RET_HEADS, HEAD_DIM, HEAD_DIM), F32),
            pltpu.VMEM((TAIL_ROWS + tile, CONV_CH), F32),
            pltpu.VMEM((tile, CONV_CH), F32),
        ],
        compiler_params=pltpu.CompilerParams(
            dimension_semantics=("arbitrary", "arbitrary"), vmem_limit_bytes=48 * MIB),
        name="prompt_mix",
    )(x2d, nw, w_in, cos_p, sin_p, mask, dec_start, dec_end, chunk_dec, r0, tail0,
      conv_w, conv_b, ln_g, ln_b)


def _sample_mix_kernel(qt_ref, kt_ref, v_ref, g_ref, u_ref, gam_ref, sr_ref, sc_ref,
                       cw_ref, cb_ref, lg_ref, lb_ref,
                       mix_ref, so_ref, sco_ref, o_scr):
    nb = v_ref.shape[0]
    for t in range(nb):
        for h in range(RET_HEADS):
            kc = kt_ref[_head(h), t:t + 1]
            qc = qt_ref[_head(h), t:t + 1]
            vr = v_ref[t:t + 1, _head(h)]
            s_new = gam_ref[h:h + 1, :] * sr_ref[t, h] + kc * vr
            so_ref[t, h] = s_new
            o_scr[t:t + 1, _head(h)] = jnp.sum(qc * s_new, axis=0, keepdims=True)
    for h in range(RET_HEADS):
        o = o_scr[:, _head(h)]
        o = o * lax.rsqrt(jnp.mean(o * o, axis=-1, keepdims=True) + EPS)
        mix_ref[:, _head(h)] = o * jax.nn.silu(g_ref[:, _head(h)])
    u = u_ref[...]
    y = jnp.sum(sc_ref[...] * cw_ref[0:CONV_TAIL, :], axis=1)
    y = y + u * cw_ref[CONV_TAIL:CONV_WIDTH, :] + cb_ref[...]
    mix_ref[:, RET_W:] = _layer_norm_swish(y, lg_ref[...], lb_ref[...])
    sco_ref[:, 0:CONV_TAIL - 1, :] = sc_ref[:, 1:CONV_TAIL, :]
    for t in range(nb):
        sco_ref[t, CONV_TAIL - 1:CONV_TAIL, :] = u_ref[t:t + 1, :]


def _sample_mix(qt3, kt3, v, g, u, gam, state_ret, state_conv, conv_w, conv_b, ln_g, ln_b):
    n = v.shape[0]
    nb = SAMPLE_BLOCK
    const = lambda shape: pl.BlockSpec(shape, lambda i: (0,) * len(shape))
    rows = lambda width: pl.BlockSpec((nb, width), lambda i: (i, 0))
    out_shapes = (
        jax.ShapeDtypeStruct((n, D_MODEL), F32),
        jax.ShapeDtypeStruct(state_ret.shape, F32),
        jax.ShapeDtypeStruct(state_conv.shape, F32),
    )
    st_spec = pl.BlockSpec((nb, RET_HEADS, HEAD_DIM, HEAD_DIM), lambda i: (i, 0, 0, 0))
    cv_spec = pl.BlockSpec((nb, CONV_TAIL, CONV_CH), lambda i: (i, 0, 0))
    return pl.pallas_call(
        _sample_mix_kernel,
        grid=(n // nb,),
        in_specs=[
            pl.BlockSpec((None, RET_W, nb), lambda i: (i, 0, 0)),
            pl.BlockSpec((None, RET_W, nb), lambda i: (i, 0, 0)),
            rows(RET_W), rows(RET_W), rows(CONV_CH), const(gam.shape), st_spec, cv_spec,
            const(conv_w.shape), const(conv_b.shape), const(ln_g.shape), const(ln_b.shape),
        ],
        out_specs=(rows(D_MODEL), st_spec, cv_spec),
        out_shape=out_shapes,
        scratch_shapes=[pltpu.VMEM((nb, RET_W), F32)],
        compiler_params=pltpu.CompilerParams(
            dimension_semantics=("arbitrary",), vmem_limit_bytes=32 * MIB),
        name="sample_mix",
    )(qt3, kt3, v, g, u, gam, state_ret, state_conv, conv_w, conv_b, ln_g, ln_b)


def _out_ffn_kernel(mix_ref, x_ref, wo_ref, wi_ref, wd_ref, npost_ref, nfpre_ref, nfpost_ref, y_ref):
    a = _dot(mix_ref[...].astype(BF16), wo_ref[...])
    h1 = x_ref[...] + _rms(a, npost_ref[...])
    xf = _rms(h1, nfpre_ref[...]).astype(BF16)
    gu = _dot(xf, wi_ref[...])
    act = (jax.nn.silu(gu[:, :D_FF]) * gu[:, D_FF:]).astype(BF16)
    dn = _dot(act, wd_ref[...])
    y_ref[...] = h1 + _rms(dn, nfpost_ref[...])


def _out_ffn(mix, x2d, w_out, w_ffn_in, w_ffn_out, n_post, n_fpre, n_fpost, tile):
    n = x2d.shape[0]
    const = lambda shape: pl.BlockSpec(shape, lambda i: (0,) * len(shape),
                                       pipeline_mode=pl.Buffered(1))
    rows = pl.BlockSpec((tile, D_MODEL), lambda i: (i, 0))
    return pl.pallas_call(
        _out_ffn_kernel,
        grid=(n // tile,),
        in_specs=[rows, rows, const(w_out.shape), const(w_ffn_in.shape), const(w_ffn_out.shape),
                  const(n_post.shape), const(n_fpre.shape), const(n_fpost.shape)],
        out_specs=rows,
        out_shape=jax.ShapeDtypeStruct((n, D_MODEL), F32),
        compiler_params=pltpu.CompilerParams(
            dimension_semantics=("arbitrary",), vmem_limit_bytes=52 * MIB),
        name="out_ffn",
    )(mix, x2d, w_out, w_ffn_in, w_ffn_out, n_post, n_fpre, n_fpost)


def _rotary_tables(pos):
    half = HEAD_DIM // 2
    inv = 1.0 / (ROPE_BASE ** jnp.linspace(0.0, 1.0, half, dtype=F32))
    ang = pos.astype(F32)[:, None] * inv[None, :]
    cos, sin = jnp.cos(ang), jnp.sin(ang)
    return jnp.concatenate([cos, cos], axis=-1), jnp.concatenate([-sin, sin], axis=-1)


def _decay_tables():
    lg = jnp.log(1.0 - 2.0 ** (-5.0 - jnp.arange(RET_HEADS, dtype=F32)))
    idx = jnp.arange(CHUNK, dtype=F32)
    diff = idx[:, None] - idx[None, :]
    mask = jnp.where(diff[None] >= 0, jnp.exp(lg[:, None, None] * jnp.maximum(diff, 0.0)[None]), 0.0)
    wide = lambda col: jnp.broadcast_to(col[:, :, None], (RET_HEADS, CHUNK, HEAD_DIM))
    dec_end = wide(jnp.exp(lg[:, None] * (CHUNK - 1 - idx)[None, :]))
    dec_start = wide(jnp.exp(lg[:, None] * (idx + 1.0)[None, :]))
    chunk_dec = jnp.broadcast_to(jnp.exp(lg * CHUNK)[:, None, None], (RET_HEADS, HEAD_DIM, HEAD_DIM))
    gam = jnp.broadcast_to(jnp.exp(lg)[:, None], (RET_HEADS, HEAD_DIM))
    return mask, dec_start, dec_end, chunk_dec, gam


def kernel(x_prompt, x_sample, state_ret, state_conv, meta_tokens, norm_mix_pre, norm_mix_post,
           norm_ffn_pre, norm_ffn_post, w_in, conv_w, conv_b, conv_ln_g, conv_ln_b, w_out,
           w_ffn_in, w_ffn_out):
    batch, seq, _ = x_prompt.shape
    n_dec, dec_seq, _ = x_sample.shape
    depth = w_in.shape[0]
    assert depth == 1 and dec_seq == 1 and seq % PROMPT_TILE == 0
    assert n_dec % SAMPLE_BLOCK == 0 and (batch * seq) % FFN_TILE == 0

    row = lambda v: v.reshape(1, -1)
    nw_pre, nw_post = row(norm_mix_pre[0]), row(norm_mix_post[0])
    nf_pre, nf_post = row(norm_ffn_pre[0]), row(norm_ffn_post[0])
    w_in_b = w_in[0].astype(BF16)
    w_out_b = w_out[0].astype(BF16)
    w_fi_b = w_ffn_in[0].astype(BF16)
    w_fo_b = w_ffn_out[0].astype(BF16)
    cw, cb, lg, lb = conv_w[0], row(conv_b[0]), row(conv_ln_g[0]), row(conv_ln_b[0])

    mask, dec_start, dec_end, chunk_dec, gam = _decay_tables()
    past_len = 16384
    pos_small = jnp.concatenate([
        jnp.full((n_dec,), past_len, jnp.int32),
        jnp.zeros((CHUNK - N_META,), jnp.int32),
        jnp.arange(N_META, dtype=jnp.int32)])
    cos_s, sin_s = _rotary_tables(pos_small)
    cos_p, sin_p = _rotary_tables(N_META + jnp.arange(seq))

    xs = jnp.concatenate([
        x_sample.reshape(n_dec, D_MODEL),
        jnp.zeros((CHUNK - N_META, D_MODEL), x_sample.dtype),
        meta_tokens.astype(x_prompt.dtype)], axis=0)
    v_s, g_s, u_s, qt, kt, r0, tail0 = _small_inproj(
        xs, nw_pre, w_in_b, cos_s, sin_s, dec_end, n_dec)

    x2d = x_prompt.reshape(batch * seq, D_MODEL)
    mix_p, ret_p, conv_p = _prompt_mix(
        x2d, nw_pre, w_in_b, cos_p, sin_p, mask, dec_start, dec_end, chunk_dec, r0, tail0,
        cw, cb, lg, lb, batch, seq)

    blocks = n_dec // SAMPLE_BLOCK
    to_blocks = lambda t: t.reshape(RET_W, blocks, SAMPLE_BLOCK).transpose(1, 0, 2)
    mix_s, ret_s, conv_s = _sample_mix(
        to_blocks(qt), to_blocks(kt), v_s, g_s, u_s, gam, state_ret[0], state_conv[0],
        cw, cb, lg, lb)

    y_p = _out_ffn(mix_p, x2d, w_out_b, w_fi_b, w_fo_b, nw_post, nf_pre, nf_post, FFN_TILE)
    y_s = _out_ffn(mix_s, x_sample.reshape(n_dec, D_MODEL), w_out_b, w_fi_b, w_fo_b,
                   nw_post, nf_pre, nf_post, n_dec)

    return (y_p.reshape(batch, seq, D_MODEL), y_s.reshape(n_dec, 1, D_MODEL),
            ret_p[None], conv_p[None], ret_s[None], conv_s[None])
```

```python
import functools

import jax
import jax.numpy as jnp
from jax import lax
from jax.experimental import pallas as pl
from jax.experimental.pallas import tpu as pltpu

F32 = jnp.float32
BF16 = jnp.bfloat16

D_MODEL = 1024
N_META = 16
RET_HEADS = 4
HEAD_DIM = 128
RET_W = RET_HEADS * HEAD_DIM
CONV_CH = D_MODEL - RET_W
CONV_WIDTH = 31
CONV_TAIL = CONV_WIDTH - 1
CHUNK = 128
ROPE_BASE = 10000.0
D_FF = 2816
IN_COLS = 4 * RET_W + 2 * CONV_CH
EPS = 1e-6
K_SCALE = HEAD_DIM ** -0.5

SUBLANES = 8
TAIL_ROWS = 32
TAIL_SKIP = TAIL_ROWS - CONV_TAIL

PROMPT_TILE = 512
CONV_ROW_STRIDE = 4
SAMPLE_BLOCK = 8
FFN_TILE = 512
FFN_CHUNK = 1024
MIB = 1024 * 1024


def _rms(x, w):
    return x * lax.rsqrt(jnp.mean(x * x, axis=-1, keepdims=True) + EPS) * w


def _rot(xh, cos2, sin2):
    return xh * cos2 + pltpu.roll(xh, HEAD_DIM // 2, 1) * sin2


def _head(i):
    return slice(i * HEAD_DIM, (i + 1) * HEAD_DIM)


def _dot(a, b):
    return jnp.dot(a, b, preferred_element_type=F32)


def _dot_nt(a, b):
    return lax.dot_general(a, b, (((1,), (1,)), ((), ())), preferred_element_type=F32)


def _dot_tn(a, b):
    return lax.dot_general(a, b, (((0,), (0,)), ((), ())), preferred_element_type=F32)


def _layer_norm_swish(y, g, b):
    mu = jnp.mean(y, axis=-1, keepdims=True)
    var = jnp.mean(jnp.square(y - mu), axis=-1, keepdims=True)
    yn = (y - mu) * lax.rsqrt(var + EPS) * g + b
    return jax.nn.silu(yn)


def _small_inproj_kernel(x_ref, nw_ref, w_ref, cos_ref, sin_ref, de_ref,
                         v_ref, g_ref, u_ref, qt_ref, kt_ref, r0_ref, tail0_ref,
                         p_scr):
    n = v_ref.shape[0]
    xn = _rms(x_ref[...], nw_ref[...]).astype(BF16)
    p_scr[...] = _dot(xn, w_ref[...])
    dec = slice(0, n)
    met = slice(n, n + CHUNK)
    for h in range(RET_HEADS):
        q = _rot(p_scr[dec, _head(h)], cos_ref[dec, :], sin_ref[dec, :])
        k = _rot(p_scr[dec, _head(RET_HEADS + h)], cos_ref[dec, :], sin_ref[dec, :]) * K_SCALE
        qt_ref[_head(h), :] = q.T
        kt_ref[_head(h), :] = k.T
        km = _rot(p_scr[met, _head(RET_HEADS + h)], cos_ref[met, :], sin_ref[met, :]) * K_SCALE
        vm = p_scr[met, _head(2 * RET_HEADS + h)]
        r0_ref[h] = _dot_tn((km * de_ref[h]).astype(BF16), vm.astype(BF16))
    v_ref[...] = p_scr[dec, 2 * RET_W:3 * RET_W]
    g_ref[...] = p_scr[dec, 3 * RET_W:4 * RET_W]
    a = p_scr[:, 4 * RET_W:4 * RET_W + CONV_CH]
    b = p_scr[:, 4 * RET_W + CONV_CH:]
    u = a * jax.nn.sigmoid(b)
    u_ref[...] = u[dec, :]
    tail0_ref[...] = u[n + CHUNK - TAIL_ROWS:n + CHUNK, :]


def _small_inproj(xs, nw, w_in, cos_s, sin_s, dec_end, n_dec):
    rows = xs.shape[0]
    full = lambda shape: pl.BlockSpec(shape, lambda i: (0,) * len(shape))
    out_shapes = (
        jax.ShapeDtypeStruct((n_dec, RET_W), F32),
        jax.ShapeDtypeStruct((n_dec, RET_W), F32),
        jax.ShapeDtypeStruct((n_dec, CONV_CH), F32),
        jax.ShapeDtypeStruct((RET_W, n_dec), F32),
        jax.ShapeDtypeStruct((RET_W, n_dec), F32),
        jax.ShapeDtypeStruct((RET_HEADS, HEAD_DIM, HEAD_DIM), F32),
        jax.ShapeDtypeStruct((TAIL_ROWS, CONV_CH), F32),
    )
    return pl.pallas_call(
        _small_inproj_kernel,
        grid=(1,),
        in_specs=[full(xs.shape), full(nw.shape), full(w_in.shape), full(cos_s.shape),
                  full(sin_s.shape), full(dec_end.shape)],
        out_specs=tuple(full(s.shape) for s in out_shapes),
        out_shape=out_shapes,
        scratch_shapes=[pltpu.VMEM((rows, IN_COLS), F32)],
        compiler_params=pltpu.CompilerParams(
            dimension_semantics=("arbitrary",), vmem_limit_bytes=40 * MIB),
        name="small_inproj",
    )(xs, nw, w_in, cos_s, sin_s, dec_end)


def _prompt_mix_kernel(x_ref, nw_ref, w_ref, cos_ref, sin_ref, mask_ref, ds_ref, de_ref, cd_ref,
                       r0_ref, tail0_ref, cw_ref, cb_ref, lg_ref, lb_ref,
                       mix_ref, retp_ref, convp_ref,
                       p_scr, r_scr, u_scr, y_scr):
    tile = x_ref.shape[0]
    j = pl.program_id(1)

    n_slabs = CONV_CH // HEAD_DIM

    @pl.when(j == 0)
    def _():
        r_scr[...] = r0_ref[...]
        for c in range(n_slabs):
            u_scr[c, 0:TAIL_ROWS, :] = tail0_ref[:, _head(c)]

    xn = _rms(x_ref[...], nw_ref[...]).astype(BF16)

    ab = _dot(xn, w_ref[:, 4 * RET_W:])
    u = ab[:, :CONV_CH] * jax.nn.sigmoid(ab[:, CONV_CH:])
    for c in range(n_slabs):
        u_scr[c, TAIL_ROWS:TAIL_ROWS + tile, :] = u[:, _head(c)]
    p_scr[...] = _dot(xn, w_ref[:, :4 * RET_W])
    block = CONV_ROW_STRIDE * SUBLANES
    for c in range(n_slabs):
        taps = [cw_ref[t:t + 1, _head(c)] for t in range(CONV_WIDTH)]
        bias = cb_ref[:, _head(c)]
        for b0 in range(0, tile, block):
            acc = [None] * CONV_ROW_STRIDE
            for p in range(CONV_ROW_STRIDE + CONV_WIDTH - 1):
                win = u_scr[c, pl.ds(TAIL_SKIP + b0 + p, SUBLANES, stride=CONV_ROW_STRIDE), :]
                for r in range(CONV_ROW_STRIDE):
                    t = p - r
                    if 0 <= t < CONV_WIDTH:
                        term = taps[t] * win
                        acc[r] = term if acc[r] is None else acc[r] + term
            for r in range(CONV_ROW_STRIDE):
                y_scr[c, pl.ds(b0 + r, SUBLANES, stride=CONV_ROW_STRIDE), :] = acc[r] + bias
    ys = [y_scr[c] for c in range(n_slabs)]
    mu = sum(jnp.sum(y, axis=-1, keepdims=True) for y in ys) * (1.0 / CONV_CH)
    dev = [y - mu for y in ys]
    var = sum(jnp.sum(d * d, axis=-1, keepdims=True) for d in dev) * (1.0 / CONV_CH)
    inv = lax.rsqrt(var + EPS)
    for c in range(n_slabs):
        yn = dev[c] * inv * lg_ref[:, _head(c)] + lb_ref[:, _head(c)]
        mix_ref[:, RET_W + c * HEAD_DIM:RET_W + (c + 1) * HEAD_DIM] = jax.nn.silu(yn).astype(BF16)
        u_scr[c, 0:TAIL_ROWS, :] = u_scr[c, tile:tile + TAIL_ROWS, :]

    for c in range(tile // CHUNK):
        rows = slice(c * CHUNK, (c + 1) * CHUNK)
        cos2 = cos_ref[rows, :]
        sin2 = sin_ref[rows, :]
        for h in range(RET_HEADS):
            q = _rot(p_scr[rows, _head(h)], cos2, sin2)
            k = _rot(p_scr[rows, _head(RET_HEADS + h)], cos2, sin2) * K_SCALE
            vb = p_scr[rows, _head(2 * RET_HEADS + h)].astype(BF16)
            gate = p_scr[rows, _head(3 * RET_HEADS + h)]
            qb = q.astype(BF16)
            scores = _dot_nt(qb, k.astype(BF16)) * mask_ref[h]
            r_prev = r_scr[h]
            o = _dot(scores.astype(BF16), vb) + _dot(qb, r_prev.astype(BF16)) * ds_ref[h]
            kv = _dot_tn((k * de_ref[h]).astype(BF16), vb)
            r_scr[h] = cd_ref[h] * r_prev + kv
            o = o * lax.rsqrt(jnp.mean(o * o, axis=-1, keepdims=True) + EPS)
            mix_ref[rows, _head(h)] = (o * jax.nn.silu(gate)).astype(BF16)

    @pl.when(j == pl.num_programs(1) - 1)
    def _():
        retp_ref[...] = r_scr[...]
        for c in range(n_slabs):
            convp_ref[:, _head(c)] = u_scr[c, TAIL_SKIP:TAIL_ROWS, :]


def _prompt_mix(x2d, nw, w_in, cos_p, sin_p, mask, dec_start, dec_end, chunk_dec, r0, tail0,
                conv_w, conv_b, ln_g, ln_b, batch, seq):
    tile = PROMPT_TILE
    nt = seq // tile
    const = lambda shape: pl.BlockSpec(shape, lambda b, j: (0,) * len(shape))
    out_shapes = (
        jax.ShapeDtypeStruct((batch * seq, D_MODEL), BF16),
        jax.ShapeDtypeStruct((batch, RET_HEADS, HEAD_DIM, HEAD_DIM), F32),
        jax.ShapeDtypeStruct((batch, CONV_TAIL, CONV_CH), F32),
    )
    return pl.pallas_call(
        _prompt_mix_kernel,
        grid=(batch, nt),
        in_specs=[
            pl.BlockSpec((tile, D_MODEL), lambda b, j: (b * nt + j, 0)),
            const(nw.shape), const(w_in.shape),
            pl.BlockSpec((tile, HEAD_DIM), lambda b, j: (j, 0)),
            pl.BlockSpec((tile, HEAD_DIM), lambda b, j: (j, 0)),
            const(mask.shape), const(dec_start.shape), const(dec_end.shape), const(chunk_dec.shape),
            const(r0.shape), const(tail0.shape), const(conv_w.shape), const(conv_b.shape),
            const(ln_g.shape), const(ln_b.shape),
        ],
        out_specs=(
            pl.BlockSpec((tile, D_MODEL), lambda b, j: (b * nt + j, 0)),
            pl.BlockSpec((None, RET_HEADS, HEAD_DIM, HEAD_DIM), lambda b, j: (b, 0, 0, 0)),
            pl.BlockSpec((None, CONV_TAIL, CONV_CH), lambda b, j: (b, 0, 0)),
        ),
        out_shape=out_shapes,
        scratch_shapes=[
            pltpu.VMEM((tile, 4 * RET_W), F32),
            pltpu.VMEM((RET_HEADS, HEAD_DIM, HEAD_DIM), F32),
            pltpu.VMEM((CONV_CH // HEAD_DIM, TAIL_ROWS + tile, HEAD_DIM), F32),
            pltpu.VMEM((CONV_CH // HEAD_DIM, tile, HEAD_DIM), F32),
        ],
        compiler_params=pltpu.CompilerParams(
            dimension_semantics=("arbitrary", "arbitrary"), vmem_limit_bytes=48 * MIB),
        name="prompt_mix",
    )(x2d, nw, w_in, cos_p, sin_p, mask, dec_start, dec_end, chunk_dec, r0, tail0,
      conv_w, conv_b, ln_g, ln_b)


def _sample_mix_kernel(qt_ref, kt_ref, v_ref, g_ref, u_ref, gam_ref, sr_ref, sc_ref,
                       cw_ref, cb_ref, lg_ref, lb_ref,
                       mix_ref, so_ref, sco_ref, o_scr):
    nb = v_ref.shape[0]
    for t in range(nb):
        for h in range(RET_HEADS):
            kc = kt_ref[_head(h), t:t + 1]
            qc = qt_ref[_head(h), t:t + 1]
            vr = v_ref[t:t + 1, _head(h)]
            s_new = gam_ref[h:h + 1, :] * sr_ref[t, h] + kc * vr
            so_ref[t, h] = s_new
            o_scr[t:t + 1, _head(h)] = jnp.sum(qc * s_new, axis=0, keepdims=True)
    for h in range(RET_HEADS):
        o = o_scr[:, _head(h)]
        o = o * lax.rsqrt(jnp.mean(o * o, axis=-1, keepdims=True) + EPS)
        mix_ref[:, _head(h)] = o * jax.nn.silu(g_ref[:, _head(h)])
    u = u_ref[...]
    y = jnp.sum(sc_ref[...] * cw_ref[0:CONV_TAIL, :], axis=1)
    y = y + u * cw_ref[CONV_TAIL:CONV_WIDTH, :] + cb_ref[...]
    mix_ref[:, RET_W:] = _layer_norm_swish(y, lg_ref[...], lb_ref[...])
    sco_ref[:, 0:CONV_TAIL - 1, :] = sc_ref[:, 1:CONV_TAIL, :]
    for t in range(nb):
        sco_ref[t, CONV_TAIL - 1:CONV_TAIL, :] = u_ref[t:t + 1, :]


def _sample_mix(qt3, kt3, v, g, u, gam, state_ret, state_conv, conv_w, conv_b, ln_g, ln_b):
    n = v.shape[0]
    nb = SAMPLE_BLOCK
    const = lambda shape: pl.BlockSpec(shape, lambda i: (0,) * len(shape))
    rows = lambda width: pl.BlockSpec((nb, width), lambda i: (i, 0))
    out_shapes = (
        jax.ShapeDtypeStruct((n, D_MODEL), F32),
        jax.ShapeDtypeStruct(state_ret.shape, F32),
        jax.ShapeDtypeStruct(state_conv.shape, F32),
    )
    st_spec = pl.BlockSpec((nb, RET_HEADS, HEAD_DIM, HEAD_DIM), lambda i: (i, 0, 0, 0))
    cv_spec = pl.BlockSpec((nb, CONV_TAIL, CONV_CH), lambda i: (i, 0, 0))
    return pl.pallas_call(
        _sample_mix_kernel,
        grid=(n // nb,),
        in_specs=[
            pl.BlockSpec((None, RET_W, nb), lambda i: (i, 0, 0)),
            pl.BlockSpec((None, RET_W, nb), lambda i: (i, 0, 0)),
            rows(RET_W), rows(RET_W), rows(CONV_CH), const(gam.shape), st_spec, cv_spec,
            const(conv_w.shape), const(conv_b.shape), const(ln_g.shape), const(ln_b.shape),
        ],
        out_specs=(rows(D_MODEL), st_spec, cv_spec),
        out_shape=out_shapes,
        scratch_shapes=[pltpu.VMEM((nb, RET_W), F32)],
        compiler_params=pltpu.CompilerParams(
            dimension_semantics=("arbitrary",), vmem_limit_bytes=32 * MIB),
        name="sample_mix",
    )(qt3, kt3, v, g, u, gam, state_ret, state_conv, conv_w, conv_b, ln_g, ln_b)


def _out_ffn_kernel(mix_ref, x_ref, wo_ref, wi_ref, wd_ref, npost_ref, nfpre_ref, nfpost_ref, y_ref):
    a = _dot(mix_ref[...].astype(BF16), wo_ref[...])
    h1 = x_ref[...] + _rms(a, npost_ref[...])
    xf = _rms(h1, nfpre_ref[...]).astype(BF16)
    dn = None
    for c0 in range(0, D_FF, FFN_CHUNK):
        c1 = min(c0 + FFN_CHUNK, D_FF)
        gate = _dot(xf, wi_ref[:, c0:c1])
        up = _dot(xf, wi_ref[:, D_FF + c0:D_FF + c1])
        part = _dot((jax.nn.silu(gate) * up).astype(BF16), wd_ref[c0:c1, :])
        dn = part if dn is None else dn + part
    y_ref[...] = h1 + _rms(dn, nfpost_ref[...])


def _out_ffn(mix, x2d, w_out, w_ffn_in, w_ffn_out, n_post, n_fpre, n_fpost, tile):
    n = x2d.shape[0]
    const = lambda shape: pl.BlockSpec(shape, lambda i: (0,) * len(shape),
                                       pipeline_mode=pl.Buffered(1))
    rows = pl.BlockSpec((tile, D_MODEL), lambda i: (i, 0))
    return pl.pallas_call(
        _out_ffn_kernel,
        grid=(n // tile,),
        in_specs=[rows, rows, const(w_out.shape), const(w_ffn_in.shape), const(w_ffn_out.shape),
                  const(n_post.shape), const(n_fpre.shape), const(n_fpost.shape)],
        out_specs=rows,
        out_shape=jax.ShapeDtypeStruct((n, D_MODEL), F32),
        compiler_params=pltpu.CompilerParams(
            dimension_semantics=("arbitrary",), vmem_limit_bytes=52 * MIB),
        name="out_ffn",
    )(mix, x2d, w_out, w_ffn_in, w_ffn_out, n_post, n_fpre, n_fpost)


def _rotary_tables(pos):
    half = HEAD_DIM // 2
    inv = 1.0 / (ROPE_BASE ** jnp.linspace(0.0, 1.0, half, dtype=F32))
    ang = pos.astype(F32)[:, None] * inv[None, :]
    cos, sin = jnp.cos(ang), jnp.sin(ang)
    return jnp.concatenate([cos, cos], axis=-1), jnp.concatenate([-sin, sin], axis=-1)


def _decay_tables():
    lg = jnp.log(1.0 - 2.0 ** (-5.0 - jnp.arange(RET_HEADS, dtype=F32)))
    idx = jnp.arange(CHUNK, dtype=F32)
    diff = idx[:, None] - idx[None, :]
    mask = jnp.where(diff[None] >= 0, jnp.exp(lg[:, None, None] * jnp.maximum(diff, 0.0)[None]), 0.0)
    wide = lambda col: jnp.broadcast_to(col[:, :, None], (RET_HEADS, CHUNK, HEAD_DIM))
    dec_end = wide(jnp.exp(lg[:, None] * (CHUNK - 1 - idx)[None, :]))
    dec_start = wide(jnp.exp(lg[:, None] * (idx + 1.0)[None, :]))
    chunk_dec = jnp.broadcast_to(jnp.exp(lg * CHUNK)[:, None, None], (RET_HEADS, HEAD_DIM, HEAD_DIM))
    gam = jnp.broadcast_to(jnp.exp(lg)[:, None], (RET_HEADS, HEAD_DIM))
    return mask, dec_start, dec_end, chunk_dec, gam


def kernel(x_prompt, x_sample, state_ret, state_conv, meta_tokens, norm_mix_pre, norm_mix_post,
           norm_ffn_pre, norm_ffn_post, w_in, conv_w, conv_b, conv_ln_g, conv_ln_b, w_out,
           w_ffn_in, w_ffn_out):
    batch, seq, _ = x_prompt.shape
    n_dec, dec_seq, _ = x_sample.shape
    depth = w_in.shape[0]
    assert depth == 1 and dec_seq == 1 and seq % PROMPT_TILE == 0
    assert n_dec % SAMPLE_BLOCK == 0 and (batch * seq) % FFN_TILE == 0

    row = lambda v: v.reshape(1, -1)
    nw_pre, nw_post = row(norm_mix_pre[0]), row(norm_mix_post[0])
    nf_pre, nf_post = row(norm_ffn_pre[0]), row(norm_ffn_post[0])
    w_in_b = w_in[0].astype(BF16)
    w_out_b = w_out[0].astype(BF16)
    w_fi_b = w_ffn_in[0].astype(BF16)
    w_fo_b = w_ffn_out[0].astype(BF16)
    cw, cb, lg, lb = conv_w[0], row(conv_b[0]), row(conv_ln_g[0]), row(conv_ln_b[0])

    mask, dec_start, dec_end, chunk_dec, gam = _decay_tables()
    past_len = 16384
    pos_small = jnp.concatenate([
        jnp.full((n_dec,), past_len, jnp.int32),
        jnp.zeros((CHUNK - N_META,), jnp.int32),
        jnp.arange(N_META, dtype=jnp.int32)])
    cos_s, sin_s = _rotary_tables(pos_small)
    cos_p, sin_p = _rotary_tables(N_META + jnp.arange(seq))

    xs = jnp.concatenate([
        x_sample.reshape(n_dec, D_MODEL),
        jnp.zeros((CHUNK - N_META, D_MODEL), x_sample.dtype),
        meta_tokens.astype(x_prompt.dtype)], axis=0)
    v_s, g_s, u_s, qt, kt, r0, tail0 = _small_inproj(
        xs, nw_pre, w_in_b, cos_s, sin_s, dec_end, n_dec)

    x2d = x_prompt.reshape(batch * seq, D_MODEL)
    mix_p, ret_p, conv_p = _prompt_mix(
        x2d, nw_pre, w_in_b, cos_p, sin_p, mask, dec_start, dec_end, chunk_dec, r0, tail0,
        cw, cb, lg, lb, batch, seq)

    blocks = n_dec // SAMPLE_BLOCK
    to_blocks = lambda t: t.reshape(RET_W, blocks, SAMPLE_BLOCK).transpose(1, 0, 2)
    mix_s, ret_s, conv_s = _sample_mix(
        to_blocks(qt), to_blocks(kt), v_s, g_s, u_s, gam, state_ret[0], state_conv[0],
        cw, cb, lg, lb)

    y_p = _out_ffn(mix_p, x2d, w_out_b, w_fi_b, w_fo_b, nw_post, nf_pre, nf_post, FFN_TILE)
    y_s = _out_ffn(mix_s, x_sample.reshape(n_dec, D_MODEL), w_out_b, w_fi_b, w_fo_b,
                   nw_post, nf_pre, nf_post, n_dec)

    return (y_p.reshape(batch, seq, D_MODEL), y_s.reshape(n_dec, 1, D_MODEL),
            ret_p[None], conv_p[None], ret_s[None], conv_s[None])
```

```python
import functools

import jax
import jax.numpy as jnp
from jax import lax
from jax.experimental import pallas as pl
from jax.experimental.pallas import tpu as pltpu

F32 = jnp.float32
BF16 = jnp.bfloat16

D_MODEL = 1024
N_META = 16
RET_HEADS = 4
HEAD_DIM = 128
RET_W = RET_HEADS * HEAD_DIM
CONV_CH = D_MODEL - RET_W
CONV_WIDTH = 31
CONV_TAIL = CONV_WIDTH - 1
CHUNK = 128
ROPE_BASE = 10000.0
D_FF = 2816
IN_COLS = 4 * RET_W + 2 * CONV_CH
EPS = 1e-6
K_SCALE = HEAD_DIM ** -0.5

SUBLANES = 8
TAIL_ROWS = 32
TAIL_SKIP = TAIL_ROWS - CONV_TAIL

PROMPT_TILE = 512
CONV_ROW_STRIDE = 4
SAMPLE_BLOCK = 16
FFN_TILE = 512
FFN_CHUNK = 1024
MIB = 1024 * 1024


def _rms(x, w):
    return x * lax.rsqrt(jnp.mean(x * x, axis=-1, keepdims=True) + EPS) * w


def _rot(xh, cos2, sin2):
    return xh * cos2 + pltpu.roll(xh, HEAD_DIM // 2, 1) * sin2


def _head(i):
    return slice(i * HEAD_DIM, (i + 1) * HEAD_DIM)


def _dot(a, b):
    return jnp.dot(a, b, preferred_element_type=F32)


def _dot_nt(a, b):
    return lax.dot_general(a, b, (((1,), (1,)), ((), ())), preferred_element_type=F32)


def _dot_tn(a, b):
    return lax.dot_general(a, b, (((0,), (0,)), ((), ())), preferred_element_type=F32)


def _layer_norm_swish(y, g, b):
    mu = jnp.mean(y, axis=-1, keepdims=True)
    var = jnp.mean(jnp.square(y - mu), axis=-1, keepdims=True)
    yn = (y - mu) * lax.rsqrt(var + EPS) * g + b
    return jax.nn.silu(yn)


def _small_inproj_kernel(x_ref, nw_ref, w_ref, cos_ref, sin_ref, de_ref,
                         v_ref, g_ref, u_ref, qt_ref, kt_ref, r0_ref, tail0_ref,
                         p_scr):
    n = v_ref.shape[0]
    xn = _rms(x_ref[...], nw_ref[...]).astype(BF16)
    p_scr[...] = _dot(xn, w_ref[...])
    dec = slice(0, n)
    met = slice(n, n + CHUNK)
    for h in range(RET_HEADS):
        q = _rot(p_scr[dec, _head(h)], cos_ref[dec, :], sin_ref[dec, :])
        k = _rot(p_scr[dec, _head(RET_HEADS + h)], cos_ref[dec, :], sin_ref[dec, :]) * K_SCALE
        qt_ref[_head(h), :] = q.T
        kt_ref[_head(h), :] = k.T
        km = _rot(p_scr[met, _head(RET_HEADS + h)], cos_ref[met, :], sin_ref[met, :]) * K_SCALE
        vm = p_scr[met, _head(2 * RET_HEADS + h)]
        r0_ref[h] = _dot_tn((km * de_ref[h]).astype(BF16), vm.astype(BF16))
    v_ref[...] = p_scr[dec, 2 * RET_W:3 * RET_W]
    g_ref[...] = p_scr[dec, 3 * RET_W:4 * RET_W]
    a = p_scr[:, 4 * RET_W:4 * RET_W + CONV_CH]
    b = p_scr[:, 4 * RET_W + CONV_CH:]
    u = a * jax.nn.sigmoid(b)
    u_ref[...] = u[dec, :]
    tail0_ref[...] = u[n + CHUNK - TAIL_ROWS:n + CHUNK, :]


def _small_inproj(xs, nw, w_in, cos_s, sin_s, dec_end, n_dec):
    rows = xs.shape[0]
    full = lambda shape: pl.BlockSpec(shape, lambda i: (0,) * len(shape))
    out_shapes = (
        jax.ShapeDtypeStruct((n_dec, RET_W), F32),
        jax.ShapeDtypeStruct((n_dec, RET_W), F32),
        jax.ShapeDtypeStruct((n_dec, CONV_CH), F32),
        jax.ShapeDtypeStruct((RET_W, n_dec), F32),
        jax.ShapeDtypeStruct((RET_W, n_dec), F32),
        jax.ShapeDtypeStruct((RET_HEADS, HEAD_DIM, HEAD_DIM), F32),
        jax.ShapeDtypeStruct((TAIL_ROWS, CONV_CH), F32),
    )
    return pl.pallas_call(
        _small_inproj_kernel,
        grid=(1,),
        in_specs=[full(xs.shape), full(nw.shape), full(w_in.shape), full(cos_s.shape),
                  full(sin_s.shape), full(dec_end.shape)],
        out_specs=tuple(full(s.shape) for s in out_shapes),
        out_shape=out_shapes,
        scratch_shapes=[pltpu.VMEM((rows, IN_COLS), F32)],
        compiler_params=pltpu.CompilerParams(
            dimension_semantics=("arbitrary",), vmem_limit_bytes=40 * MIB),
        name="small_inproj",
    )(xs, nw, w_in, cos_s, sin_s, dec_end)


def _prompt_mix_kernel(x_ref, nw_ref, w_ref, cos_ref, sin_ref, mask_ref, ds_ref, de_ref, cd_ref,
                       r0_ref, tail0_ref, cw_ref, cb_ref, lg_ref, lb_ref,
                       mix_ref, retp_ref, convp_ref,
                       p_scr, r_scr, u_scr, y_scr):
    tile = x_ref.shape[0]
    j = pl.program_id(1)

    n_slabs = CONV_CH // HEAD_DIM

    @pl.when(j == 0)
    def _():
        r_scr[...] = r0_ref[...]
        for c in range(n_slabs):
            u_scr[c, 0:TAIL_ROWS, :] = tail0_ref[:, _head(c)]

    xn = _rms(x_ref[...], nw_ref[...]).astype(BF16)

    ab = _dot(xn, w_ref[:, 4 * RET_W:])
    u = ab[:, :CONV_CH] * jax.nn.sigmoid(ab[:, CONV_CH:])
    for c in range(n_slabs):
        u_scr[c, TAIL_ROWS:TAIL_ROWS + tile, :] = u[:, _head(c)]
    p_scr[...] = _dot(xn, w_ref[:, :4 * RET_W])
    block = CONV_ROW_STRIDE * SUBLANES
    for c in range(n_slabs):
        taps = [cw_ref[t:t + 1, _head(c)] for t in range(CONV_WIDTH)]
        bias = cb_ref[:, _head(c)]
        for b0 in range(0, tile, block):
            acc = [None] * CONV_ROW_STRIDE
            for p in range(CONV_ROW_STRIDE + CONV_WIDTH - 1):
                win = u_scr[c, pl.ds(TAIL_SKIP + b0 + p, SUBLANES, stride=CONV_ROW_STRIDE), :]
                for r in range(CONV_ROW_STRIDE):
                    t = p - r
                    if 0 <= t < CONV_WIDTH:
                        term = taps[t] * win
                        acc[r] = term if acc[r] is None else acc[r] + term
            for r in range(CONV_ROW_STRIDE):
                y_scr[c, pl.ds(b0 + r, SUBLANES, stride=CONV_ROW_STRIDE), :] = acc[r] + bias
    ys = [y_scr[c] for c in range(n_slabs)]
    mu = sum(jnp.sum(y, axis=-1, keepdims=True) for y in ys) * (1.0 / CONV_CH)
    dev = [y - mu for y in ys]
    var = sum(jnp.sum(d * d, axis=-1, keepdims=True) for d in dev) * (1.0 / CONV_CH)
    inv = lax.rsqrt(var + EPS)
    for c in range(n_slabs):
        yn = dev[c] * inv * lg_ref[:, _head(c)] + lb_ref[:, _head(c)]
        mix_ref[:, RET_W + c * HEAD_DIM:RET_W + (c + 1) * HEAD_DIM] = jax.nn.silu(yn).astype(BF16)
        u_scr[c, 0:TAIL_ROWS, :] = u_scr[c, tile:tile + TAIL_ROWS, :]

    for c in range(tile // CHUNK):
        rows = slice(c * CHUNK, (c + 1) * CHUNK)
        cos2 = cos_ref[rows, :]
        sin2 = sin_ref[rows, :]
        for h in range(RET_HEADS):
            q = _rot(p_scr[rows, _head(h)], cos2, sin2)
            k = _rot(p_scr[rows, _head(RET_HEADS + h)], cos2, sin2) * K_SCALE
            vb = p_scr[rows, _head(2 * RET_HEADS + h)].astype(BF16)
            gate = p_scr[rows, _head(3 * RET_HEADS + h)]
            qb = q.astype(BF16)
            scores = _dot_nt(qb, k.astype(BF16)) * mask_ref[h]
            r_prev = r_scr[h]
            o = _dot(scores.astype(BF16), vb) + _dot(qb, r_prev.astype(BF16)) * ds_ref[h]
            kv = _dot_tn((k * de_ref[h]).astype(BF16), vb)
            r_scr[h] = cd_ref[h] * r_prev + kv
            o = o * lax.rsqrt(jnp.mean(o * o, axis=-1, keepdims=True) + EPS)
            mix_ref[rows, _head(h)] = (o * jax.nn.silu(gate)).astype(BF16)

    @pl.when(j == pl.num_programs(1) - 1)
    def _():
        retp_ref[...] = r_scr[...]
        for c in range(n_slabs):
            convp_ref[:, _head(c)] = u_scr[c, TAIL_SKIP:TAIL_ROWS, :]


def _prompt_mix(x2d, nw, w_in, cos_p, sin_p, mask, dec_start, dec_end, chunk_dec, r0, tail0,
                conv_w, conv_b, ln_g, ln_b, batch, seq):
    tile = PROMPT_TILE
    nt = seq // tile
    const = lambda shape: pl.BlockSpec(shape, lambda b, j: (0,) * len(shape))
    out_shapes = (
        jax.ShapeDtypeStruct((batch * seq, D_MODEL), BF16),
        jax.ShapeDtypeStruct((batch, RET_HEADS, HEAD_DIM, HEAD_DIM), F32),
        jax.ShapeDtypeStruct((batch, CONV_TAIL, CONV_CH), F32),
    )
    return pl.pallas_call(
        _prompt_mix_kernel,
        grid=(batch, nt),
        in_specs=[
            pl.BlockSpec((tile, D_MODEL), lambda b, j: (b * nt + j, 0)),
            const(nw.shape), const(w_in.shape),
            pl.BlockSpec((tile, HEAD_DIM), lambda b, j: (j, 0)),
            pl.BlockSpec((tile, HEAD_DIM), lambda b, j: (j, 0)),
            const(mask.shape), const(dec_start.shape), const(dec_end.shape), const(chunk_dec.shape),
            const(r0.shape), const(tail0.shape), const(conv_w.shape), const(conv_b.shape),
            const(ln_g.shape), const(ln_b.shape),
        ],
        out_specs=(
            pl.BlockSpec((tile, D_MODEL), lambda b, j: (b * nt + j, 0)),
            pl.BlockSpec((None, RET_HEADS, HEAD_DIM, HEAD_DIM), lambda b, j: (b, 0, 0, 0)),
            pl.BlockSpec((None, CONV_TAIL, CONV_CH), lambda b, j: (b, 0, 0)),
        ),
        out_shape=out_shapes,
        scratch_shapes=[
            pltpu.VMEM((tile, 4 * RET_W), F32),
            pltpu.VMEM((RET_HEADS, HEAD_DIM, HEAD_DIM), F32),
            pltpu.VMEM((CONV_CH // HEAD_DIM, TAIL_ROWS + tile, HEAD_DIM), F32),
            pltpu.VMEM((CONV_CH // HEAD_DIM, tile, HEAD_DIM), F32),
        ],
        compiler_params=pltpu.CompilerParams(
            dimension_semantics=("arbitrary", "arbitrary"), vmem_limit_bytes=48 * MIB),
        name="prompt_mix",
    )(x2d, nw, w_in, cos_p, sin_p, mask, dec_start, dec_end, chunk_dec, r0, tail0,
      conv_w, conv_b, ln_g, ln_b)


def _sample_mix_kernel(qt_ref, kt_ref, v_ref, g_ref, u_ref, gam_ref, sr_ref, sc_ref,
                       cw_ref, cb_ref, lg_ref, lb_ref,
                       mix_ref, so_ref, sco_ref, o_scr):
    nb = v_ref.shape[0]
    for t in range(nb):
        for h in range(RET_HEADS):
            kc = kt_ref[_head(h), t:t + 1]
            qc = qt_ref[_head(h), t:t + 1]
            vr = v_ref[t:t + 1, _head(h)]
            s_new = gam_ref[h:h + 1, :] * sr_ref[t, h] + kc * vr
            so_ref[t, h] = s_new
            o_scr[t:t + 1, _head(h)] = jnp.sum(qc * s_new, axis=0, keepdims=True)
    for h in range(RET_HEADS):
        o = o_scr[:, _head(h)]
        o = o * lax.rsqrt(jnp.mean(o * o, axis=-1, keepdims=True) + EPS)
        mix_ref[:, _head(h)] = o * jax.nn.silu(g_ref[:, _head(h)])
    u = u_ref[...]
    y = u * cw_ref[CONV_TAIL:CONV_WIDTH, :] + cb_ref[...]
    for t in range(CONV_TAIL):
        y = y + sc_ref[t] * cw_ref[t:t + 1, :]
    mix_ref[:, RET_W:] = _layer_norm_swish(y, lg_ref[...], lb_ref[...])
    for t in range(CONV_TAIL - 1):
        sco_ref[t] = sc_ref[t + 1]
    sco_ref[CONV_TAIL - 1] = u


def _sample_mix(qt3, kt3, v, g, u, gam, state_ret, state_conv, conv_w, conv_b, ln_g, ln_b):
    n = v.shape[0]
    nb = SAMPLE_BLOCK
    const = lambda shape: pl.BlockSpec(shape, lambda i: (0,) * len(shape))
    rows = lambda width: pl.BlockSpec((nb, width), lambda i: (i, 0))
    out_shapes = (
        jax.ShapeDtypeStruct((n, D_MODEL), F32),
        jax.ShapeDtypeStruct(state_ret.shape, F32),
        jax.ShapeDtypeStruct(state_conv.shape, F32),
    )
    st_spec = pl.BlockSpec((nb, RET_HEADS, HEAD_DIM, HEAD_DIM), lambda i: (i, 0, 0, 0))
    cv_spec = pl.BlockSpec((CONV_TAIL, nb, CONV_CH), lambda i: (0, i, 0))
    return pl.pallas_call(
        _sample_mix_kernel,
        grid=(n // nb,),
        in_specs=[
            pl.BlockSpec((None, RET_W, nb), lambda i: (i, 0, 0)),
            pl.BlockSpec((None, RET_W, nb), lambda i: (i, 0, 0)),
            rows(RET_W), rows(RET_W), rows(CONV_CH), const(gam.shape), st_spec, cv_spec,
            const(conv_w.shape), const(conv_b.shape), const(ln_g.shape), const(ln_b.shape),
        ],
        out_specs=(rows(D_MODEL), st_spec, cv_spec),
        out_shape=out_shapes,
        scratch_shapes=[pltpu.VMEM((nb, RET_W), F32)],
        compiler_params=pltpu.CompilerParams(
            dimension_semantics=("arbitrary",), vmem_limit_bytes=32 * MIB),
        name="sample_mix",
    )(qt3, kt3, v, g, u, gam, state_ret, state_conv, conv_w, conv_b, ln_g, ln_b)


def _out_ffn_kernel(mix_ref, x_ref, wo_ref, wi_ref, wd_ref, npost_ref, nfpre_ref, nfpost_ref, y_ref):
    a = _dot(mix_ref[...].astype(BF16), wo_ref[...])
    h1 = x_ref[...] + _rms(a, npost_ref[...])
    xf = _rms(h1, nfpre_ref[...]).astype(BF16)
    dn = None
    for c0 in range(0, D_FF, FFN_CHUNK):
        c1 = min(c0 + FFN_CHUNK, D_FF)
        gate = _dot(xf, wi_ref[:, c0:c1])
        up = _dot(xf, wi_ref[:, D_FF + c0:D_FF + c1])
        part = _dot((jax.nn.silu(gate) * up).astype(BF16), wd_ref[c0:c1, :])
        dn = part if dn is None else dn + part
    y_ref[...] = h1 + _rms(dn, nfpost_ref[...])


def _out_ffn(mix, x2d, w_out, w_ffn_in, w_ffn_out, n_post, n_fpre, n_fpost, tile):
    n = x2d.shape[0]
    const = lambda shape: pl.BlockSpec(shape, lambda i: (0,) * len(shape),
                                       pipeline_mode=pl.Buffered(1))
    rows = pl.BlockSpec((tile, D_MODEL), lambda i: (i, 0))
    return pl.pallas_call(
        _out_ffn_kernel,
        grid=(n // tile,),
        in_specs=[rows, rows, const(w_out.shape), const(w_ffn_in.shape), const(w_ffn_out.shape),
                  const(n_post.shape), const(n_fpre.shape), const(n_fpost.shape)],
        out_specs=rows,
        out_shape=jax.ShapeDtypeStruct((n, D_MODEL), F32),
        compiler_params=pltpu.CompilerParams(
            dimension_semantics=("arbitrary",), vmem_limit_bytes=52 * MIB),
        name="out_ffn",
    )(mix, x2d, w_out, w_ffn_in, w_ffn_out, n_post, n_fpre, n_fpost)


def _rotary_tables(pos):
    half = HEAD_DIM // 2
    inv = 1.0 / (ROPE_BASE ** jnp.linspace(0.0, 1.0, half, dtype=F32))
    ang = pos.astype(F32)[:, None] * inv[None, :]
    cos, sin = jnp.cos(ang), jnp.sin(ang)
    return jnp.concatenate([cos, cos], axis=-1), jnp.concatenate([-sin, sin], axis=-1)


def _decay_tables():
    lg = jnp.log(1.0 - 2.0 ** (-5.0 - jnp.arange(RET_HEADS, dtype=F32)))
    idx = jnp.arange(CHUNK, dtype=F32)
    diff = idx[:, None] - idx[None, :]
    mask = jnp.where(diff[None] >= 0, jnp.exp(lg[:, None, None] * jnp.maximum(diff, 0.0)[None]), 0.0)
    wide = lambda col: jnp.broadcast_to(col[:, :, None], (RET_HEADS, CHUNK, HEAD_DIM))
    dec_end = wide(jnp.exp(lg[:, None] * (CHUNK - 1 - idx)[None, :]))
    dec_start = wide(jnp.exp(lg[:, None] * (idx + 1.0)[None, :]))
    chunk_dec = jnp.broadcast_to(jnp.exp(lg * CHUNK)[:, None, None], (RET_HEADS, HEAD_DIM, HEAD_DIM))
    gam = jnp.broadcast_to(jnp.exp(lg)[:, None], (RET_HEADS, HEAD_DIM))
    return mask, dec_start, dec_end, chunk_dec, gam


def kernel(x_prompt, x_sample, state_ret, state_conv, meta_tokens, norm_mix_pre, norm_mix_post,
           norm_ffn_pre, norm_ffn_post, w_in, conv_w, conv_b, conv_ln_g, conv_ln_b, w_out,
           w_ffn_in, w_ffn_out):
    batch, seq, _ = x_prompt.shape
    n_dec, dec_seq, _ = x_sample.shape
    depth = w_in.shape[0]
    assert depth == 1 and dec_seq == 1 and seq % PROMPT_TILE == 0
    assert n_dec % SAMPLE_BLOCK == 0 and (batch * seq) % FFN_TILE == 0

    row = lambda v: v.reshape(1, -1)
    nw_pre, nw_post = row(norm_mix_pre[0]), row(norm_mix_post[0])
    nf_pre, nf_post = row(norm_ffn_pre[0]), row(norm_ffn_post[0])
    w_in_b = w_in[0].astype(BF16)
    w_out_b = w_out[0].astype(BF16)
    w_fi_b = w_ffn_in[0].astype(BF16)
    w_fo_b = w_ffn_out[0].astype(BF16)
    cw, cb, lg, lb = conv_w[0], row(conv_b[0]), row(conv_ln_g[0]), row(conv_ln_b[0])

    mask, dec_start, dec_end, chunk_dec, gam = _decay_tables()
    past_len = 16384
    pos_small = jnp.concatenate([
        jnp.full((n_dec,), past_len, jnp.int32),
        jnp.zeros((CHUNK - N_META,), jnp.int32),
        jnp.arange(N_META, dtype=jnp.int32)])
    cos_s, sin_s = _rotary_tables(pos_small)
    cos_p, sin_p = _rotary_tables(N_META + jnp.arange(seq))

    xs = jnp.concatenate([
        x_sample.reshape(n_dec, D_MODEL),
        jnp.zeros((CHUNK - N_META, D_MODEL), x_sample.dtype),
        meta_tokens.astype(x_prompt.dtype)], axis=0)
    v_s, g_s, u_s, qt, kt, r0, tail0 = _small_inproj(
        xs, nw_pre, w_in_b, cos_s, sin_s, dec_end, n_dec)

    x2d = x_prompt.reshape(batch * seq, D_MODEL)
    mix_p, ret_p, conv_p = _prompt_mix(
        x2d, nw_pre, w_in_b, cos_p, sin_p, mask, dec_start, dec_end, chunk_dec, r0, tail0,
        cw, cb, lg, lb, batch, seq)

    blocks = n_dec // SAMPLE_BLOCK
    to_blocks = lambda t: t.reshape(RET_W, blocks, SAMPLE_BLOCK).transpose(1, 0, 2)
    mix_s, ret_s, conv_s = _sample_mix(
        to_blocks(qt), to_blocks(kt), v_s, g_s, u_s, gam, state_ret[0],
        state_conv[0].transpose(1, 0, 2), cw, cb, lg, lb)
    conv_s = conv_s.transpose(1, 0, 2)

    y_p = _out_ffn(mix_p, x2d, w_out_b, w_fi_b, w_fo_b, nw_post, nf_pre, nf_post, FFN_TILE)
    y_s = _out_ffn(mix_s, x_sample.reshape(n_dec, D_MODEL), w_out_b, w_fi_b, w_fo_b,
                   nw_post, nf_pre, nf_post, n_dec)

    return (y_p.reshape(batch, seq, D_MODEL), y_s.reshape(n_dec, 1, D_MODEL),
            ret_p[None], conv_p[None], ret_s[None], conv_s[None])
```

```python
import jax
import jax.numpy as jnp
from jax import lax
from jax.experimental import pallas as pl
from jax.experimental.pallas import tpu as pltpu

F32 = jnp.float32
BF16 = jnp.bfloat16

D_MODEL = 1024
N_META = 16
PAST_LEN = 16384
RET_HEADS = 4
HEAD_DIM = 128
RET_W = RET_HEADS * HEAD_DIM
CONV_CH = D_MODEL - RET_W
CONV_WIDTH = 31
CONV_TAIL = CONV_WIDTH - 1
CHUNK = 128
ROPE_BASE = 10000.0
D_FF = 2816
IN_COLS = 4 * RET_W + 2 * CONV_CH
EPS = 1e-6
K_SCALE = HEAD_DIM ** -0.5

SUBLANES = 8
TAIL_ROWS = 32
TAIL_SKIP = TAIL_ROWS - CONV_TAIL

PROMPT_TILE = 512
CONV_ROW_STRIDE = 4
SAMPLE_BLOCK = 16
FFN_TILE = 512
FFN_CHUNK = 1024
WEIGHT_CHUNKS = 8
MIB = 1024 * 1024


def _rms(x, w):
    return x * lax.rsqrt(jnp.mean(x * x, axis=-1, keepdims=True) + EPS) * w


def _rot(xh, cos2, sin2):
    return xh * cos2 + pltpu.roll(xh, HEAD_DIM // 2, 1) * sin2


def _head(i):
    return slice(i * HEAD_DIM, (i + 1) * HEAD_DIM)


def _dot(a, b):
    return jnp.dot(a, b, preferred_element_type=F32)


def _dot_nt(a, b):
    return lax.dot_general(a, b, (((1,), (1,)), ((), ())), preferred_element_type=F32)


def _dot_tn(a, b):
    return lax.dot_general(a, b, (((0,), (0,)), ((), ())), preferred_element_type=F32)


def _layer_norm_swish(y, g, b):
    mu = jnp.mean(y, axis=-1, keepdims=True)
    var = jnp.mean(jnp.square(y - mu), axis=-1, keepdims=True)
    yn = (y - mu) * lax.rsqrt(var + EPS) * g + b
    return jax.nn.silu(yn)


def _small_inproj_kernel(x_ref, nw_ref, w_ref, cos_ref, sin_ref, de_ref,
                         v_ref, g_ref, u_ref, qt_ref, kt_ref, r0_ref, tail0_ref,
                         p_scr):
    n = v_ref.shape[0]
    xn = _rms(x_ref[...], nw_ref[...]).astype(BF16)
    p_scr[...] = _dot(xn, w_ref[...])
    dec = slice(0, n)
    met = slice(n, n + CHUNK)
    for h in range(RET_HEADS):
        q = _rot(p_scr[dec, _head(h)], cos_ref[dec, :], sin_ref[dec, :])
        k = _rot(p_scr[dec, _head(RET_HEADS + h)], cos_ref[dec, :], sin_ref[dec, :]) * K_SCALE
        qt_ref[_head(h), :] = q.T
        kt_ref[_head(h), :] = k.T
        km = _rot(p_scr[met, _head(RET_HEADS + h)], cos_ref[met, :], sin_ref[met, :]) * K_SCALE
        vm = p_scr[met, _head(2 * RET_HEADS + h)]
        r0_ref[h] = _dot_tn((km * de_ref[h]).astype(BF16), vm.astype(BF16))
    v_ref[...] = p_scr[dec, 2 * RET_W:3 * RET_W]
    g_ref[...] = p_scr[dec, 3 * RET_W:4 * RET_W]
    a = p_scr[:, 4 * RET_W:4 * RET_W + CONV_CH]
    b = p_scr[:, 4 * RET_W + CONV_CH:]
    u = a * jax.nn.sigmoid(b)
    u_ref[...] = u[dec, :]
    tail0_ref[...] = u[n + CHUNK - TAIL_ROWS:n + CHUNK, :]


def _small_inproj(xs, nw, w_in, cos_s, sin_s, dec_end, n_dec):
    rows = xs.shape[0]
    full = lambda shape: pl.BlockSpec(shape, lambda i: (0,) * len(shape))
    out_shapes = (
        jax.ShapeDtypeStruct((n_dec, RET_W), F32),
        jax.ShapeDtypeStruct((n_dec, RET_W), F32),
        jax.ShapeDtypeStruct((n_dec, CONV_CH), F32),
        jax.ShapeDtypeStruct((RET_W, n_dec), F32),
        jax.ShapeDtypeStruct((RET_W, n_dec), F32),
        jax.ShapeDtypeStruct((RET_HEADS, HEAD_DIM, HEAD_DIM), F32),
        jax.ShapeDtypeStruct((TAIL_ROWS, CONV_CH), F32),
    )
    return pl.pallas_call(
        _small_inproj_kernel,
        grid=(1,),
        in_specs=[full(xs.shape), full(nw.shape), full(w_in.shape), full(cos_s.shape),
                  full(sin_s.shape), full(dec_end.shape)],
        out_specs=tuple(full(s.shape) for s in out_shapes),
        out_shape=out_shapes,
        scratch_shapes=[pltpu.VMEM((rows, IN_COLS), F32)],
        compiler_params=pltpu.CompilerParams(
            dimension_semantics=("arbitrary",), vmem_limit_bytes=40 * MIB),
        name="small_inproj",
    )(xs, nw, w_in, cos_s, sin_s, dec_end)


def _prompt_mix_kernel(x_ref, nw_ref, w_ref, cos_ref, sin_ref, mask_ref, ds_ref, de_ref, cd_ref,
                       r0_ref, tail0_ref, cw_ref, cb_ref, lg_ref, lb_ref,
                       mix_ref, retp_ref, convp_ref,
                       p_scr, r_scr, u_scr, y_scr):
    tile = x_ref.shape[0]
    j = pl.program_id(1)

    n_slabs = CONV_CH // HEAD_DIM

    @pl.when(j == 0)
    def _():
        r_scr[...] = r0_ref[...]
        for c in range(n_slabs):
            u_scr[c, 0:TAIL_ROWS, :] = tail0_ref[:, _head(c)]

    xn = _rms(x_ref[...], nw_ref[...]).astype(BF16)

    ab = _dot(xn, w_ref[:, 4 * RET_W:])
    u = ab[:, :CONV_CH] * jax.nn.sigmoid(ab[:, CONV_CH:])
    for c in range(n_slabs):
        u_scr[c, TAIL_ROWS:TAIL_ROWS + tile, :] = u[:, _head(c)]
    p_scr[...] = _dot(xn, w_ref[:, :4 * RET_W])
    block = CONV_ROW_STRIDE * SUBLANES
    for c in range(n_slabs):
        taps = [cw_ref[t:t + 1, _head(c)] for t in range(CONV_WIDTH)]
        bias = cb_ref[:, _head(c)]
        for b0 in range(0, tile, block):
            acc = [None] * CONV_ROW_STRIDE
            for p in range(CONV_ROW_STRIDE + CONV_WIDTH - 1):
                win = u_scr[c, pl.ds(TAIL_SKIP + b0 + p, SUBLANES, stride=CONV_ROW_STRIDE), :]
                for r in range(CONV_ROW_STRIDE):
                    t = p - r
                    if 0 <= t < CONV_WIDTH:
                        term = taps[t] * win
                        acc[r] = term if acc[r] is None else acc[r] + term
            for r in range(CONV_ROW_STRIDE):
                y_scr[c, pl.ds(b0 + r, SUBLANES, stride=CONV_ROW_STRIDE), :] = acc[r] + bias
    ys = [y_scr[c] for c in range(n_slabs)]
    mu = sum(jnp.sum(y, axis=-1, keepdims=True) for y in ys) * (1.0 / CONV_CH)
    dev = [y - mu for y in ys]
    var = sum(jnp.sum(d * d, axis=-1, keepdims=True) for d in dev) * (1.0 / CONV_CH)
    inv = lax.rsqrt(var + EPS)
    for c in range(n_slabs):
        yn = dev[c] * inv * lg_ref[:, _head(c)] + lb_ref[:, _head(c)]
        mix_ref[:, RET_W + c * HEAD_DIM:RET_W + (c + 1) * HEAD_DIM] = jax.nn.silu(yn).astype(BF16)
        u_scr[c, 0:TAIL_ROWS, :] = u_scr[c, tile:tile + TAIL_ROWS, :]

    heads = range(RET_HEADS)
    for c in range(tile // CHUNK):
        rows = slice(c * CHUNK, (c + 1) * CHUNK)
        cos2 = cos_ref[rows, :]
        sin2 = sin_ref[rows, :]
        q = [_rot(p_scr[rows, _head(h)], cos2, sin2) for h in heads]
        k = [_rot(p_scr[rows, _head(RET_HEADS + h)], cos2, sin2) * K_SCALE for h in heads]
        qb = [t.astype(BF16) for t in q]
        kb = [t.astype(BF16) for t in k]
        vb = [p_scr[rows, _head(2 * RET_HEADS + h)].astype(BF16) for h in heads]
        r_prev = [r_scr[h] for h in heads]
        scores = [(_dot_nt(qb[h], kb[h]) * mask_ref[h]).astype(BF16) for h in heads]
        inter = [_dot(qb[h], r_prev[h].astype(BF16)) * ds_ref[h] for h in heads]
        kv = [_dot_tn((k[h] * de_ref[h]).astype(BF16), vb[h]) for h in heads]
        for h in heads:
            r_scr[h] = cd_ref[h] * r_prev[h] + kv[h]
        o = [_dot(scores[h], vb[h]) + inter[h] for h in heads]
        for h in heads:
            on = o[h] * lax.rsqrt(jnp.mean(o[h] * o[h], axis=-1, keepdims=True) + EPS)
            gate = p_scr[rows, _head(3 * RET_HEADS + h)]
            mix_ref[rows, _head(h)] = (on * jax.nn.silu(gate)).astype(BF16)

    @pl.when(j == pl.num_programs(1) - 1)
    def _():
        retp_ref[...] = r_scr[...]
        for c in range(n_slabs):
            convp_ref[:, _head(c)] = u_scr[c, TAIL_SKIP:TAIL_ROWS, :]


def _prompt_mix(x2d, nw, w_in, cos_p, sin_p, mask, dec_start, dec_end, chunk_dec, r0, tail0,
                conv_w, conv_b, ln_g, ln_b, batch, seq):
    tile = PROMPT_TILE
    nt = seq // tile
    const = lambda shape: pl.BlockSpec(shape, lambda b, j: (0,) * len(shape))
    out_shapes = (
        jax.ShapeDtypeStruct((batch * seq, D_MODEL), BF16),
        jax.ShapeDtypeStruct((batch, RET_HEADS, HEAD_DIM, HEAD_DIM), F32),
        jax.ShapeDtypeStruct((batch, CONV_TAIL, CONV_CH), F32),
    )
    return pl.pallas_call(
        _prompt_mix_kernel,
        grid=(batch, nt),
        in_specs=[
            pl.BlockSpec((tile, D_MODEL), lambda b, j: (b * nt + j, 0)),
            const(nw.shape), const(w_in.shape),
            pl.BlockSpec((tile, HEAD_DIM), lambda b, j: (j, 0)),
            pl.BlockSpec((tile, HEAD_DIM), lambda b, j: (j, 0)),
            const(mask.shape), const(dec_start.shape), const(dec_end.shape), const(chunk_dec.shape),
            const(r0.shape), const(tail0.shape), const(conv_w.shape), const(conv_b.shape),
            const(ln_g.shape), const(ln_b.shape),
        ],
        out_specs=(
            pl.BlockSpec((tile, D_MODEL), lambda b, j: (b * nt + j, 0)),
            pl.BlockSpec((None, RET_HEADS, HEAD_DIM, HEAD_DIM), lambda b, j: (b, 0, 0, 0)),
            pl.BlockSpec((None, CONV_TAIL, CONV_CH), lambda b, j: (b, 0, 0)),
        ),
        out_shape=out_shapes,
        scratch_shapes=[
            pltpu.VMEM((tile, 4 * RET_W), F32),
            pltpu.VMEM((RET_HEADS, HEAD_DIM, HEAD_DIM), F32),
            pltpu.VMEM((CONV_CH // HEAD_DIM, TAIL_ROWS + tile, HEAD_DIM), F32),
            pltpu.VMEM((CONV_CH // HEAD_DIM, tile, HEAD_DIM), F32),
        ],
        compiler_params=pltpu.CompilerParams(
            dimension_semantics=("arbitrary", "arbitrary"), vmem_limit_bytes=48 * MIB),
        name="prompt_mix",
    )(x2d, nw, w_in, cos_p, sin_p, mask, dec_start, dec_end, chunk_dec, r0, tail0,
      conv_w, conv_b, ln_g, ln_b)


def _sample_mix_kernel(qt_ref, kt_ref, v_ref, g_ref, u_ref, gam_ref, sr_ref, sc_ref,
                       cw_ref, cb_ref, lg_ref, lb_ref,
                       mix_ref, so_ref, sco_ref, o_scr):
    nb = v_ref.shape[0]
    for t in range(nb):
        for h in range(RET_HEADS):
            kc = kt_ref[_head(h), t:t + 1]
            qc = qt_ref[_head(h), t:t + 1]
            vr = v_ref[t:t + 1, _head(h)]
            s_new = gam_ref[h:h + 1, :] * sr_ref[t, h] + kc * vr
            so_ref[t, h] = s_new
            o_scr[t:t + 1, _head(h)] = jnp.sum(qc * s_new, axis=0, keepdims=True)
    for h in range(RET_HEADS):
        o = o_scr[:, _head(h)]
        o = o * lax.rsqrt(jnp.mean(o * o, axis=-1, keepdims=True) + EPS)
        mix_ref[:, _head(h)] = o * jax.nn.silu(g_ref[:, _head(h)])
    u = u_ref[...]
    y = u * cw_ref[CONV_TAIL:CONV_WIDTH, :] + cb_ref[...]
    for t in range(CONV_TAIL):
        y = y + sc_ref[t] * cw_ref[t:t + 1, :]
    mix_ref[:, RET_W:] = _layer_norm_swish(y, lg_ref[...], lb_ref[...])
    for t in range(CONV_TAIL - 1):
        sco_ref[t] = sc_ref[t + 1]
    sco_ref[CONV_TAIL - 1] = u


def _sample_mix(qt3, kt3, v, g, u, gam, state_ret, state_conv, conv_w, conv_b, ln_g, ln_b):
    n = v.shape[0]
    nb = SAMPLE_BLOCK
    const = lambda shape: pl.BlockSpec(shape, lambda i: (0,) * len(shape))
    rows = lambda width: pl.BlockSpec((nb, width), lambda i: (i, 0))
    out_shapes = (
        jax.ShapeDtypeStruct((n, D_MODEL), F32),
        jax.ShapeDtypeStruct(state_ret.shape, F32),
        jax.ShapeDtypeStruct(state_conv.shape, F32),
    )
    st_spec = pl.BlockSpec((nb, RET_HEADS, HEAD_DIM, HEAD_DIM), lambda i: (i, 0, 0, 0))
    cv_spec = pl.BlockSpec((CONV_TAIL, nb, CONV_CH), lambda i: (0, i, 0))
    return pl.pallas_call(
        _sample_mix_kernel,
        grid=(n // nb,),
        in_specs=[
            pl.BlockSpec((None, RET_W, nb), lambda i: (i, 0, 0)),
            pl.BlockSpec((None, RET_W, nb), lambda i: (i, 0, 0)),
            rows(RET_W), rows(RET_W), rows(CONV_CH), const(gam.shape), st_spec, cv_spec,
            const(conv_w.shape), const(conv_b.shape), const(ln_g.shape), const(ln_b.shape),
        ],
        out_specs=(rows(D_MODEL), st_spec, cv_spec),
        out_shape=out_shapes,
        scratch_shapes=[pltpu.VMEM((nb, RET_W), F32)],
        compiler_params=pltpu.CompilerParams(
            dimension_semantics=("arbitrary",), vmem_limit_bytes=32 * MIB),
        name="sample_mix",
    )(qt3, kt3, v, g, u, gam, state_ret, state_conv, conv_w, conv_b, ln_g, ln_b)


def _out_ffn_kernel(mix_ref, x_ref, wo_ref, wi_ref, wd_ref, npost_ref, nfpre_ref, nfpost_ref, y_ref):
    a = _dot(mix_ref[...].astype(BF16), wo_ref[...])
    h1 = x_ref[...] + _rms(a, npost_ref[...])
    xf = _rms(h1, nfpre_ref[...]).astype(BF16)
    dn = None
    for c0 in range(0, D_FF, FFN_CHUNK):
        c1 = min(c0 + FFN_CHUNK, D_FF)
        gate = _dot(xf, wi_ref[:, c0:c1])
        up = _dot(xf, wi_ref[:, D_FF + c0:D_FF + c1])
        part = _dot((jax.nn.silu(gate) * up).astype(BF16), wd_ref[c0:c1, :])
        dn = part if dn is None else dn + part
    y_ref[...] = h1 + _rms(dn, nfpost_ref[...])


def _out_ffn(mix, x2d, w_out, w_ffn_in, w_ffn_out, n_post, n_fpre, n_fpost, tile):
    n = x2d.shape[0]
    const = lambda shape: pl.BlockSpec(shape, lambda i: (0,) * len(shape),
                                       pipeline_mode=pl.Buffered(1))
    rows = pl.BlockSpec((tile, D_MODEL), lambda i: (i, 0))
    return pl.pallas_call(
        _out_ffn_kernel,
        grid=(n // tile,),
        in_specs=[rows, rows, const(w_out.shape), const(w_ffn_in.shape), const(w_ffn_out.shape),
                  const(n_post.shape), const(n_fpre.shape), const(n_fpost.shape)],
        out_specs=rows,
        out_shape=jax.ShapeDtypeStruct((n, D_MODEL), F32),
        compiler_params=pltpu.CompilerParams(
            dimension_semantics=("arbitrary",), vmem_limit_bytes=52 * MIB),
        name="out_ffn",
    )(mix, x2d, w_out, w_ffn_in, w_ffn_out, n_post, n_fpre, n_fpost)


def _decode_ffn_kernel(mix_ref, x_ref, woc_ref, wic_ref, wdc_ref, npost_ref, nfpre_ref, nfpost_ref,
                       y_ref, wob_ref, wib_ref, wdb_ref, wo_scr, wi_scr, wd_scr):
    i = pl.program_id(0)
    n_chunks = pl.num_programs(0) - 1

    @pl.when(i < n_chunks)
    def _():
        for src, dst, out in ((woc_ref, wo_scr, wob_ref), (wic_ref, wi_scr, wib_ref),
                              (wdc_ref, wd_scr, wdb_ref)):
            rows = src.shape[0]
            chunk = src[...].astype(BF16)
            dst[pl.ds(pl.multiple_of(i * rows, rows), rows), :] = chunk
            out[...] = chunk

    @pl.when(i == n_chunks)
    def _():
        _out_ffn_kernel(mix_ref, x_ref, wo_scr, wi_scr, wd_scr, npost_ref, nfpre_ref, nfpost_ref, y_ref)


def _decode_ffn(mix, x2d, w_out, w_ffn_in, w_ffn_out, n_post, n_fpre, n_fpost):
    n = x2d.shape[0]
    nc = WEIGHT_CHUNKS
    last = nc - 1
    const = lambda shape: pl.BlockSpec(shape, lambda i: (0,) * len(shape))
    chunk = lambda w: pl.BlockSpec((w.shape[0] // nc, w.shape[1]), lambda i: (jnp.minimum(i, last), 0))
    weights = (w_out, w_ffn_in, w_ffn_out)
    assert all(w.shape[0] % (nc * 16) == 0 for w in weights)
    return pl.pallas_call(
        _decode_ffn_kernel,
        grid=(nc + 1,),
        in_specs=[const(mix.shape), const(x2d.shape)] + [chunk(w) for w in weights]
                 + [const(n_post.shape), const(n_fpre.shape), const(n_fpost.shape)],
        out_specs=(const((n, D_MODEL)),) + tuple(chunk(w) for w in weights),
        out_shape=(jax.ShapeDtypeStruct((n, D_MODEL), F32),)
                  + tuple(jax.ShapeDtypeStruct(w.shape, BF16) for w in weights),
        scratch_shapes=[pltpu.VMEM(w.shape, BF16) for w in weights],
        compiler_params=pltpu.CompilerParams(
            dimension_semantics=("arbitrary",), vmem_limit_bytes=52 * MIB),
        name="decode_ffn",
    )(mix, x2d, w_out, w_ffn_in, w_ffn_out, n_post, n_fpre, n_fpost)


def _rotary_tables(pos):
    half = HEAD_DIM // 2
    inv = 1.0 / (ROPE_BASE ** jnp.linspace(0.0, 1.0, half, dtype=F32))
    ang = pos.astype(F32)[:, None] * inv[None, :]
    cos, sin = jnp.cos(ang), jnp.sin(ang)
    return jnp.concatenate([cos, cos], axis=-1), jnp.concatenate([-sin, sin], axis=-1)


def _decay_tables():
    lg = jnp.log(1.0 - 2.0 ** (-5.0 - jnp.arange(RET_HEADS, dtype=F32)))
    idx = jnp.arange(CHUNK, dtype=F32)
    diff = idx[:, None] - idx[None, :]
    mask = jnp.where(diff[None] >= 0, jnp.exp(lg[:, None, None] * jnp.maximum(diff, 0.0)[None]), 0.0)
    wide = lambda col: jnp.broadcast_to(col[:, :, None], (RET_HEADS, CHUNK, HEAD_DIM))
    dec_end = wide(jnp.exp(lg[:, None] * (CHUNK - 1 - idx)[None, :]))
    dec_start = wide(jnp.exp(lg[:, None] * (idx + 1.0)[None, :]))
    chunk_dec = jnp.broadcast_to(jnp.exp(lg * CHUNK)[:, None, None], (RET_HEADS, HEAD_DIM, HEAD_DIM))
    gam = jnp.broadcast_to(jnp.exp(lg)[:, None], (RET_HEADS, HEAD_DIM))
    return mask, dec_start, dec_end, chunk_dec, gam


def kernel(x_prompt, x_sample, state_ret, state_conv, meta_tokens, norm_mix_pre, norm_mix_post,
           norm_ffn_pre, norm_ffn_post, w_in, conv_w, conv_b, conv_ln_g, conv_ln_b, w_out,
           w_ffn_in, w_ffn_out):
    batch, seq, _ = x_prompt.shape
    n_dec, dec_seq, _ = x_sample.shape
    depth = w_in.shape[0]
    assert depth == 1 and dec_seq == 1 and seq % PROMPT_TILE == 0
    assert n_dec % SAMPLE_BLOCK == 0 and (batch * seq) % FFN_TILE == 0

    row = lambda v: v.reshape(1, -1)
    nw_pre, nw_post = row(norm_mix_pre[0]), row(norm_mix_post[0])
    nf_pre, nf_post = row(norm_ffn_pre[0]), row(norm_ffn_post[0])
    w_in_b = w_in[0].astype(BF16)
    cw, cb, lg, lb = conv_w[0], row(conv_b[0]), row(conv_ln_g[0]), row(conv_ln_b[0])

    mask, dec_start, dec_end, chunk_dec, gam = _decay_tables()
    pos_small = jnp.concatenate([
        jnp.full((n_dec,), PAST_LEN, jnp.int32),
        jnp.zeros((CHUNK - N_META,), jnp.int32),
        jnp.arange(N_META, dtype=jnp.int32)])
    cos_s, sin_s = _rotary_tables(pos_small)
    cos_p, sin_p = _rotary_tables(N_META + jnp.arange(seq))

    xs = jnp.concatenate([
        x_sample.reshape(n_dec, D_MODEL),
        jnp.zeros((CHUNK - N_META, D_MODEL), x_sample.dtype),
        meta_tokens.astype(x_prompt.dtype)], axis=0)
    v_s, g_s, u_s, qt, kt, r0, tail0 = _small_inproj(
        xs, nw_pre, w_in_b, cos_s, sin_s, dec_end, n_dec)

    x2d = x_prompt.reshape(batch * seq, D_MODEL)
    mix_p, ret_p, conv_p = _prompt_mix(
        x2d, nw_pre, w_in_b, cos_p, sin_p, mask, dec_start, dec_end, chunk_dec, r0, tail0,
        cw, cb, lg, lb, batch, seq)

    blocks = n_dec // SAMPLE_BLOCK
    to_blocks = lambda t: t.reshape(RET_W, blocks, SAMPLE_BLOCK).transpose(1, 0, 2)
    mix_s, ret_s, conv_s = _sample_mix(
        to_blocks(qt), to_blocks(kt), v_s, g_s, u_s, gam, state_ret[0],
        state_conv[0].transpose(1, 0, 2), cw, cb, lg, lb)
    conv_s = conv_s.transpose(1, 0, 2)

    y_s, w_out_b, w_fi_b, w_fo_b = _decode_ffn(
        mix_s, x_sample.reshape(n_dec, D_MODEL), w_out[0], w_ffn_in[0], w_ffn_out[0],
        nw_post, nf_pre, nf_post)
    y_p = _out_ffn(mix_p, x2d, w_out_b, w_fi_b, w_fo_b, nw_post, nf_pre, nf_post, FFN_TILE)

    return (y_p.reshape(batch, seq, D_MODEL), y_s.reshape(n_dec, 1, D_MODEL),
            ret_p[None], conv_p[None], ret_s[None], conv_s[None])
```

```python
import jax
import jax.numpy as jnp
import numpy as np
from jax import lax
from jax.experimental import pallas as pl
from jax.experimental.pallas import tpu as pltpu

F32 = jnp.float32
BF16 = jnp.bfloat16

D_MODEL = 1024
N_META = 16
PAST_LEN = 16384
RET_HEADS = 4
HEAD_DIM = 128
RET_W = RET_HEADS * HEAD_DIM
CONV_CH = D_MODEL - RET_W
CONV_WIDTH = 31
CONV_TAIL = CONV_WIDTH - 1
CHUNK = 128
ROPE_BASE = 10000.0
D_FF = 2816
IN_COLS = 4 * RET_W + 2 * CONV_CH
EPS = 1e-6
K_SCALE = HEAD_DIM ** -0.5

SUBLANES = 8
TAIL_ROWS = 32
TAIL_SKIP = TAIL_ROWS - CONV_TAIL

PROMPT_TILE = 512
CONV_ROW_STRIDE = 4
SAMPLE_BLOCK = 16
FFN_TILE = 512
FFN_CHUNK = 1024
WEIGHT_CHUNKS = 8
MIB = 1024 * 1024


def _rms(x, w):
    return x * lax.rsqrt(jnp.mean(x * x, axis=-1, keepdims=True) + EPS) * w


def _rot(xh, cos2, sin2):
    return xh * cos2 + pltpu.roll(xh, HEAD_DIM // 2, 1) * sin2


def _head(i):
    return slice(i * HEAD_DIM, (i + 1) * HEAD_DIM)


def _dot(a, b):
    return jnp.dot(a, b, preferred_element_type=F32)


def _dot_nt(a, b):
    return lax.dot_general(a, b, (((1,), (1,)), ((), ())), preferred_element_type=F32)


def _dot_tn(a, b):
    return lax.dot_general(a, b, (((0,), (0,)), ((), ())), preferred_element_type=F32)


def _layer_norm_swish(y, g, b):
    mu = jnp.mean(y, axis=-1, keepdims=True)
    var = jnp.mean(jnp.square(y - mu), axis=-1, keepdims=True)
    yn = (y - mu) * lax.rsqrt(var + EPS) * g + b
    return jax.nn.silu(yn)


def _small_inproj_kernel(x_ref, nw_ref, w_ref, cos_ref, sin_ref, de_ref,
                         v_ref, g_ref, u_ref, qt_ref, kt_ref, r0_ref, tail0_ref,
                         p_scr):
    n = v_ref.shape[0]
    xn = _rms(x_ref[...], nw_ref[...]).astype(BF16)
    p_scr[...] = _dot(xn, w_ref[...])
    dec = slice(0, n)
    met = slice(n, n + CHUNK)
    for h in range(RET_HEADS):
        q = _rot(p_scr[dec, _head(h)], cos_ref[dec, :], sin_ref[dec, :])
        k = _rot(p_scr[dec, _head(RET_HEADS + h)], cos_ref[dec, :], sin_ref[dec, :]) * K_SCALE
        qt_ref[_head(h), :] = q.T
        kt_ref[_head(h), :] = k.T
        km = _rot(p_scr[met, _head(RET_HEADS + h)], cos_ref[met, :], sin_ref[met, :]) * K_SCALE
        vm = p_scr[met, _head(2 * RET_HEADS + h)]
        r0_ref[h] = _dot_tn((km * de_ref[h]).astype(BF16), vm.astype(BF16))
    v_ref[...] = p_scr[dec, 2 * RET_W:3 * RET_W]
    g_ref[...] = p_scr[dec, 3 * RET_W:4 * RET_W]
    a = p_scr[:, 4 * RET_W:4 * RET_W + CONV_CH]
    b = p_scr[:, 4 * RET_W + CONV_CH:]
    u = a * jax.nn.sigmoid(b)
    u_ref[...] = u[dec, :]
    tail0_ref[...] = u[n + CHUNK - TAIL_ROWS:n + CHUNK, :]


def _small_inproj(xs, nw, w_in, cos_s, sin_s, dec_end, n_dec):
    rows = xs.shape[0]
    full = lambda shape: pl.BlockSpec(shape, lambda i: (0,) * len(shape))
    out_shapes = (
        jax.ShapeDtypeStruct((n_dec, RET_W), F32),
        jax.ShapeDtypeStruct((n_dec, RET_W), F32),
        jax.ShapeDtypeStruct((n_dec, CONV_CH), F32),
        jax.ShapeDtypeStruct((RET_W, n_dec), F32),
        jax.ShapeDtypeStruct((RET_W, n_dec), F32),
        jax.ShapeDtypeStruct((RET_HEADS, HEAD_DIM, HEAD_DIM), F32),
        jax.ShapeDtypeStruct((TAIL_ROWS, CONV_CH), F32),
    )
    return pl.pallas_call(
        _small_inproj_kernel,
        grid=(1,),
        in_specs=[full(xs.shape), full(nw.shape), full(w_in.shape), full(cos_s.shape),
                  full(sin_s.shape), full(dec_end.shape)],
        out_specs=tuple(full(s.shape) for s in out_shapes),
        out_shape=out_shapes,
        scratch_shapes=[pltpu.VMEM((rows, IN_COLS), F32)],
        compiler_params=pltpu.CompilerParams(
            dimension_semantics=("arbitrary",), vmem_limit_bytes=40 * MIB),
        name="small_inproj",
    )(xs, nw, w_in, cos_s, sin_s, dec_end)


def _prompt_mix_kernel(x_ref, nw_ref, w_ref, cos_ref, sin_ref, mask_ref, ds_ref, de_ref, cd_ref,
                       r0_ref, tail0_ref, cw_ref, cb_ref, lg_ref, lb_ref,
                       mix_ref, retp_ref, convp_ref,
                       p_scr, r_scr, u_scr, y_scr):
    tile = x_ref.shape[0]
    j = pl.program_id(1)

    n_slabs = CONV_CH // HEAD_DIM

    @pl.when(j == 0)
    def _():
        r_scr[...] = r0_ref[...]
        for c in range(n_slabs):
            u_scr[c, 0:TAIL_ROWS, :] = tail0_ref[:, _head(c)]

    xn = _rms(x_ref[...], nw_ref[...]).astype(BF16)

    ab = _dot(xn, w_ref[:, 4 * RET_W:])
    u = ab[:, :CONV_CH] * jax.nn.sigmoid(ab[:, CONV_CH:])
    for c in range(n_slabs):
        u_scr[c, TAIL_ROWS:TAIL_ROWS + tile, :] = u[:, _head(c)]
    p_scr[...] = _dot(xn, w_ref[:, :4 * RET_W])
    block = CONV_ROW_STRIDE * SUBLANES
    for c in range(n_slabs):
        taps = [cw_ref[t:t + 1, _head(c)] for t in range(CONV_WIDTH)]
        bias = cb_ref[:, _head(c)]
        for b0 in range(0, tile, block):
            acc = [None] * CONV_ROW_STRIDE
            for p in range(CONV_ROW_STRIDE + CONV_WIDTH - 1):
                win = u_scr[c, pl.ds(TAIL_SKIP + b0 + p, SUBLANES, stride=CONV_ROW_STRIDE), :]
                for r in range(CONV_ROW_STRIDE):
                    t = p - r
                    if 0 <= t < CONV_WIDTH:
                        term = taps[t] * win
                        acc[r] = term if acc[r] is None else acc[r] + term
            for r in range(CONV_ROW_STRIDE):
                y_scr[c, pl.ds(b0 + r, SUBLANES, stride=CONV_ROW_STRIDE), :] = acc[r] + bias
    ys = [y_scr[c] for c in range(n_slabs)]
    mu = sum(jnp.sum(y, axis=-1, keepdims=True) for y in ys) * (1.0 / CONV_CH)
    dev = [y - mu for y in ys]
    var = sum(jnp.sum(d * d, axis=-1, keepdims=True) for d in dev) * (1.0 / CONV_CH)
    inv = lax.rsqrt(var + EPS)
    for c in range(n_slabs):
        yn = dev[c] * inv * lg_ref[:, _head(c)] + lb_ref[:, _head(c)]
        mix_ref[:, RET_W + c * HEAD_DIM:RET_W + (c + 1) * HEAD_DIM] = jax.nn.silu(yn).astype(BF16)
        u_scr[c, 0:TAIL_ROWS, :] = u_scr[c, tile:tile + TAIL_ROWS, :]

    for c in range(tile // CHUNK):
        rows = slice(c * CHUNK, (c + 1) * CHUNK)
        cos2 = cos_ref[rows, :]
        sin2 = sin_ref[rows, :]
        for h in range(RET_HEADS):
            q = _rot(p_scr[rows, _head(h)], cos2, sin2)
            k = _rot(p_scr[rows, _head(RET_HEADS + h)], cos2, sin2) * K_SCALE
            vb = p_scr[rows, _head(2 * RET_HEADS + h)].astype(BF16)
            gate = p_scr[rows, _head(3 * RET_HEADS + h)]
            qb = q.astype(BF16)
            scores = _dot_nt(qb, k.astype(BF16)) * mask_ref[h]
            r_prev = r_scr[h]
            o = _dot(scores.astype(BF16), vb) + _dot(qb, r_prev.astype(BF16)) * ds_ref[h]
            kv = _dot_tn((k * de_ref[h]).astype(BF16), vb)
            r_scr[h] = cd_ref[h] * r_prev + kv
            o = o * lax.rsqrt(jnp.mean(o * o, axis=-1, keepdims=True) + EPS)
            mix_ref[rows, _head(h)] = (o * jax.nn.silu(gate)).astype(BF16)

    @pl.when(j == pl.num_programs(1) - 1)
    def _():
        retp_ref[...] = r_scr[...]
        for c in range(n_slabs):
            convp_ref[:, _head(c)] = u_scr[c, TAIL_SKIP:TAIL_ROWS, :]


def _prompt_mix(x2d, nw, w_in, cos_p, sin_p, mask, dec_start, dec_end, chunk_dec, r0, tail0,
                conv_w, conv_b, ln_g, ln_b, batch, seq):
    tile = PROMPT_TILE
    nt = seq // tile
    const = lambda shape: pl.BlockSpec(shape, lambda b, j: (0,) * len(shape))
    out_shapes = (
        jax.ShapeDtypeStruct((batch * seq, D_MODEL), BF16),
        jax.ShapeDtypeStruct((batch, RET_HEADS, HEAD_DIM, HEAD_DIM), F32),
        jax.ShapeDtypeStruct((batch, CONV_TAIL, CONV_CH), F32),
    )
    return pl.pallas_call(
        _prompt_mix_kernel,
        grid=(batch, nt),
        in_specs=[
            pl.BlockSpec((tile, D_MODEL), lambda b, j: (b * nt + j, 0)),
            const(nw.shape), const(w_in.shape),
            pl.BlockSpec((tile, HEAD_DIM), lambda b, j: (j, 0)),
            pl.BlockSpec((tile, HEAD_DIM), lambda b, j: (j, 0)),
            const(mask.shape), const(dec_start.shape), const(dec_end.shape), const(chunk_dec.shape),
            const(r0.shape), const(tail0.shape), const(conv_w.shape), const(conv_b.shape),
            const(ln_g.shape), const(ln_b.shape),
        ],
        out_specs=(
            pl.BlockSpec((tile, D_MODEL), lambda b, j: (b * nt + j, 0)),
            pl.BlockSpec((None, RET_HEADS, HEAD_DIM, HEAD_DIM), lambda b, j: (b, 0, 0, 0)),
            pl.BlockSpec((None, CONV_TAIL, CONV_CH), lambda b, j: (b, 0, 0)),
        ),
        out_shape=out_shapes,
        scratch_shapes=[
            pltpu.VMEM((tile, 4 * RET_W), F32),
            pltpu.VMEM((RET_HEADS, HEAD_DIM, HEAD_DIM), F32),
            pltpu.VMEM((CONV_CH // HEAD_DIM, TAIL_ROWS + tile, HEAD_DIM), F32),
            pltpu.VMEM((CONV_CH // HEAD_DIM, tile, HEAD_DIM), F32),
        ],
        compiler_params=pltpu.CompilerParams(
            dimension_semantics=("arbitrary", "arbitrary"), vmem_limit_bytes=48 * MIB),
        name="prompt_mix",
    )(x2d, nw, w_in, cos_p, sin_p, mask, dec_start, dec_end, chunk_dec, r0, tail0,
      conv_w, conv_b, ln_g, ln_b)


def _sample_mix_kernel(qt_ref, kt_ref, v_ref, g_ref, u_ref, gam_ref, sr_ref, sc_ref,
                       cw_ref, cb_ref, lg_ref, lb_ref,
                       mix_ref, so_ref, sco_ref, o_scr):
    nb = v_ref.shape[0]
    for t in range(nb):
        for h in range(RET_HEADS):
            kc = kt_ref[_head(h), t:t + 1]
            qc = qt_ref[_head(h), t:t + 1]
            vr = v_ref[t:t + 1, _head(h)]
            s_new = gam_ref[h:h + 1, :] * sr_ref[t, h] + kc * vr
            so_ref[t, h] = s_new
            o_scr[t:t + 1, _head(h)] = jnp.sum(qc * s_new, axis=0, keepdims=True)
    for h in range(RET_HEADS):
        o = o_scr[:, _head(h)]
        o = o * lax.rsqrt(jnp.mean(o * o, axis=-1, keepdims=True) + EPS)
        mix_ref[:, _head(h)] = o * jax.nn.silu(g_ref[:, _head(h)])
    u = u_ref[...]
    y = u * cw_ref[CONV_TAIL:CONV_WIDTH, :] + cb_ref[...]
    for t in range(CONV_TAIL):
        y = y + sc_ref[t] * cw_ref[t:t + 1, :]
    mix_ref[:, RET_W:] = _layer_norm_swish(y, lg_ref[...], lb_ref[...])
    for t in range(CONV_TAIL - 1):
        sco_ref[t] = sc_ref[t + 1]
    sco_ref[CONV_TAIL - 1] = u


def _sample_mix(qt3, kt3, v, g, u, gam, state_ret, state_conv, conv_w, conv_b, ln_g, ln_b):
    n = v.shape[0]
    nb = SAMPLE_BLOCK
    const = lambda shape: pl.BlockSpec(shape, lambda i: (0,) * len(shape))
    rows = lambda width: pl.BlockSpec((nb, width), lambda i: (i, 0))
    out_shapes = (
        jax.ShapeDtypeStruct((n, D_MODEL), F32),
        jax.ShapeDtypeStruct(state_ret.shape, F32),
        jax.ShapeDtypeStruct(state_conv.shape, F32),
    )
    st_spec = pl.BlockSpec((nb, RET_HEADS, HEAD_DIM, HEAD_DIM), lambda i: (i, 0, 0, 0))
    cv_spec = pl.BlockSpec((CONV_TAIL, nb, CONV_CH), lambda i: (0, i, 0))
    return pl.pallas_call(
        _sample_mix_kernel,
        grid=(n // nb,),
        in_specs=[
            pl.BlockSpec((None, RET_W, nb), lambda i: (i, 0, 0)),
            pl.BlockSpec((None, RET_W, nb), lambda i: (i, 0, 0)),
            rows(RET_W), rows(RET_W), rows(CONV_CH), const(gam.shape), st_spec, cv_spec,
            const(conv_w.shape), const(conv_b.shape), const(ln_g.shape), const(ln_b.shape),
        ],
        out_specs=(rows(D_MODEL), st_spec, cv_spec),
        out_shape=out_shapes,
        scratch_shapes=[pltpu.VMEM((nb, RET_W), F32)],
        compiler_params=pltpu.CompilerParams(
            dimension_semantics=("arbitrary",), vmem_limit_bytes=32 * MIB),
        name="sample_mix",
    )(qt3, kt3, v, g, u, gam, state_ret, state_conv, conv_w, conv_b, ln_g, ln_b)


def _out_ffn_kernel(mix_ref, x_ref, wo_ref, wi_ref, wd_ref, npost_ref, nfpre_ref, nfpost_ref, y_ref):
    a = _dot(mix_ref[...].astype(BF16), wo_ref[...])
    h1 = x_ref[...] + _rms(a, npost_ref[...])
    xf = _rms(h1, nfpre_ref[...]).astype(BF16)
    dn = None
    for c0 in range(0, D_FF, FFN_CHUNK):
        c1 = min(c0 + FFN_CHUNK, D_FF)
        gate = _dot(xf, wi_ref[:, c0:c1])
        up = _dot(xf, wi_ref[:, D_FF + c0:D_FF + c1])
        part = _dot((jax.nn.silu(gate) * up).astype(BF16), wd_ref[c0:c1, :])
        dn = part if dn is None else dn + part
    y_ref[...] = h1 + _rms(dn, nfpost_ref[...])


def _out_ffn(mix, x2d, w_out, w_ffn_in, w_ffn_out, n_post, n_fpre, n_fpost, tile):
    n = x2d.shape[0]
    const = lambda shape: pl.BlockSpec(shape, lambda i: (0,) * len(shape),
                                       pipeline_mode=pl.Buffered(1))
    rows = pl.BlockSpec((tile, D_MODEL), lambda i: (i, 0))
    return pl.pallas_call(
        _out_ffn_kernel,
        grid=(n // tile,),
        in_specs=[rows, rows, const(w_out.shape), const(w_ffn_in.shape), const(w_ffn_out.shape),
                  const(n_post.shape), const(n_fpre.shape), const(n_fpost.shape)],
        out_specs=rows,
        out_shape=jax.ShapeDtypeStruct((n, D_MODEL), F32),
        compiler_params=pltpu.CompilerParams(
            dimension_semantics=("arbitrary",), vmem_limit_bytes=52 * MIB),
        name="out_ffn",
    )(mix, x2d, w_out, w_ffn_in, w_ffn_out, n_post, n_fpre, n_fpost)


def _decode_ffn_kernel(mix_ref, x_ref, woc_ref, wic_ref, wdc_ref, npost_ref, nfpre_ref, nfpost_ref,
                       y_ref, wob_ref, wib_ref, wdb_ref, wo_scr, wi_scr, wd_scr):
    i = pl.program_id(0)
    n_chunks = pl.num_programs(0) - 1

    @pl.when(i < n_chunks)
    def _():
        for src, dst, out in ((woc_ref, wo_scr, wob_ref), (wic_ref, wi_scr, wib_ref),
                              (wdc_ref, wd_scr, wdb_ref)):
            rows = src.shape[0]
            chunk = src[...].astype(BF16)
            dst[pl.ds(pl.multiple_of(i * rows, rows), rows), :] = chunk
            out[...] = chunk

    @pl.when(i == n_chunks)
    def _():
        _out_ffn_kernel(mix_ref, x_ref, wo_scr, wi_scr, wd_scr, npost_ref, nfpre_ref, nfpost_ref, y_ref)


def _decode_ffn(mix, x2d, w_out, w_ffn_in, w_ffn_out, n_post, n_fpre, n_fpost):
    n = x2d.shape[0]
    nc = WEIGHT_CHUNKS
    last = nc - 1
    const = lambda shape: pl.BlockSpec(shape, lambda i: (0,) * len(shape))
    chunk = lambda w: pl.BlockSpec((w.shape[0] // nc, w.shape[1]), lambda i: (jnp.minimum(i, last), 0))
    weights = (w_out, w_ffn_in, w_ffn_out)
    assert all(w.shape[0] % (nc * 16) == 0 for w in weights)
    return pl.pallas_call(
        _decode_ffn_kernel,
        grid=(nc + 1,),
        in_specs=[const(mix.shape), const(x2d.shape)] + [chunk(w) for w in weights]
                 + [const(n_post.shape), const(n_fpre.shape), const(n_fpost.shape)],
        out_specs=(const((n, D_MODEL)),) + tuple(chunk(w) for w in weights),
        out_shape=(jax.ShapeDtypeStruct((n, D_MODEL), F32),)
                  + tuple(jax.ShapeDtypeStruct(w.shape, BF16) for w in weights),
        scratch_shapes=[pltpu.VMEM(w.shape, BF16) for w in weights],
        compiler_params=pltpu.CompilerParams(
            dimension_semantics=("arbitrary",), vmem_limit_bytes=52 * MIB),
        name="decode_ffn",
    )(mix, x2d, w_out, w_ffn_in, w_ffn_out, n_post, n_fpre, n_fpost)


def _rotary_tables(pos):
    half = HEAD_DIM // 2
    inv = 1.0 / (ROPE_BASE ** np.linspace(0.0, 1.0, half))
    ang = np.asarray(pos, np.float64)[:, None] * inv[None, :]
    cos, sin = np.cos(ang), np.sin(ang)
    return (np.concatenate([cos, cos], axis=-1).astype(np.float32),
            np.concatenate([-sin, sin], axis=-1).astype(np.float32))


def _decay_tables():
    lg = np.log(1.0 - 2.0 ** (-5.0 - np.arange(RET_HEADS, dtype=np.float64)))
    idx = np.arange(CHUNK, dtype=np.float64)
    diff = idx[:, None] - idx[None, :]
    mask = np.where(diff[None] >= 0, np.exp(lg[:, None, None] * np.maximum(diff, 0.0)[None]), 0.0)
    wide = lambda col: np.broadcast_to(col[:, :, None], (RET_HEADS, CHUNK, HEAD_DIM))
    dec_end = wide(np.exp(lg[:, None] * (CHUNK - 1 - idx)[None, :]))
    dec_start = wide(np.exp(lg[:, None] * (idx + 1.0)[None, :]))
    chunk_dec = np.broadcast_to(np.exp(lg * CHUNK)[:, None, None], (RET_HEADS, HEAD_DIM, HEAD_DIM))
    gam = np.broadcast_to(np.exp(lg)[:, None], (RET_HEADS, HEAD_DIM))
    return tuple(np.ascontiguousarray(t, np.float32) for t in (mask, dec_start, dec_end, chunk_dec, gam))


def kernel(x_prompt, x_sample, state_ret, state_conv, meta_tokens, norm_mix_pre, norm_mix_post,
           norm_ffn_pre, norm_ffn_post, w_in, conv_w, conv_b, conv_ln_g, conv_ln_b, w_out,
           w_ffn_in, w_ffn_out):
    batch, seq, _ = x_prompt.shape
    n_dec, dec_seq, _ = x_sample.shape
    depth = w_in.shape[0]
    assert depth == 1 and dec_seq == 1 and seq % PROMPT_TILE == 0
    assert n_dec % SAMPLE_BLOCK == 0 and (batch * seq) % FFN_TILE == 0

    row = lambda v: v.reshape(1, -1)
    nw_pre, nw_post = row(norm_mix_pre[0]), row(norm_mix_post[0])
    nf_pre, nf_post = row(norm_ffn_pre[0]), row(norm_ffn_post[0])
    w_in_b = w_in[0].astype(BF16)
    cw, cb, lg, lb = conv_w[0], row(conv_b[0]), row(conv_ln_g[0]), row(conv_ln_b[0])

    mask, dec_start, dec_end, chunk_dec, gam = _decay_tables()
    pos_small = np.concatenate([
        np.full((n_dec,), PAST_LEN), np.zeros((CHUNK - N_META,)), np.arange(N_META)])
    cos_s, sin_s = _rotary_tables(pos_small)
    cos_p, sin_p = _rotary_tables(N_META + np.arange(seq))

    xs = jnp.concatenate([
        x_sample.reshape(n_dec, D_MODEL),
        jnp.zeros((CHUNK - N_META, D_MODEL), x_sample.dtype),
        meta_tokens.astype(x_prompt.dtype)], axis=0)
    v_s, g_s, u_s, qt, kt, r0, tail0 = _small_inproj(
        xs, nw_pre, w_in_b, cos_s, sin_s, dec_end, n_dec)

    x2d = x_prompt.reshape(batch * seq, D_MODEL)
    mix_p, ret_p, conv_p = _prompt_mix(
        x2d, nw_pre, w_in_b, cos_p, sin_p, mask, dec_start, dec_end, chunk_dec, r0, tail0,
        cw, cb, lg, lb, batch, seq)

    blocks = n_dec // SAMPLE_BLOCK
    to_blocks = lambda t: t.reshape(RET_W, blocks, SAMPLE_BLOCK).transpose(1, 0, 2)
    mix_s, ret_s, conv_s = _sample_mix(
        to_blocks(qt), to_blocks(kt), v_s, g_s, u_s, gam, state_ret[0],
        state_conv[0].transpose(1, 0, 2), cw, cb, lg, lb)
    conv_s = conv_s.transpose(1, 0, 2)

    y_s, w_out_b, w_fi_b, w_fo_b = _decode_ffn(
        mix_s, x_sample.reshape(n_dec, D_MODEL), w_out[0], w_ffn_in[0], w_ffn_out[0],
        nw_post, nf_pre, nf_post)
    y_p = _out_ffn(mix_p, x2d, w_out_b, w_fi_b, w_fo_b, nw_post, nf_pre, nf_post, FFN_TILE)

    return (y_p.reshape(batch, seq, D_MODEL), y_s.reshape(n_dec, 1, D_MODEL),
            ret_p[None], conv_p[None], ret_s[None], conv_s[None])
```

```python
import jax
import jax.numpy as jnp
import numpy as np
from jax import lax
from jax.experimental import pallas as pl
from jax.experimental.pallas import tpu as pltpu

F32 = jnp.float32
BF16 = jnp.bfloat16

D_MODEL = 1024
N_META = 16
PAST_LEN = 16384
RET_HEADS = 4
HEAD_DIM = 128
RET_W = RET_HEADS * HEAD_DIM
CONV_CH = D_MODEL - RET_W
CONV_WIDTH = 31
CONV_TAIL = CONV_WIDTH - 1
CHUNK = 128
ROPE_BASE = 10000.0
D_FF = 2816
IN_COLS = 4 * RET_W + 2 * CONV_CH
EPS = 1e-6
K_SCALE = HEAD_DIM ** -0.5

SUBLANES = 8
TAIL_ROWS = 32
TAIL_SKIP = TAIL_ROWS - CONV_TAIL

PROMPT_TILE = 512
CONV_ROW_STRIDE = 4
SAMPLE_BLOCK = 16
FFN_TILE = 512
FFN_CHUNK = 1024
WEIGHT_CHUNKS = 8
MIB = 1024 * 1024


def _hbm(*xs):
    return tuple(pltpu.with_memory_space_constraint(jnp.asarray(x), pltpu.HBM) for x in xs)


def _rms(x, w):
    return x * lax.rsqrt(jnp.mean(x * x, axis=-1, keepdims=True) + EPS) * w


def _rot(xh, cos2, sin2):
    return xh * cos2 + pltpu.roll(xh, HEAD_DIM // 2, 1) * sin2


def _head(i):
    return slice(i * HEAD_DIM, (i + 1) * HEAD_DIM)


def _dot(a, b):
    return jnp.dot(a, b, preferred_element_type=F32)


def _dot_nt(a, b):
    return lax.dot_general(a, b, (((1,), (1,)), ((), ())), preferred_element_type=F32)


def _dot_tn(a, b):
    return lax.dot_general(a, b, (((0,), (0,)), ((), ())), preferred_element_type=F32)


def _layer_norm_swish(y, g, b):
    mu = jnp.mean(y, axis=-1, keepdims=True)
    var = jnp.mean(jnp.square(y - mu), axis=-1, keepdims=True)
    yn = (y - mu) * lax.rsqrt(var + EPS) * g + b
    return jax.nn.silu(yn)


def _small_inproj_kernel(x_ref, nw_ref, w_ref, cos_ref, sin_ref, de_ref,
                         v_ref, g_ref, u_ref, qt_ref, kt_ref, r0_ref, tail0_ref,
                         p_scr):
    n = v_ref.shape[0]
    xn = _rms(x_ref[...], nw_ref[...]).astype(BF16)
    p_scr[...] = _dot(xn, w_ref[...])
    dec = slice(0, n)
    met = slice(n, n + CHUNK)
    for h in range(RET_HEADS):
        q = _rot(p_scr[dec, _head(h)], cos_ref[dec, :], sin_ref[dec, :])
        k = _rot(p_scr[dec, _head(RET_HEADS + h)], cos_ref[dec, :], sin_ref[dec, :]) * K_SCALE
        qt_ref[_head(h), :] = q.T
        kt_ref[_head(h), :] = k.T
        km = _rot(p_scr[met, _head(RET_HEADS + h)], cos_ref[met, :], sin_ref[met, :]) * K_SCALE
        vm = p_scr[met, _head(2 * RET_HEADS + h)]
        r0_ref[h] = _dot_tn((km * de_ref[h]).astype(BF16), vm.astype(BF16))
    v_ref[...] = p_scr[dec, 2 * RET_W:3 * RET_W]
    g_ref[...] = p_scr[dec, 3 * RET_W:4 * RET_W]
    a = p_scr[:, 4 * RET_W:4 * RET_W + CONV_CH]
    b = p_scr[:, 4 * RET_W + CONV_CH:]
    u = a * jax.nn.sigmoid(b)
    u_ref[...] = u[dec, :]
    tail0_ref[...] = u[n + CHUNK - TAIL_ROWS:n + CHUNK, :]


def _small_inproj(xs, nw, w_in, cos_s, sin_s, dec_end, n_dec):
    rows = xs.shape[0]
    full = lambda shape: pl.BlockSpec(shape, lambda i: (0,) * len(shape))
    out_shapes = (
        jax.ShapeDtypeStruct((n_dec, RET_W), F32),
        jax.ShapeDtypeStruct((n_dec, RET_W), F32),
        jax.ShapeDtypeStruct((n_dec, CONV_CH), F32),
        jax.ShapeDtypeStruct((RET_W, n_dec), F32),
        jax.ShapeDtypeStruct((RET_W, n_dec), F32),
        jax.ShapeDtypeStruct((RET_HEADS, HEAD_DIM, HEAD_DIM), F32),
        jax.ShapeDtypeStruct((TAIL_ROWS, CONV_CH), F32),
    )
    return pl.pallas_call(
        _small_inproj_kernel,
        grid=(1,),
        in_specs=[full(xs.shape), full(nw.shape), full(w_in.shape), full(cos_s.shape),
                  full(sin_s.shape), full(dec_end.shape)],
        out_specs=tuple(full(s.shape) for s in out_shapes),
        out_shape=out_shapes,
        scratch_shapes=[pltpu.VMEM((rows, IN_COLS), F32)],
        compiler_params=pltpu.CompilerParams(
            dimension_semantics=("arbitrary",), vmem_limit_bytes=40 * MIB),
        name="small_inproj",
    )(*_hbm(xs, nw, w_in, cos_s, sin_s, dec_end))


def _prompt_mix_kernel(x_ref, nw_ref, w_ref, cos_ref, sin_ref, mask_ref, ds_ref, de_ref, cd_ref,
                       r0_ref, tail0_ref, cw_ref, cb_ref, lg_ref, lb_ref,
                       mix_ref, retp_ref, convp_ref,
                       p_scr, r_scr, u_scr, y_scr):
    tile = x_ref.shape[0]
    j = pl.program_id(1)

    n_slabs = CONV_CH // HEAD_DIM

    @pl.when(j == 0)
    def _():
        r_scr[...] = r0_ref[...]
        for c in range(n_slabs):
            u_scr[c, 0:TAIL_ROWS, :] = tail0_ref[:, _head(c)]

    xn = _rms(x_ref[...], nw_ref[...]).astype(BF16)

    ab = _dot(xn, w_ref[:, 4 * RET_W:])
    u = ab[:, :CONV_CH] * jax.nn.sigmoid(ab[:, CONV_CH:])
    for c in range(n_slabs):
        u_scr[c, TAIL_ROWS:TAIL_ROWS + tile, :] = u[:, _head(c)]
    p_scr[...] = _dot(xn, w_ref[:, :4 * RET_W])
    block = CONV_ROW_STRIDE * SUBLANES
    for c in range(n_slabs):
        taps = [cw_ref[t:t + 1, _head(c)] for t in range(CONV_WIDTH)]
        bias = cb_ref[:, _head(c)]
        for b0 in range(0, tile, block):
            acc = [None] * CONV_ROW_STRIDE
            for p in range(CONV_ROW_STRIDE + CONV_WIDTH - 1):
                win = u_scr[c, pl.ds(TAIL_SKIP + b0 + p, SUBLANES, stride=CONV_ROW_STRIDE), :]
                for r in range(CONV_ROW_STRIDE):
                    t = p - r
                    if 0 <= t < CONV_WIDTH:
                        term = taps[t] * win
                        acc[r] = term if acc[r] is None else acc[r] + term
            for r in range(CONV_ROW_STRIDE):
                y_scr[c, pl.ds(b0 + r, SUBLANES, stride=CONV_ROW_STRIDE), :] = acc[r] + bias
    ys = [y_scr[c] for c in range(n_slabs)]
    mu = sum(jnp.sum(y, axis=-1, keepdims=True) for y in ys) * (1.0 / CONV_CH)
    dev = [y - mu for y in ys]
    var = sum(jnp.sum(d * d, axis=-1, keepdims=True) for d in dev) * (1.0 / CONV_CH)
    inv = lax.rsqrt(var + EPS)
    for c in range(n_slabs):
        yn = dev[c] * inv * lg_ref[:, _head(c)] + lb_ref[:, _head(c)]
        mix_ref[:, RET_W + c * HEAD_DIM:RET_W + (c + 1) * HEAD_DIM] = jax.nn.silu(yn).astype(BF16)
        u_scr[c, 0:TAIL_ROWS, :] = u_scr[c, tile:tile + TAIL_ROWS, :]

    for c in range(tile // CHUNK):
        rows = slice(c * CHUNK, (c + 1) * CHUNK)
        cos2 = cos_ref[rows, :]
        sin2 = sin_ref[rows, :]
        for h in range(RET_HEADS):
            q = _rot(p_scr[rows, _head(h)], cos2, sin2)
            k = _rot(p_scr[rows, _head(RET_HEADS + h)], cos2, sin2) * K_SCALE
            vb = p_scr[rows, _head(2 * RET_HEADS + h)].astype(BF16)
            gate = p_scr[rows, _head(3 * RET_HEADS + h)]
            qb = q.astype(BF16)
            scores = _dot_nt(qb, k.astype(BF16)) * mask_ref[h]
            r_prev = r_scr[h]
            o = _dot(scores.astype(BF16), vb) + _dot(qb, r_prev.astype(BF16)) * ds_ref[h]
            kv = _dot_tn((k * de_ref[h]).astype(BF16), vb)
            r_scr[h] = cd_ref[h] * r_prev + kv
            o = o * lax.rsqrt(jnp.mean(o * o, axis=-1, keepdims=True) + EPS)
            mix_ref[rows, _head(h)] = (o * jax.nn.silu(gate)).astype(BF16)

    @pl.when(j == pl.num_programs(1) - 1)
    def _():
        retp_ref[...] = r_scr[...]
        for c in range(n_slabs):
            convp_ref[:, _head(c)] = u_scr[c, TAIL_SKIP:TAIL_ROWS, :]


def _prompt_mix(x2d, nw, w_in, cos_p, sin_p, mask, dec_start, dec_end, chunk_dec, r0, tail0,
                conv_w, conv_b, ln_g, ln_b, batch, seq):
    tile = PROMPT_TILE
    nt = seq // tile
    const = lambda shape: pl.BlockSpec(shape, lambda b, j: (0,) * len(shape))
    out_shapes = (
        jax.ShapeDtypeStruct((batch * seq, D_MODEL), BF16),
        jax.ShapeDtypeStruct((batch, RET_HEADS, HEAD_DIM, HEAD_DIM), F32),
        jax.ShapeDtypeStruct((batch, CONV_TAIL, CONV_CH), F32),
    )
    return pl.pallas_call(
        _prompt_mix_kernel,
        grid=(batch, nt),
        in_specs=[
            pl.BlockSpec((tile, D_MODEL), lambda b, j: (b * nt + j, 0)),
            const(nw.shape), const(w_in.shape),
            pl.BlockSpec((tile, HEAD_DIM), lambda b, j: (j, 0)),
            pl.BlockSpec((tile, HEAD_DIM), lambda b, j: (j, 0)),
            const(mask.shape), const(dec_start.shape), const(dec_end.shape), const(chunk_dec.shape),
            const(r0.shape), const(tail0.shape), const(conv_w.shape), const(conv_b.shape),
            const(ln_g.shape), const(ln_b.shape),
        ],
        out_specs=(
            pl.BlockSpec((tile, D_MODEL), lambda b, j: (b * nt + j, 0)),
            pl.BlockSpec((None, RET_HEADS, HEAD_DIM, HEAD_DIM), lambda b, j: (b, 0, 0, 0)),
            pl.BlockSpec((None, CONV_TAIL, CONV_CH), lambda b, j: (b, 0, 0)),
        ),
        out_shape=out_shapes,
        scratch_shapes=[
            pltpu.VMEM((tile, 4 * RET_W), F32),
            pltpu.VMEM((RET_HEADS, HEAD_DIM, HEAD_DIM), F32),
            pltpu.VMEM((CONV_CH // HEAD_DIM, TAIL_ROWS + tile, HEAD_DIM), F32),
            pltpu.VMEM((CONV_CH // HEAD_DIM, tile, HEAD_DIM), F32),
        ],
        compiler_params=pltpu.CompilerParams(
            dimension_semantics=("arbitrary", "arbitrary"), vmem_limit_bytes=48 * MIB),
        name="prompt_mix",
    )(*_hbm(x2d, nw, w_in, cos_p, sin_p, mask, dec_start, dec_end, chunk_dec, r0, tail0,
            conv_w, conv_b, ln_g, ln_b))


def _sample_mix_kernel(qt_ref, kt_ref, v_ref, g_ref, u_ref, gam_ref, sr_ref, sc_ref,
                       cw_ref, cb_ref, lg_ref, lb_ref,
                       mix_ref, so_ref, sco_ref, o_scr):
    nb = v_ref.shape[0]
    for t in range(nb):
        for h in range(RET_HEADS):
            kc = kt_ref[_head(h), t:t + 1]
            qc = qt_ref[_head(h), t:t + 1]
            vr = v_ref[t:t + 1, _head(h)]
            s_new = gam_ref[h:h + 1, :] * sr_ref[t, h] + kc * vr
            so_ref[t, h] = s_new
            o_scr[t:t + 1, _head(h)] = jnp.sum(qc * s_new, axis=0, keepdims=True)
    for h in range(RET_HEADS):
        o = o_scr[:, _head(h)]
        o = o * lax.rsqrt(jnp.mean(o * o, axis=-1, keepdims=True) + EPS)
        mix_ref[:, _head(h)] = o * jax.nn.silu(g_ref[:, _head(h)])
    u = u_ref[...]
    y = u * cw_ref[CONV_TAIL:CONV_WIDTH, :] + cb_ref[...]
    for t in range(CONV_TAIL):
        y = y + sc_ref[t] * cw_ref[t:t + 1, :]
    mix_ref[:, RET_W:] = _layer_norm_swish(y, lg_ref[...], lb_ref[...])
    for t in range(CONV_TAIL - 1):
        sco_ref[t] = sc_ref[t + 1]
    sco_ref[CONV_TAIL - 1] = u


def _sample_mix(qt3, kt3, v, g, u, gam, state_ret, state_conv, conv_w, conv_b, ln_g, ln_b):
    n = v.shape[0]
    nb = SAMPLE_BLOCK
    const = lambda shape: pl.BlockSpec(shape, lambda i: (0,) * len(shape))
    rows = lambda width: pl.BlockSpec((nb, width), lambda i: (i, 0))
    out_shapes = (
        jax.ShapeDtypeStruct((n, D_MODEL), F32),
        jax.ShapeDtypeStruct(state_ret.shape, F32),
        jax.ShapeDtypeStruct(state_conv.shape, F32),
    )
    st_spec = pl.BlockSpec((nb, RET_HEADS, HEAD_DIM, HEAD_DIM), lambda i: (i, 0, 0, 0))
    cv_spec = pl.BlockSpec((CONV_TAIL, nb, CONV_CH), lambda i: (0, i, 0))
    return pl.pallas_call(
        _sample_mix_kernel,
        grid=(n // nb,),
        in_specs=[
            pl.BlockSpec((None, RET_W, nb), lambda i: (i, 0, 0)),
            pl.BlockSpec((None, RET_W, nb), lambda i: (i, 0, 0)),
            rows(RET_W), rows(RET_W), rows(CONV_CH), const(gam.shape), st_spec, cv_spec,
            const(conv_w.shape), const(conv_b.shape), const(ln_g.shape), const(ln_b.shape),
        ],
        out_specs=(rows(D_MODEL), st_spec, cv_spec),
        out_shape=out_shapes,
        scratch_shapes=[pltpu.VMEM((nb, RET_W), F32)],
        compiler_params=pltpu.CompilerParams(
            dimension_semantics=("arbitrary",), vmem_limit_bytes=32 * MIB),
        name="sample_mix",
    )(*_hbm(qt3, kt3, v, g, u, gam, state_ret, state_conv, conv_w, conv_b, ln_g, ln_b))


def _out_ffn_kernel(mix_ref, x_ref, wo_ref, wi_ref, wd_ref, npost_ref, nfpre_ref, nfpost_ref, y_ref):
    a = _dot(mix_ref[...].astype(BF16), wo_ref[...])
    h1 = x_ref[...] + _rms(a, npost_ref[...])
    xf = _rms(h1, nfpre_ref[...]).astype(BF16)
    dn = None
    for c0 in range(0, D_FF, FFN_CHUNK):
        c1 = min(c0 + FFN_CHUNK, D_FF)
        gate = _dot(xf, wi_ref[:, c0:c1])
        up = _dot(xf, wi_ref[:, D_FF + c0:D_FF + c1])
        part = _dot((jax.nn.silu(gate) * up).astype(BF16), wd_ref[c0:c1, :])
        dn = part if dn is None else dn + part
    y_ref[...] = h1 + _rms(dn, nfpost_ref[...])


def _out_ffn(mix, x2d, w_out, w_ffn_in, w_ffn_out, n_post, n_fpre, n_fpost, tile):
    n = x2d.shape[0]
    const = lambda shape: pl.BlockSpec(shape, lambda i: (0,) * len(shape),
                                       pipeline_mode=pl.Buffered(1))
    rows = pl.BlockSpec((tile, D_MODEL), lambda i: (i, 0))
    return pl.pallas_call(
        _out_ffn_kernel,
        grid=(n // tile,),
        in_specs=[rows, rows, const(w_out.shape), const(w_ffn_in.shape), const(w_ffn_out.shape),
                  const(n_post.shape), const(n_fpre.shape), const(n_fpost.shape)],
        out_specs=rows,
        out_shape=jax.ShapeDtypeStruct((n, D_MODEL), F32),
        compiler_params=pltpu.CompilerParams(
            dimension_semantics=("arbitrary",), vmem_limit_bytes=52 * MIB),
        name="out_ffn",
    )(*_hbm(mix, x2d, w_out, w_ffn_in, w_ffn_out, n_post, n_fpre, n_fpost))


def _decode_ffn_kernel(mix_ref, x_ref, woc_ref, wic_ref, wdc_ref, npost_ref, nfpre_ref, nfpost_ref,
                       y_ref, wob_ref, wib_ref, wdb_ref, wo_scr, wi_scr, wd_scr):
    i = pl.program_id(0)
    n_chunks = pl.num_programs(0) - 1

    @pl.when(i < n_chunks)
    def _():
        for src, dst, out in ((woc_ref, wo_scr, wob_ref), (wic_ref, wi_scr, wib_ref),
                              (wdc_ref, wd_scr, wdb_ref)):
            rows = src.shape[0]
            chunk = src[...].astype(BF16)
            dst[pl.ds(pl.multiple_of(i * rows, rows), rows), :] = chunk
            out[...] = chunk

    @pl.when(i == n_chunks)
    def _():
        _out_ffn_kernel(mix_ref, x_ref, wo_scr, wi_scr, wd_scr, npost_ref, nfpre_ref, nfpost_ref, y_ref)


def _decode_ffn(mix, x2d, w_out, w_ffn_in, w_ffn_out, n_post, n_fpre, n_fpost):
    n = x2d.shape[0]
    nc = WEIGHT_CHUNKS
    last = nc - 1
    const = lambda shape: pl.BlockSpec(shape, lambda i: (0,) * len(shape))
    chunk = lambda w: pl.BlockSpec((w.shape[0] // nc, w.shape[1]), lambda i: (jnp.minimum(i, last), 0))
    weights = (w_out, w_ffn_in, w_ffn_out)
    assert all(w.shape[0] % (nc * 16) == 0 for w in weights)
    return pl.pallas_call(
        _decode_ffn_kernel,
        grid=(nc + 1,),
        in_specs=[const(mix.shape), const(x2d.shape)] + [chunk(w) for w in weights]
                 + [const(n_post.shape), const(n_fpre.shape), const(n_fpost.shape)],
        out_specs=(const((n, D_MODEL)),) + tuple(chunk(w) for w in weights),
        out_shape=(jax.ShapeDtypeStruct((n, D_MODEL), F32),)
                  + tuple(jax.ShapeDtypeStruct(w.shape, BF16) for w in weights),
        scratch_shapes=[pltpu.VMEM(w.shape, BF16) for w in weights],
        compiler_params=pltpu.CompilerParams(
            dimension_semantics=("arbitrary",), vmem_limit_bytes=52 * MIB),
        name="decode_ffn",
    )(*_hbm(mix, x2d, w_out, w_ffn_in, w_ffn_out, n_post, n_fpre, n_fpost))


def _rotary_tables(pos):
    half = HEAD_DIM // 2
    inv = 1.0 / (ROPE_BASE ** np.linspace(0.0, 1.0, half))
    ang = np.asarray(pos, np.float64)[:, None] * inv[None, :]
    cos, sin = np.cos(ang), np.sin(ang)
    return (np.concatenate([cos, cos], axis=-1).astype(np.float32),
            np.concatenate([-sin, sin], axis=-1).astype(np.float32))


def _decay_tables():
    lg = np.log(1.0 - 2.0 ** (-5.0 - np.arange(RET_HEADS, dtype=np.float64)))
    idx = np.arange(CHUNK, dtype=np.float64)
    diff = idx[:, None] - idx[None, :]
    mask = np.where(diff[None] >= 0, np.exp(lg[:, None, None] * np.maximum(diff, 0.0)[None]), 0.0)
    wide = lambda col: np.broadcast_to(col[:, :, None], (RET_HEADS, CHUNK, HEAD_DIM))
    dec_end = wide(np.exp(lg[:, None] * (CHUNK - 1 - idx)[None, :]))
    dec_start = wide(np.exp(lg[:, None] * (idx + 1.0)[None, :]))
    chunk_dec = np.broadcast_to(np.exp(lg * CHUNK)[:, None, None], (RET_HEADS, HEAD_DIM, HEAD_DIM))
    gam = np.broadcast_to(np.exp(lg)[:, None], (RET_HEADS, HEAD_DIM))
    return tuple(np.ascontiguousarray(t, np.float32) for t in (mask, dec_start, dec_end, chunk_dec, gam))


def kernel(x_prompt, x_sample, state_ret, state_conv, meta_tokens, norm_mix_pre, norm_mix_post,
           norm_ffn_pre, norm_ffn_post, w_in, conv_w, conv_b, conv_ln_g, conv_ln_b, w_out,
           w_ffn_in, w_ffn_out):
    batch, seq, _ = x_prompt.shape
    n_dec, dec_seq, _ = x_sample.shape
    depth = w_in.shape[0]
    assert depth == 1 and dec_seq == 1 and seq % PROMPT_TILE == 0
    assert n_dec % SAMPLE_BLOCK == 0 and (batch * seq) % FFN_TILE == 0

    row = lambda v: v.reshape(1, -1)
    nw_pre, nw_post = row(norm_mix_pre[0]), row(norm_mix_post[0])
    nf_pre, nf_post = row(norm_ffn_pre[0]), row(norm_ffn_post[0])
    w_in_b = w_in[0].astype(BF16)
    cw, cb, lg, lb = conv_w[0], row(conv_b[0]), row(conv_ln_g[0]), row(conv_ln_b[0])

    mask, dec_start, dec_end, chunk_dec, gam = _decay_tables()
    pos_small = np.concatenate([
        np.full((n_dec,), PAST_LEN), np.zeros((CHUNK - N_META,)), np.arange(N_META)])
    cos_s, sin_s = _rotary_tables(pos_small)
    cos_p, sin_p = _rotary_tables(N_META + np.arange(seq))

    xs = jnp.concatenate([
        x_sample.reshape(n_dec, D_MODEL),
        jnp.zeros((CHUNK - N_META, D_MODEL), x_sample.dtype),
        meta_tokens.astype(x_prompt.dtype)], axis=0)
    v_s, g_s, u_s, qt, kt, r0, tail0 = _small_inproj(
        xs, nw_pre, w_in_b, cos_s, sin_s, dec_end, n_dec)

    x2d = x_prompt.reshape(batch * seq, D_MODEL)
    mix_p, ret_p, conv_p = _prompt_mix(
        x2d, nw_pre, w_in_b, cos_p, sin_p, mask, dec_start, dec_end, chunk_dec, r0, tail0,
        cw, cb, lg, lb, batch, seq)

    blocks = n_dec // SAMPLE_BLOCK
    to_blocks = lambda t: t.reshape(RET_W, blocks, SAMPLE_BLOCK).transpose(1, 0, 2)
    mix_s, ret_s, conv_s = _sample_mix(
        to_blocks(qt), to_blocks(kt), v_s, g_s, u_s, gam, state_ret[0],
        state_conv[0].transpose(1, 0, 2), cw, cb, lg, lb)
    conv_s = conv_s.transpose(1, 0, 2)

    y_s, w_out_b, w_fi_b, w_fo_b = _decode_ffn(
        mix_s, x_sample.reshape(n_dec, D_MODEL), w_out[0], w_ffn_in[0], w_ffn_out[0],
        nw_post, nf_pre, nf_post)
    y_p = _out_ffn(mix_p, x2d, w_out_b, w_fi_b, w_fo_b, nw_post, nf_pre, nf_post, FFN_TILE)

    return (y_p.reshape(batch, seq, D_MODEL), y_s.reshape(n_dec, 1, D_MODEL),
            ret_p[None], conv_p[None], ret_s[None], conv_s[None])
```

```python
import jax
import jax.numpy as jnp
import numpy as np
from jax import lax
from jax.experimental import pallas as pl
from jax.experimental.pallas import tpu as pltpu

F32 = jnp.float32
BF16 = jnp.bfloat16

D_MODEL = 1024
N_META = 16
PAST_LEN = 16384
RET_HEADS = 4
HEAD_DIM = 128
RET_W = RET_HEADS * HEAD_DIM
CONV_CH = D_MODEL - RET_W
CONV_WIDTH = 31
CONV_TAIL = CONV_WIDTH - 1
CHUNK = 128
ROPE_BASE = 10000.0
D_FF = 2816
IN_COLS = 4 * RET_W + 2 * CONV_CH
EPS = 1e-6
K_SCALE = HEAD_DIM ** -0.5

SUBLANES = 8
TAIL_ROWS = 32
TAIL_SKIP = TAIL_ROWS - CONV_TAIL

PROMPT_TILE = 512
CONV_ROW_STRIDE = 4
SAMPLE_BLOCK = 16
FFN_TILE = 512
FFN_GROUP_ROWS = 256
FFN_CHUNK = 1024
WEIGHT_CHUNKS = 8
MIB = 1024 * 1024


def _hbm(*xs):
    return tuple(pltpu.with_memory_space_constraint(jnp.asarray(x), pltpu.HBM) for x in xs)


def _rms(x, w):
    return x * lax.rsqrt(jnp.mean(x * x, axis=-1, keepdims=True) + EPS) * w


def _rot(xh, cos2, sin2):
    return xh * cos2 + pltpu.roll(xh, HEAD_DIM // 2, 1) * sin2


def _head(i):
    return slice(i * HEAD_DIM, (i + 1) * HEAD_DIM)


def _dot(a, b):
    return jnp.dot(a, b, preferred_element_type=F32)


def _dot_nt(a, b):
    return lax.dot_general(a, b, (((1,), (1,)), ((), ())), preferred_element_type=F32)


def _dot_tn(a, b):
    return lax.dot_general(a, b, (((0,), (0,)), ((), ())), preferred_element_type=F32)


def _layer_norm_swish(y, g, b):
    mu = jnp.mean(y, axis=-1, keepdims=True)
    var = jnp.mean(jnp.square(y - mu), axis=-1, keepdims=True)
    yn = (y - mu) * lax.rsqrt(var + EPS) * g + b
    return jax.nn.silu(yn)


def _small_inproj_kernel(x_ref, nw_ref, w_ref, cos_ref, sin_ref, de_ref,
                         v_ref, g_ref, u_ref, qt_ref, kt_ref, r0_ref, tail0_ref,
                         p_scr):
    n = v_ref.shape[0]
    xn = _rms(x_ref[...], nw_ref[...]).astype(BF16)
    p_scr[...] = _dot(xn, w_ref[...])
    dec = slice(0, n)
    met = slice(n, n + CHUNK)
    for h in range(RET_HEADS):
        q = _rot(p_scr[dec, _head(h)], cos_ref[dec, :], sin_ref[dec, :])
        k = _rot(p_scr[dec, _head(RET_HEADS + h)], cos_ref[dec, :], sin_ref[dec, :]) * K_SCALE
        qt_ref[_head(h), :] = q.T
        kt_ref[_head(h), :] = k.T
        km = _rot(p_scr[met, _head(RET_HEADS + h)], cos_ref[met, :], sin_ref[met, :]) * K_SCALE
        vm = p_scr[met, _head(2 * RET_HEADS + h)]
        r0_ref[h] = _dot_tn((km * de_ref[h]).astype(BF16), vm.astype(BF16))
    v_ref[...] = p_scr[dec, 2 * RET_W:3 * RET_W]
    g_ref[...] = p_scr[dec, 3 * RET_W:4 * RET_W]
    a = p_scr[:, 4 * RET_W:4 * RET_W + CONV_CH]
    b = p_scr[:, 4 * RET_W + CONV_CH:]
    u = a * jax.nn.sigmoid(b)
    u_ref[...] = u[dec, :]
    tail0_ref[...] = u[n + CHUNK - TAIL_ROWS:n + CHUNK, :]


def _small_inproj(xs, nw, w_in, cos_s, sin_s, dec_end, n_dec):
    rows = xs.shape[0]
    full = lambda shape: pl.BlockSpec(shape, lambda i: (0,) * len(shape))
    out_shapes = (
        jax.ShapeDtypeStruct((n_dec, RET_W), F32),
        jax.ShapeDtypeStruct((n_dec, RET_W), F32),
        jax.ShapeDtypeStruct((n_dec, CONV_CH), F32),
        jax.ShapeDtypeStruct((RET_W, n_dec), F32),
        jax.ShapeDtypeStruct((RET_W, n_dec), F32),
        jax.ShapeDtypeStruct((RET_HEADS, HEAD_DIM, HEAD_DIM), F32),
        jax.ShapeDtypeStruct((TAIL_ROWS, CONV_CH), F32),
    )
    return pl.pallas_call(
        _small_inproj_kernel,
        grid=(1,),
        in_specs=[full(xs.shape), full(nw.shape), full(w_in.shape), full(cos_s.shape),
                  full(sin_s.shape), full(dec_end.shape)],
        out_specs=tuple(full(s.shape) for s in out_shapes),
        out_shape=out_shapes,
        scratch_shapes=[pltpu.VMEM((rows, IN_COLS), F32)],
        compiler_params=pltpu.CompilerParams(
            dimension_semantics=("arbitrary",), vmem_limit_bytes=40 * MIB),
        name="small_inproj",
    )(*_hbm(xs, nw, w_in, cos_s, sin_s, dec_end))


def _prompt_mix_kernel(x_ref, nw_ref, w_ref, cos_ref, sin_ref, mask_ref, ds_ref, de_ref, cd_ref,
                       r0_ref, tail0_ref, cw_ref, cb_ref, lg_ref, lb_ref,
                       mix_ref, retp_ref, convp_ref,
                       p_scr, r_scr, u_scr, y_scr):
    tile = x_ref.shape[0]
    j = pl.program_id(1)

    n_slabs = CONV_CH // HEAD_DIM

    @pl.when(j == 0)
    def _():
        r_scr[...] = r0_ref[...]
        for c in range(n_slabs):
            u_scr[c, 0:TAIL_ROWS, :] = tail0_ref[:, _head(c)]

    xn = _rms(x_ref[...], nw_ref[...]).astype(BF16)

    ab = _dot(xn, w_ref[:, 4 * RET_W:])
    u = ab[:, :CONV_CH] * jax.nn.sigmoid(ab[:, CONV_CH:])
    for c in range(n_slabs):
        u_scr[c, TAIL_ROWS:TAIL_ROWS + tile, :] = u[:, _head(c)]
    p_scr[...] = _dot(xn, w_ref[:, :4 * RET_W])
    block = CONV_ROW_STRIDE * SUBLANES
    for c in range(n_slabs):
        taps = [cw_ref[t:t + 1, _head(c)] for t in range(CONV_WIDTH)]
        bias = cb_ref[:, _head(c)]
        for b0 in range(0, tile, block):
            acc = [None] * CONV_ROW_STRIDE
            for p in range(CONV_ROW_STRIDE + CONV_WIDTH - 1):
                win = u_scr[c, pl.ds(TAIL_SKIP + b0 + p, SUBLANES, stride=CONV_ROW_STRIDE), :]
                for r in range(CONV_ROW_STRIDE):
                    t = p - r
                    if 0 <= t < CONV_WIDTH:
                        term = taps[t] * win
                        acc[r] = term if acc[r] is None else acc[r] + term
            for r in range(CONV_ROW_STRIDE):
                y_scr[c, pl.ds(b0 + r, SUBLANES, stride=CONV_ROW_STRIDE), :] = acc[r] + bias
    ys = [y_scr[c] for c in range(n_slabs)]
    mu = sum(jnp.sum(y, axis=-1, keepdims=True) for y in ys) * (1.0 / CONV_CH)
    dev = [y - mu for y in ys]
    var = sum(jnp.sum(d * d, axis=-1, keepdims=True) for d in dev) * (1.0 / CONV_CH)
    inv = lax.rsqrt(var + EPS)
    for c in range(n_slabs):
        yn = dev[c] * inv * lg_ref[:, _head(c)] + lb_ref[:, _head(c)]
        mix_ref[:, RET_W + c * HEAD_DIM:RET_W + (c + 1) * HEAD_DIM] = jax.nn.silu(yn).astype(BF16)
        u_scr[c, 0:TAIL_ROWS, :] = u_scr[c, tile:tile + TAIL_ROWS, :]

    for c in range(tile // CHUNK):
        rows = slice(c * CHUNK, (c + 1) * CHUNK)
        cos2 = cos_ref[rows, :]
        sin2 = sin_ref[rows, :]
        for h in range(RET_HEADS):
            q = _rot(p_scr[rows, _head(h)], cos2, sin2)
            k = _rot(p_scr[rows, _head(RET_HEADS + h)], cos2, sin2) * K_SCALE
            vb = p_scr[rows, _head(2 * RET_HEADS + h)].astype(BF16)
            gate = p_scr[rows, _head(3 * RET_HEADS + h)]
            qb = q.astype(BF16)
            scores = _dot_nt(qb, k.astype(BF16)) * mask_ref[h]
            r_prev = r_scr[h]
            o = _dot(scores.astype(BF16), vb) + _dot(qb, r_prev.astype(BF16)) * ds_ref[h]
            kv = _dot_tn((k * de_ref[h]).astype(BF16), vb)
            r_scr[h] = cd_ref[h] * r_prev + kv
            o = o * lax.rsqrt(jnp.mean(o * o, axis=-1, keepdims=True) + EPS)
            mix_ref[rows, _head(h)] = (o * jax.nn.silu(gate)).astype(BF16)

    @pl.when(j == pl.num_programs(1) - 1)
    def _():
        retp_ref[...] = r_scr[...]
        for c in range(n_slabs):
            convp_ref[:, _head(c)] = u_scr[c, TAIL_SKIP:TAIL_ROWS, :]


def _prompt_mix(x2d, nw, w_in, cos_p, sin_p, mask, dec_start, dec_end, chunk_dec, r0, tail0,
                conv_w, conv_b, ln_g, ln_b, batch, seq):
    tile = PROMPT_TILE
    nt = seq // tile
    const = lambda shape: pl.BlockSpec(shape, lambda b, j: (0,) * len(shape))
    out_shapes = (
        jax.ShapeDtypeStruct((batch * seq, D_MODEL), BF16),
        jax.ShapeDtypeStruct((batch, RET_HEADS, HEAD_DIM, HEAD_DIM), F32),
        jax.ShapeDtypeStruct((batch, CONV_TAIL, CONV_CH), F32),
    )
    return pl.pallas_call(
        _prompt_mix_kernel,
        grid=(batch, nt),
        in_specs=[
            pl.BlockSpec((tile, D_MODEL), lambda b, j: (b * nt + j, 0)),
            const(nw.shape), const(w_in.shape),
            pl.BlockSpec((tile, HEAD_DIM), lambda b, j: (j, 0)),
            pl.BlockSpec((tile, HEAD_DIM), lambda b, j: (j, 0)),
            const(mask.shape), const(dec_start.shape), const(dec_end.shape), const(chunk_dec.shape),
            const(r0.shape), const(tail0.shape), const(conv_w.shape), const(conv_b.shape),
            const(ln_g.shape), const(ln_b.shape),
        ],
        out_specs=(
            pl.BlockSpec((tile, D_MODEL), lambda b, j: (b * nt + j, 0)),
            pl.BlockSpec((None, RET_HEADS, HEAD_DIM, HEAD_DIM), lambda b, j: (b, 0, 0, 0)),
            pl.BlockSpec((None, CONV_TAIL, CONV_CH), lambda b, j: (b, 0, 0)),
        ),
        out_shape=out_shapes,
        scratch_shapes=[
            pltpu.VMEM((tile, 4 * RET_W), F32),
            pltpu.VMEM((RET_HEADS, HEAD_DIM, HEAD_DIM), F32),
            pltpu.VMEM((CONV_CH // HEAD_DIM, TAIL_ROWS + tile, HEAD_DIM), F32),
            pltpu.VMEM((CONV_CH // HEAD_DIM, tile, HEAD_DIM), F32),
        ],
        compiler_params=pltpu.CompilerParams(
            dimension_semantics=("arbitrary", "arbitrary"), vmem_limit_bytes=48 * MIB),
        name="prompt_mix",
    )(*_hbm(x2d, nw, w_in, cos_p, sin_p, mask, dec_start, dec_end, chunk_dec, r0, tail0,
            conv_w, conv_b, ln_g, ln_b))


def _sample_mix_kernel(qt_ref, kt_ref, v_ref, g_ref, u_ref, gam_ref, sr_ref, sc_ref,
                       cw_ref, cb_ref, lg_ref, lb_ref,
                       mix_ref, so_ref, sco_ref, o_scr):
    nb = v_ref.shape[0]
    for t in range(nb):
        for h in range(RET_HEADS):
            kc = kt_ref[_head(h), t:t + 1]
            qc = qt_ref[_head(h), t:t + 1]
            vr = v_ref[t:t + 1, _head(h)]
            s_new = gam_ref[h:h + 1, :] * sr_ref[t, h] + kc * vr
            so_ref[t, h] = s_new
            o_scr[t:t + 1, _head(h)] = jnp.sum(qc * s_new, axis=0, keepdims=True)
    for h in range(RET_HEADS):
        o = o_scr[:, _head(h)]
        o = o * lax.rsqrt(jnp.mean(o * o, axis=-1, keepdims=True) + EPS)
        mix_ref[:, _head(h)] = o * jax.nn.silu(g_ref[:, _head(h)])
    u = u_ref[...]
    y = u * cw_ref[CONV_TAIL:CONV_WIDTH, :] + cb_ref[...]
    for t in range(CONV_TAIL):
        y = y + sc_ref[t] * cw_ref[t:t + 1, :]
    mix_ref[:, RET_W:] = _layer_norm_swish(y, lg_ref[...], lb_ref[...])
    for t in range(CONV_TAIL - 1):
        sco_ref[t] = sc_ref[t + 1]
    sco_ref[CONV_TAIL - 1] = u


def _sample_mix(qt3, kt3, v, g, u, gam, state_ret, state_conv, conv_w, conv_b, ln_g, ln_b):
    n = v.shape[0]
    nb = SAMPLE_BLOCK
    const = lambda shape: pl.BlockSpec(shape, lambda i: (0,) * len(shape))
    rows = lambda width: pl.BlockSpec((nb, width), lambda i: (i, 0))
    out_shapes = (
        jax.ShapeDtypeStruct((n, D_MODEL), F32),
        jax.ShapeDtypeStruct(state_ret.shape, F32),
        jax.ShapeDtypeStruct(state_conv.shape, F32),
    )
    st_spec = pl.BlockSpec((nb, RET_HEADS, HEAD_DIM, HEAD_DIM), lambda i: (i, 0, 0, 0))
    cv_spec = pl.BlockSpec((CONV_TAIL, nb, CONV_CH), lambda i: (0, i, 0))
    return pl.pallas_call(
        _sample_mix_kernel,
        grid=(n // nb,),
        in_specs=[
            pl.BlockSpec((None, RET_W, nb), lambda i: (i, 0, 0)),
            pl.BlockSpec((None, RET_W, nb), lambda i: (i, 0, 0)),
            rows(RET_W), rows(RET_W), rows(CONV_CH), const(gam.shape), st_spec, cv_spec,
            const(conv_w.shape), const(conv_b.shape), const(ln_g.shape), const(ln_b.shape),
        ],
        out_specs=(rows(D_MODEL), st_spec, cv_spec),
        out_shape=out_shapes,
        scratch_shapes=[pltpu.VMEM((nb, RET_W), F32)],
        compiler_params=pltpu.CompilerParams(
            dimension_semantics=("arbitrary",), vmem_limit_bytes=32 * MIB),
        name="sample_mix",
    )(*_hbm(qt3, kt3, v, g, u, gam, state_ret, state_conv, conv_w, conv_b, ln_g, ln_b))


def _out_ffn_kernel(mix_ref, x_ref, wo_ref, wi_ref, wd_ref, npost_ref, nfpre_ref, nfpost_ref, y_ref):
    tile = x_ref.shape[0]
    n_groups = max(1, tile // FFN_GROUP_ROWS)
    groups = [slice(i * tile // n_groups, (i + 1) * tile // n_groups) for i in range(n_groups)]
    a = [_dot(mix_ref[r, :].astype(BF16), wo_ref[...]) for r in groups]
    h1 = [x_ref[r, :] + _rms(a[i], npost_ref[...]) for i, r in enumerate(groups)]
    xf = [_rms(h, nfpre_ref[...]).astype(BF16) for h in h1]
    dn = [None] * n_groups
    for c0 in range(0, D_FF, FFN_CHUNK):
        c1 = min(c0 + FFN_CHUNK, D_FF)
        for i in range(n_groups):
            gate = _dot(xf[i], wi_ref[:, c0:c1])
            up = _dot(xf[i], wi_ref[:, D_FF + c0:D_FF + c1])
            part = _dot((jax.nn.silu(gate) * up).astype(BF16), wd_ref[c0:c1, :])
            dn[i] = part if dn[i] is None else dn[i] + part
    for i, r in enumerate(groups):
        y_ref[r, :] = h1[i] + _rms(dn[i], nfpost_ref[...])


def _out_ffn(mix, x2d, w_out, w_ffn_in, w_ffn_out, n_post, n_fpre, n_fpost, tile):
    n = x2d.shape[0]
    const = lambda shape: pl.BlockSpec(shape, lambda i: (0,) * len(shape),
                                       pipeline_mode=pl.Buffered(1))
    rows = pl.BlockSpec((tile, D_MODEL), lambda i: (i, 0))
    return pl.pallas_call(
        _out_ffn_kernel,
        grid=(n // tile,),
        in_specs=[rows, rows, const(w_out.shape), const(w_ffn_in.shape), const(w_ffn_out.shape),
                  const(n_post.shape), const(n_fpre.shape), const(n_fpost.shape)],
        out_specs=rows,
        out_shape=jax.ShapeDtypeStruct((n, D_MODEL), F32),
        compiler_params=pltpu.CompilerParams(
            dimension_semantics=("arbitrary",), vmem_limit_bytes=52 * MIB),
        name="out_ffn",
    )(*_hbm(mix, x2d, w_out, w_ffn_in, w_ffn_out, n_post, n_fpre, n_fpost))


def _decode_ffn_kernel(mix_ref, x_ref, woc_ref, wic_ref, wdc_ref, npost_ref, nfpre_ref, nfpost_ref,
                       y_ref, wob_ref, wib_ref, wdb_ref, wo_scr, wi_scr, wd_scr):
    i = pl.program_id(0)
    n_chunks = pl.num_programs(0) - 1

    @pl.when(i < n_chunks)
    def _():
        for src, dst, out in ((woc_ref, wo_scr, wob_ref), (wic_ref, wi_scr, wib_ref),
                              (wdc_ref, wd_scr, wdb_ref)):
            rows = src.shape[0]
            chunk = src[...].astype(BF16)
            dst[pl.ds(pl.multiple_of(i * rows, rows), rows), :] = chunk
            out[...] = chunk

    @pl.when(i == n_chunks)
    def _():
        _out_ffn_kernel(mix_ref, x_ref, wo_scr, wi_scr, wd_scr, npost_ref, nfpre_ref, nfpost_ref, y_ref)


def _decode_ffn(mix, x2d, w_out, w_ffn_in, w_ffn_out, n_post, n_fpre, n_fpost):
    n = x2d.shape[0]
    nc = WEIGHT_CHUNKS
    last = nc - 1
    const = lambda shape: pl.BlockSpec(shape, lambda i: (0,) * len(shape))
    chunk = lambda w: pl.BlockSpec((w.shape[0] // nc, w.shape[1]), lambda i: (jnp.minimum(i, last), 0))
    weights = (w_out, w_ffn_in, w_ffn_out)
    assert all(w.shape[0] % (nc * 16) == 0 for w in weights)
    return pl.pallas_call(
        _decode_ffn_kernel,
        grid=(nc + 1,),
        in_specs=[const(mix.shape), const(x2d.shape)] + [chunk(w) for w in weights]
                 + [const(n_post.shape), const(n_fpre.shape), const(n_fpost.shape)],
        out_specs=(const((n, D_MODEL)),) + tuple(chunk(w) for w in weights),
        out_shape=(jax.ShapeDtypeStruct((n, D_MODEL), F32),)
                  + tuple(jax.ShapeDtypeStruct(w.shape, BF16) for w in weights),
        scratch_shapes=[pltpu.VMEM(w.shape, BF16) for w in weights],
        compiler_params=pltpu.CompilerParams(
            dimension_semantics=("arbitrary",), vmem_limit_bytes=52 * MIB),
        name="decode_ffn",
    )(*_hbm(mix, x2d, w_out, w_ffn_in, w_ffn_out, n_post, n_fpre, n_fpost))


def _rotary_tables(pos):
    half = HEAD_DIM // 2
    inv = 1.0 / (ROPE_BASE ** np.linspace(0.0, 1.0, half))
    ang = np.asarray(pos, np.float64)[:, None] * inv[None, :]
    cos, sin = np.cos(ang), np.sin(ang)
    return (np.concatenate([cos, cos], axis=-1).astype(np.float32),
            np.concatenate([-sin, sin], axis=-1).astype(np.float32))


def _decay_tables():
    lg = np.log(1.0 - 2.0 ** (-5.0 - np.arange(RET_HEADS, dtype=np.float64)))
    idx = np.arange(CHUNK, dtype=np.float64)
    diff = idx[:, None] - idx[None, :]
    mask = np.where(diff[None] >= 0, np.exp(lg[:, None, None] * np.maximum(diff, 0.0)[None]), 0.0)
    wide = lambda col: np.broadcast_to(col[:, :, None], (RET_HEADS, CHUNK, HEAD_DIM))
    dec_end = wide(np.exp(lg[:, None] * (CHUNK - 1 - idx)[None, :]))
    dec_start = wide(np.exp(lg[:, None] * (idx + 1.0)[None, :]))
    chunk_dec = np.broadcast_to(np.exp(lg * CHUNK)[:, None, None], (RET_HEADS, HEAD_DIM, HEAD_DIM))
    gam = np.broadcast_to(np.exp(lg)[:, None], (RET_HEADS, HEAD_DIM))
    return tuple(np.ascontiguousarray(t, np.float32) for t in (mask, dec_start, dec_end, chunk_dec, gam))


def kernel(x_prompt, x_sample, state_ret, state_conv, meta_tokens, norm_mix_pre, norm_mix_post,
           norm_ffn_pre, norm_ffn_post, w_in, conv_w, conv_b, conv_ln_g, conv_ln_b, w_out,
           w_ffn_in, w_ffn_out):
    batch, seq, _ = x_prompt.shape
    n_dec, dec_seq, _ = x_sample.shape
    depth = w_in.shape[0]
    assert depth == 1 and dec_seq == 1 and seq % PROMPT_TILE == 0
    assert n_dec % SAMPLE_BLOCK == 0 and (batch * seq) % FFN_TILE == 0

    row = lambda v: v.reshape(1, -1)
    nw_pre, nw_post = row(norm_mix_pre[0]), row(norm_mix_post[0])
    nf_pre, nf_post = row(norm_ffn_pre[0]), row(norm_ffn_post[0])
    w_in_b = w_in[0].astype(BF16)
    cw, cb, lg, lb = conv_w[0], row(conv_b[0]), row(conv_ln_g[0]), row(conv_ln_b[0])

    mask, dec_start, dec_end, chunk_dec, gam = _decay_tables()
    pos_small = np.concatenate([
        np.full((n_dec,), PAST_LEN), np.zeros((CHUNK - N_META,)), np.arange(N_META)])
    cos_s, sin_s = _rotary_tables(pos_small)
    cos_p, sin_p = _rotary_tables(N_META + np.arange(seq))

    xs = jnp.concatenate([
        x_sample.reshape(n_dec, D_MODEL),
        jnp.zeros((CHUNK - N_META, D_MODEL), x_sample.dtype),
        meta_tokens.astype(x_prompt.dtype)], axis=0)
    v_s, g_s, u_s, qt, kt, r0, tail0 = _small_inproj(
        xs, nw_pre, w_in_b, cos_s, sin_s, dec_end, n_dec)

    x2d = x_prompt.reshape(batch * seq, D_MODEL)
    mix_p, ret_p, conv_p = _prompt_mix(
        x2d, nw_pre, w_in_b, cos_p, sin_p, mask, dec_start, dec_end, chunk_dec, r0, tail0,
        cw, cb, lg, lb, batch, seq)

    blocks = n_dec // SAMPLE_BLOCK
    to_blocks = lambda t: t.reshape(RET_W, blocks, SAMPLE_BLOCK).transpose(1, 0, 2)
    mix_s, ret_s, conv_s = _sample_mix(
        to_blocks(qt), to_blocks(kt), v_s, g_s, u_s, gam, state_ret[0],
        state_conv[0].transpose(1, 0, 2), cw, cb, lg, lb)
    conv_s = conv_s.transpose(1, 0, 2)

    y_s, w_out_b, w_fi_b, w_fo_b = _decode_ffn(
        mix_s, x_sample.reshape(n_dec, D_MODEL), w_out[0], w_ffn_in[0], w_ffn_out[0],
        nw_post, nf_pre, nf_post)
    y_p = _out_ffn(mix_p, x2d, w_out_b, w_fi_b, w_fo_b, nw_post, nf_pre, nf_post, FFN_TILE)

    return (y_p.reshape(batch, seq, D_MODEL), y_s.reshape(n_dec, 1, D_MODEL),
            ret_p[None], conv_p[None], ret_s[None], conv_s[None])
```

```python
import jax
import jax.numpy as jnp
import numpy as np
from jax import lax
from jax.experimental import pallas as pl
from jax.experimental.pallas import tpu as pltpu

F32 = jnp.float32
BF16 = jnp.bfloat16

D_MODEL = 1024
N_META = 16
PAST_LEN = 16384
RET_HEADS = 4
HEAD_DIM = 128
RET_W = RET_HEADS * HEAD_DIM
CONV_CH = D_MODEL - RET_W
CONV_WIDTH = 31
CONV_TAIL = CONV_WIDTH - 1
CHUNK = 128
ROPE_BASE = 10000.0
D_FF = 2816
IN_COLS = 4 * RET_W + 2 * CONV_CH
EPS = 1e-6
K_SCALE = HEAD_DIM ** -0.5

SUBLANES = 8
TAIL_ROWS = 32
TAIL_SKIP = TAIL_ROWS - CONV_TAIL

PROMPT_TILE = 512
CONV_ROW_STRIDE = 4
SAMPLE_BLOCK = 16
FFN_TILE = 1024
FFN_GROUP_ROWS = 256
FFN_CHUNK = 1024
WEIGHT_CHUNKS = 8
MIB = 1024 * 1024


def _hbm(*xs):
    return tuple(pltpu.with_memory_space_constraint(jnp.asarray(x), pltpu.HBM) for x in xs)


def _rms(x, w):
    return x * lax.rsqrt(jnp.mean(x * x, axis=-1, keepdims=True) + EPS) * w


def _rot(xh, cos2, sin2):
    return xh * cos2 + pltpu.roll(xh, HEAD_DIM // 2, 1) * sin2


def _head(i):
    return slice(i * HEAD_DIM, (i + 1) * HEAD_DIM)


def _dot(a, b):
    return jnp.dot(a, b, preferred_element_type=F32)


def _dot_nt(a, b):
    return lax.dot_general(a, b, (((1,), (1,)), ((), ())), preferred_element_type=F32)


def _dot_tn(a, b):
    return lax.dot_general(a, b, (((0,), (0,)), ((), ())), preferred_element_type=F32)


def _layer_norm_swish(y, g, b):
    mu = jnp.mean(y, axis=-1, keepdims=True)
    var = jnp.mean(jnp.square(y - mu), axis=-1, keepdims=True)
    yn = (y - mu) * lax.rsqrt(var + EPS) * g + b
    return jax.nn.silu(yn)


def _small_inproj_kernel(x_ref, nw_ref, w_ref, cos_ref, sin_ref, de_ref,
                         v_ref, g_ref, u_ref, qt_ref, kt_ref, r0_ref, tail0_ref,
                         p_scr):
    n = v_ref.shape[0]
    xn = _rms(x_ref[...], nw_ref[...]).astype(BF16)
    p_scr[...] = _dot(xn, w_ref[...])
    dec = slice(0, n)
    met = slice(n, n + CHUNK)
    for h in range(RET_HEADS):
        q = _rot(p_scr[dec, _head(h)], cos_ref[dec, :], sin_ref[dec, :])
        k = _rot(p_scr[dec, _head(RET_HEADS + h)], cos_ref[dec, :], sin_ref[dec, :]) * K_SCALE
        qt_ref[_head(h), :] = q.T
        kt_ref[_head(h), :] = k.T
        km = _rot(p_scr[met, _head(RET_HEADS + h)], cos_ref[met, :], sin_ref[met, :]) * K_SCALE
        vm = p_scr[met, _head(2 * RET_HEADS + h)]
        r0_ref[h] = _dot_tn((km * de_ref[h]).astype(BF16), vm.astype(BF16))
    v_ref[...] = p_scr[dec, 2 * RET_W:3 * RET_W]
    g_ref[...] = p_scr[dec, 3 * RET_W:4 * RET_W]
    a = p_scr[:, 4 * RET_W:4 * RET_W + CONV_CH]
    b = p_scr[:, 4 * RET_W + CONV_CH:]
    u = a * jax.nn.sigmoid(b)
    u_ref[...] = u[dec, :]
    tail0_ref[...] = u[n + CHUNK - TAIL_ROWS:n + CHUNK, :]


def _small_inproj(xs, nw, w_in, cos_s, sin_s, dec_end, n_dec):
    rows = xs.shape[0]
    full = lambda shape: pl.BlockSpec(shape, lambda i: (0,) * len(shape))
    out_shapes = (
        jax.ShapeDtypeStruct((n_dec, RET_W), F32),
        jax.ShapeDtypeStruct((n_dec, RET_W), F32),
        jax.ShapeDtypeStruct((n_dec, CONV_CH), F32),
        jax.ShapeDtypeStruct((RET_W, n_dec), F32),
        jax.ShapeDtypeStruct((RET_W, n_dec), F32),
        jax.ShapeDtypeStruct((RET_HEADS, HEAD_DIM, HEAD_DIM), F32),
        jax.ShapeDtypeStruct((TAIL_ROWS, CONV_CH), F32),
    )
    return pl.pallas_call(
        _small_inproj_kernel,
        grid=(1,),
        in_specs=[full(xs.shape), full(nw.shape), full(w_in.shape), full(cos_s.shape),
                  full(sin_s.shape), full(dec_end.shape)],
        out_specs=tuple(full(s.shape) for s in out_shapes),
        out_shape=out_shapes,
        scratch_shapes=[pltpu.VMEM((rows, IN_COLS), F32)],
        compiler_params=pltpu.CompilerParams(
            dimension_semantics=("arbitrary",), vmem_limit_bytes=40 * MIB),
        name="small_inproj",
    )(*_hbm(xs, nw, w_in, cos_s, sin_s, dec_end))


def _prompt_mix_kernel(x_ref, nw_ref, w_ref, cos_ref, sin_ref, mask_ref, ds_ref, de_ref, cd_ref,
                       r0_ref, tail0_ref, cw_ref, cb_ref, lg_ref, lb_ref,
                       mix_ref, retp_ref, convp_ref,
                       p_scr, r_scr, u_scr, y_scr):
    tile = x_ref.shape[0]
    j = pl.program_id(1)

    n_slabs = CONV_CH // HEAD_DIM

    @pl.when(j == 0)
    def _():
        r_scr[...] = r0_ref[...]
        for c in range(n_slabs):
            u_scr[c, 0:TAIL_ROWS, :] = tail0_ref[:, _head(c)]

    xn = _rms(x_ref[...], nw_ref[...]).astype(BF16)

    ab = _dot(xn, w_ref[:, 4 * RET_W:])
    u = ab[:, :CONV_CH] * jax.nn.sigmoid(ab[:, CONV_CH:])
    for c in range(n_slabs):
        u_scr[c, TAIL_ROWS:TAIL_ROWS + tile, :] = u[:, _head(c)]
    p_scr[...] = _dot(xn, w_ref[:, :4 * RET_W])
    block = CONV_ROW_STRIDE * SUBLANES
    for c in range(n_slabs):
        taps = [cw_ref[t:t + 1, _head(c)] for t in range(CONV_WIDTH)]
        bias = cb_ref[:, _head(c)]
        for b0 in range(0, tile, block):
            acc = [None] * CONV_ROW_STRIDE
            for p in range(CONV_ROW_STRIDE + CONV_WIDTH - 1):
                win = u_scr[c, pl.ds(TAIL_SKIP + b0 + p, SUBLANES, stride=CONV_ROW_STRIDE), :]
                for r in range(CONV_ROW_STRIDE):
                    t = p - r
                    if 0 <= t < CONV_WIDTH:
                        term = taps[t] * win
                        acc[r] = term if acc[r] is None else acc[r] + term
            for r in range(CONV_ROW_STRIDE):
                y_scr[c, pl.ds(b0 + r, SUBLANES, stride=CONV_ROW_STRIDE), :] = acc[r] + bias
    ys = [y_scr[c] for c in range(n_slabs)]
    mu = sum(jnp.sum(y, axis=-1, keepdims=True) for y in ys) * (1.0 / CONV_CH)
    dev = [y - mu for y in ys]
    var = sum(jnp.sum(d * d, axis=-1, keepdims=True) for d in dev) * (1.0 / CONV_CH)
    inv = lax.rsqrt(var + EPS)
    for c in range(n_slabs):
        yn = dev[c] * inv * lg_ref[:, _head(c)] + lb_ref[:, _head(c)]
        mix_ref[:, RET_W + c * HEAD_DIM:RET_W + (c + 1) * HEAD_DIM] = jax.nn.silu(yn).astype(BF16)
        u_scr[c, 0:TAIL_ROWS, :] = u_scr[c, tile:tile + TAIL_ROWS, :]

    for c in range(tile // CHUNK):
        rows = slice(c * CHUNK, (c + 1) * CHUNK)
        cos2 = cos_ref[rows, :]
        sin2 = sin_ref[rows, :]
        for h in range(RET_HEADS):
            q = _rot(p_scr[rows, _head(h)], cos2, sin2)
            k = _rot(p_scr[rows, _head(RET_HEADS + h)], cos2, sin2) * K_SCALE
            vb = p_scr[rows, _head(2 * RET_HEADS + h)].astype(BF16)
            gate = p_scr[rows, _head(3 * RET_HEADS + h)]
            qb = q.astype(BF16)
            scores = _dot_nt(qb, k.astype(BF16)) * mask_ref[h]
            r_prev = r_scr[h]
            o = _dot(scores.astype(BF16), vb) + _dot(qb, r_prev.astype(BF16)) * ds_ref[h]
            kv = _dot_tn((k * de_ref[h]).astype(BF16), vb)
            r_scr[h] = cd_ref[h] * r_prev + kv
            o = o * lax.rsqrt(jnp.mean(o * o, axis=-1, keepdims=True) + EPS)
            mix_ref[rows, _head(h)] = (o * jax.nn.silu(gate)).astype(BF16)

    @pl.when(j == pl.num_programs(1) - 1)
    def _():
        retp_ref[...] = r_scr[...]
        for c in range(n_slabs):
            convp_ref[:, _head(c)] = u_scr[c, TAIL_SKIP:TAIL_ROWS, :]


def _prompt_mix(x2d, nw, w_in, cos_p, sin_p, mask, dec_start, dec_end, chunk_dec, r0, tail0,
                conv_w, conv_b, ln_g, ln_b, batch, seq):
    tile = PROMPT_TILE
    nt = seq // tile
    const = lambda shape: pl.BlockSpec(shape, lambda b, j: (0,) * len(shape))
    out_shapes = (
        jax.ShapeDtypeStruct((batch * seq, D_MODEL), BF16),
        jax.ShapeDtypeStruct((batch, RET_HEADS, HEAD_DIM, HEAD_DIM), F32),
        jax.ShapeDtypeStruct((batch, CONV_TAIL, CONV_CH), F32),
    )
    return pl.pallas_call(
        _prompt_mix_kernel,
        grid=(batch, nt),
        in_specs=[
            pl.BlockSpec((tile, D_MODEL), lambda b, j: (b * nt + j, 0)),
            const(nw.shape), const(w_in.shape),
            pl.BlockSpec((tile, HEAD_DIM), lambda b, j: (j, 0)),
            pl.BlockSpec((tile, HEAD_DIM), lambda b, j: (j, 0)),
            const(mask.shape), const(dec_start.shape), const(dec_end.shape), const(chunk_dec.shape),
            const(r0.shape), const(tail0.shape), const(conv_w.shape), const(conv_b.shape),
            const(ln_g.shape), const(ln_b.shape),
        ],
        out_specs=(
            pl.BlockSpec((tile, D_MODEL), lambda b, j: (b * nt + j, 0)),
            pl.BlockSpec((None, RET_HEADS, HEAD_DIM, HEAD_DIM), lambda b, j: (b, 0, 0, 0)),
            pl.BlockSpec((None, CONV_TAIL, CONV_CH), lambda b, j: (b, 0, 0)),
        ),
        out_shape=out_shapes,
        scratch_shapes=[
            pltpu.VMEM((tile, 4 * RET_W), F32),
            pltpu.VMEM((RET_HEADS, HEAD_DIM, HEAD_DIM), F32),
            pltpu.VMEM((CONV_CH // HEAD_DIM, TAIL_ROWS + tile, HEAD_DIM), F32),
            pltpu.VMEM((CONV_CH // HEAD_DIM, tile, HEAD_DIM), F32),
        ],
        compiler_params=pltpu.CompilerParams(
            dimension_semantics=("arbitrary", "arbitrary"), vmem_limit_bytes=48 * MIB),
        name="prompt_mix",
    )(*_hbm(x2d, nw, w_in, cos_p, sin_p, mask, dec_start, dec_end, chunk_dec, r0, tail0,
            conv_w, conv_b, ln_g, ln_b))


def _sample_mix_kernel(qt_ref, kt_ref, v_ref, g_ref, u_ref, gam_ref, sr_ref, sc_ref,
                       cw_ref, cb_ref, lg_ref, lb_ref,
                       mix_ref, so_ref, sco_ref, o_scr):
    nb = v_ref.shape[0]
    for t in range(nb):
        for h in range(RET_HEADS):
            kc = kt_ref[_head(h), t:t + 1]
            qc = qt_ref[_head(h), t:t + 1]
            vr = v_ref[t:t + 1, _head(h)]
            s_new = gam_ref[h:h + 1, :] * sr_ref[t, h] + kc * vr
            so_ref[t, h] = s_new
            o_scr[t:t + 1, _head(h)] = jnp.sum(qc * s_new, axis=0, keepdims=True)
    for h in range(RET_HEADS):
        o = o_scr[:, _head(h)]
        o = o * lax.rsqrt(jnp.mean(o * o, axis=-1, keepdims=True) + EPS)
        mix_ref[:, _head(h)] = o * jax.nn.silu(g_ref[:, _head(h)])
    u = u_ref[...]
    y = u * cw_ref[CONV_TAIL:CONV_WIDTH, :] + cb_ref[...]
    for t in range(CONV_TAIL):
        y = y + sc_ref[t] * cw_ref[t:t + 1, :]
    mix_ref[:, RET_W:] = _layer_norm_swish(y, lg_ref[...], lb_ref[...])
    for t in range(CONV_TAIL - 1):
        sco_ref[t] = sc_ref[t + 1]
    sco_ref[CONV_TAIL - 1] = u


def _sample_mix(qt3, kt3, v, g, u, gam, state_ret, state_conv, conv_w, conv_b, ln_g, ln_b):
    n = v.shape[0]
    nb = SAMPLE_BLOCK
    const = lambda shape: pl.BlockSpec(shape, lambda i: (0,) * len(shape))
    rows = lambda width: pl.BlockSpec((nb, width), lambda i: (i, 0))
    out_shapes = (
        jax.ShapeDtypeStruct((n, D_MODEL), F32),
        jax.ShapeDtypeStruct(state_ret.shape, F32),
        jax.ShapeDtypeStruct(state_conv.shape, F32),
    )
    st_spec = pl.BlockSpec((nb, RET_HEADS, HEAD_DIM, HEAD_DIM), lambda i: (i, 0, 0, 0))
    cv_spec = pl.BlockSpec((CONV_TAIL, nb, CONV_CH), lambda i: (0, i, 0))
    return pl.pallas_call(
        _sample_mix_kernel,
        grid=(n // nb,),
        in_specs=[
            pl.BlockSpec((None, RET_W, nb), lambda i: (i, 0, 0)),
            pl.BlockSpec((None, RET_W, nb), lambda i: (i, 0, 0)),
            rows(RET_W), rows(RET_W), rows(CONV_CH), const(gam.shape), st_spec, cv_spec,
            const(conv_w.shape), const(conv_b.shape), const(ln_g.shape), const(ln_b.shape),
        ],
        out_specs=(rows(D_MODEL), st_spec, cv_spec),
        out_shape=out_shapes,
        scratch_shapes=[pltpu.VMEM((nb, RET_W), F32)],
        compiler_params=pltpu.CompilerParams(
            dimension_semantics=("arbitrary",), vmem_limit_bytes=32 * MIB),
        name="sample_mix",
    )(*_hbm(qt3, kt3, v, g, u, gam, state_ret, state_conv, conv_w, conv_b, ln_g, ln_b))


def _out_ffn_kernel(mix_ref, x_ref, wo_ref, wi_ref, wd_ref, npost_ref, nfpre_ref, nfpost_ref, y_ref):
    tile = x_ref.shape[0]
    n_groups = max(1, tile // FFN_GROUP_ROWS)
    groups = [slice(i * tile // n_groups, (i + 1) * tile // n_groups) for i in range(n_groups)]
    a = [_dot(mix_ref[r, :].astype(BF16), wo_ref[...]) for r in groups]
    h1 = [x_ref[r, :] + _rms(a[i], npost_ref[...]) for i, r in enumerate(groups)]
    xf = [_rms(h, nfpre_ref[...]).astype(BF16) for h in h1]
    dn = [None] * n_groups
    for c0 in range(0, D_FF, FFN_CHUNK):
        c1 = min(c0 + FFN_CHUNK, D_FF)
        for i in range(n_groups):
            gate = _dot(xf[i], wi_ref[:, c0:c1])
            up = _dot(xf[i], wi_ref[:, D_FF + c0:D_FF + c1])
            part = _dot((jax.nn.silu(gate) * up).astype(BF16), wd_ref[c0:c1, :])
            dn[i] = part if dn[i] is None else dn[i] + part
    for i, r in enumerate(groups):
        y_ref[r, :] = h1[i] + _rms(dn[i], nfpost_ref[...])


def _out_ffn(mix, x2d, w_out, w_ffn_in, w_ffn_out, n_post, n_fpre, n_fpost, tile):
    n = x2d.shape[0]
    const = lambda shape: pl.BlockSpec(shape, lambda i: (0,) * len(shape),
                                       pipeline_mode=pl.Buffered(1))
    rows = pl.BlockSpec((tile, D_MODEL), lambda i: (i, 0))
    return pl.pallas_call(
        _out_ffn_kernel,
        grid=(n // tile,),
        in_specs=[rows, rows, const(w_out.shape), const(w_ffn_in.shape), const(w_ffn_out.shape),
                  const(n_post.shape), const(n_fpre.shape), const(n_fpost.shape)],
        out_specs=rows,
        out_shape=jax.ShapeDtypeStruct((n, D_MODEL), F32),
        compiler_params=pltpu.CompilerParams(
            dimension_semantics=("arbitrary",), vmem_limit_bytes=60 * MIB),
        name="out_ffn",
    )(*_hbm(mix, x2d, w_out, w_ffn_in, w_ffn_out, n_post, n_fpre, n_fpost))


def _decode_ffn_kernel(mix_ref, x_ref, woc_ref, wic_ref, wdc_ref, npost_ref, nfpre_ref, nfpost_ref,
                       y_ref, wob_ref, wib_ref, wdb_ref, wo_scr, wi_scr, wd_scr):
    i = pl.program_id(0)
    n_chunks = pl.num_programs(0) - 1

    @pl.when(i < n_chunks)
    def _():
        for src, dst, out in ((woc_ref, wo_scr, wob_ref), (wic_ref, wi_scr, wib_ref),
                              (wdc_ref, wd_scr, wdb_ref)):
            rows = src.shape[0]
            chunk = src[...].astype(BF16)
            dst[pl.ds(pl.multiple_of(i * rows, rows), rows), :] = chunk
            out[...] = chunk

    @pl.when(i == n_chunks)
    def _():
        _out_ffn_kernel(mix_ref, x_ref, wo_scr, wi_scr, wd_scr, npost_ref, nfpre_ref, nfpost_ref, y_ref)


def _decode_ffn(mix, x2d, w_out, w_ffn_in, w_ffn_out, n_post, n_fpre, n_fpost):
    n = x2d.shape[0]
    nc = WEIGHT_CHUNKS
    last = nc - 1
    const = lambda shape: pl.BlockSpec(shape, lambda i: (0,) * len(shape))
    chunk = lambda w: pl.BlockSpec((w.shape[0] // nc, w.shape[1]), lambda i: (jnp.minimum(i, last), 0))
    weights = (w_out, w_ffn_in, w_ffn_out)
    assert all(w.shape[0] % (nc * 16) == 0 for w in weights)
    return pl.pallas_call(
        _decode_ffn_kernel,
        grid=(nc + 1,),
        in_specs=[const(mix.shape), const(x2d.shape)] + [chunk(w) for w in weights]
                 + [const(n_post.shape), const(n_fpre.shape), const(n_fpost.shape)],
        out_specs=(const((n, D_MODEL)),) + tuple(chunk(w) for w in weights),
        out_shape=(jax.ShapeDtypeStruct((n, D_MODEL), F32),)
                  + tuple(jax.ShapeDtypeStruct(w.shape, BF16) for w in weights),
        scratch_shapes=[pltpu.VMEM(w.shape, BF16) for w in weights],
        compiler_params=pltpu.CompilerParams(
            dimension_semantics=("arbitrary",), vmem_limit_bytes=52 * MIB),
        name="decode_ffn",
    )(*_hbm(mix, x2d, w_out, w_ffn_in, w_ffn_out, n_post, n_fpre, n_fpost))


def _rotary_tables(pos):
    half = HEAD_DIM // 2
    inv = 1.0 / (ROPE_BASE ** np.linspace(0.0, 1.0, half))
    ang = np.asarray(pos, np.float64)[:, None] * inv[None, :]
    cos, sin = np.cos(ang), np.sin(ang)
    return (np.concatenate([cos, cos], axis=-1).astype(np.float32),
            np.concatenate([-sin, sin], axis=-1).astype(np.float32))


def _decay_tables():
    lg = np.log(1.0 - 2.0 ** (-5.0 - np.arange(RET_HEADS, dtype=np.float64)))
    idx = np.arange(CHUNK, dtype=np.float64)
    diff = idx[:, None] - idx[None, :]
    mask = np.where(diff[None] >= 0, np.exp(lg[:, None, None] * np.maximum(diff, 0.0)[None]), 0.0)
    wide = lambda col: np.broadcast_to(col[:, :, None], (RET_HEADS, CHUNK, HEAD_DIM))
    dec_end = wide(np.exp(lg[:, None] * (CHUNK - 1 - idx)[None, :]))
    dec_start = wide(np.exp(lg[:, None] * (idx + 1.0)[None, :]))
    chunk_dec = np.broadcast_to(np.exp(lg * CHUNK)[:, None, None], (RET_HEADS, HEAD_DIM, HEAD_DIM))
    gam = np.broadcast_to(np.exp(lg)[:, None], (RET_HEADS, HEAD_DIM))
    return tuple(np.ascontiguousarray(t, np.float32) for t in (mask, dec_start, dec_end, chunk_dec, gam))


def kernel(x_prompt, x_sample, state_ret, state_conv, meta_tokens, norm_mix_pre, norm_mix_post,
           norm_ffn_pre, norm_ffn_post, w_in, conv_w, conv_b, conv_ln_g, conv_ln_b, w_out,
           w_ffn_in, w_ffn_out):
    batch, seq, _ = x_prompt.shape
    n_dec, dec_seq, _ = x_sample.shape
    depth = w_in.shape[0]
    assert depth == 1 and dec_seq == 1 and seq % PROMPT_TILE == 0
    assert n_dec % SAMPLE_BLOCK == 0 and (batch * seq) % FFN_TILE == 0

    row = lambda v: v.reshape(1, -1)
    nw_pre, nw_post = row(norm_mix_pre[0]), row(norm_mix_post[0])
    nf_pre, nf_post = row(norm_ffn_pre[0]), row(norm_ffn_post[0])
    w_in_b = w_in[0].astype(BF16)
    cw, cb, lg, lb = conv_w[0], row(conv_b[0]), row(conv_ln_g[0]), row(conv_ln_b[0])

    mask, dec_start, dec_end, chunk_dec, gam = _decay_tables()
    pos_small = np.concatenate([
        np.full((n_dec,), PAST_LEN), np.zeros((CHUNK - N_META,)), np.arange(N_META)])
    cos_s, sin_s = _rotary_tables(pos_small)
    cos_p, sin_p = _rotary_tables(N_META + np.arange(seq))

    xs = jnp.concatenate([
        x_sample.reshape(n_dec, D_MODEL),
        jnp.zeros((CHUNK - N_META, D_MODEL), x_sample.dtype),
        meta_tokens.astype(x_prompt.dtype)], axis=0)
    v_s, g_s, u_s, qt, kt, r0, tail0 = _small_inproj(
        xs, nw_pre, w_in_b, cos_s, sin_s, dec_end, n_dec)

    x2d = x_prompt.reshape(batch * seq, D_MODEL)
    mix_p, ret_p, conv_p = _prompt_mix(
        x2d, nw_pre, w_in_b, cos_p, sin_p, mask, dec_start, dec_end, chunk_dec, r0, tail0,
        cw, cb, lg, lb, batch, seq)

    blocks = n_dec // SAMPLE_BLOCK
    to_blocks = lambda t: t.reshape(RET_W, blocks, SAMPLE_BLOCK).transpose(1, 0, 2)
    mix_s, ret_s, conv_s = _sample_mix(
        to_blocks(qt), to_blocks(kt), v_s, g_s, u_s, gam, state_ret[0],
        state_conv[0].transpose(1, 0, 2), cw, cb, lg, lb)
    conv_s = conv_s.transpose(1, 0, 2)

    y_s, w_out_b, w_fi_b, w_fo_b = _decode_ffn(
        mix_s, x_sample.reshape(n_dec, D_MODEL), w_out[0], w_ffn_in[0], w_ffn_out[0],
        nw_post, nf_pre, nf_post)
    y_p = _out_ffn(mix_p, x2d, w_out_b, w_fi_b, w_fo_b, nw_post, nf_pre, nf_post, FFN_TILE)

    return (y_p.reshape(batch, seq, D_MODEL), y_s.reshape(n_dec, 1, D_MODEL),
            ret_p[None], conv_p[None], ret_s[None], conv_s[None])
```

```python
import jax
import jax.numpy as jnp
import numpy as np
from jax import lax
from jax.experimental import pallas as pl
from jax.experimental.pallas import tpu as pltpu

F32 = jnp.float32
BF16 = jnp.bfloat16

D_MODEL = 1024
N_META = 16
PAST_LEN = 16384
RET_HEADS = 4
HEAD_DIM = 128
RET_W = RET_HEADS * HEAD_DIM
CONV_CH = D_MODEL - RET_W
CONV_WIDTH = 31
CONV_TAIL = CONV_WIDTH - 1
CHUNK = 128
ROPE_BASE = 10000.0
D_FF = 2816
IN_COLS = 4 * RET_W + 2 * CONV_CH
EPS = 1e-6
K_SCALE = HEAD_DIM ** -0.5

SUBLANES = 8
TAIL_ROWS = 32
TAIL_SKIP = TAIL_ROWS - CONV_TAIL

PROMPT_TILE = 1024
CONV_ROW_STRIDE = 4
SAMPLE_BLOCK = 16
FFN_TILE = 1024
FFN_GROUP_ROWS = 256
FFN_CHUNK = 1024
WEIGHT_CHUNKS = 8
INPROJ_WEIGHT_CHUNKS = 4
MIB = 1024 * 1024


def _hbm(*xs):
    return tuple(pltpu.with_memory_space_constraint(jnp.asarray(x), pltpu.HBM) for x in xs)


def _rms(x, w):
    return x * lax.rsqrt(jnp.mean(x * x, axis=-1, keepdims=True) + EPS) * w


def _rot(xh, cos2, sin2):
    return xh * cos2 + pltpu.roll(xh, HEAD_DIM // 2, 1) * sin2


def _head(i):
    return slice(i * HEAD_DIM, (i + 1) * HEAD_DIM)


def _dot(a, b):
    return jnp.dot(a, b, preferred_element_type=F32)


def _dot_nt(a, b):
    return lax.dot_general(a, b, (((1,), (1,)), ((), ())), preferred_element_type=F32)


def _dot_tn(a, b):
    return lax.dot_general(a, b, (((0,), (0,)), ((), ())), preferred_element_type=F32)


def _layer_norm_swish(y, g, b):
    mu = jnp.mean(y, axis=-1, keepdims=True)
    var = jnp.mean(jnp.square(y - mu), axis=-1, keepdims=True)
    yn = (y - mu) * lax.rsqrt(var + EPS) * g + b
    return jax.nn.silu(yn)


def _small_inproj_kernel(x_ref, nw_ref, wc_ref, cos_ref, sin_ref, de_ref,
                         v_ref, g_ref, u_ref, qt_ref, kt_ref, r0_ref, tail0_ref, wb_ref,
                         p_scr, w_scr):
    i = pl.program_id(0)
    n_chunks = pl.num_programs(0) - 1

    @pl.when(i < n_chunks)
    def _():
        rows = wc_ref.shape[0]
        chunk = wc_ref[...].astype(BF16)
        w_scr[pl.ds(pl.multiple_of(i * rows, rows), rows), :] = chunk
        wb_ref[...] = chunk

    @pl.when(i == n_chunks)
    def _():
        n = v_ref.shape[0]
        xn = _rms(x_ref[...], nw_ref[...]).astype(BF16)
        p_scr[...] = _dot(xn, w_scr[...])
        dec = slice(0, n)
        met = slice(n, n + CHUNK)
        for h in range(RET_HEADS):
            q = _rot(p_scr[dec, _head(h)], cos_ref[dec, :], sin_ref[dec, :])
            k = _rot(p_scr[dec, _head(RET_HEADS + h)], cos_ref[dec, :], sin_ref[dec, :]) * K_SCALE
            qt_ref[_head(h), :] = q.T
            kt_ref[_head(h), :] = k.T
            km = _rot(p_scr[met, _head(RET_HEADS + h)], cos_ref[met, :], sin_ref[met, :]) * K_SCALE
            vm = p_scr[met, _head(2 * RET_HEADS + h)]
            r0_ref[h] = _dot_tn((km * de_ref[h]).astype(BF16), vm.astype(BF16))
        v_ref[...] = p_scr[dec, 2 * RET_W:3 * RET_W]
        g_ref[...] = p_scr[dec, 3 * RET_W:4 * RET_W]
        a = p_scr[:, 4 * RET_W:4 * RET_W + CONV_CH]
        b = p_scr[:, 4 * RET_W + CONV_CH:]
        u = a * jax.nn.sigmoid(b)
        u_ref[...] = u[dec, :]
        tail0_ref[...] = u[n + CHUNK - TAIL_ROWS:n + CHUNK, :]


def _small_inproj(xs, nw, w_in, cos_s, sin_s, dec_end, n_dec):
    rows = xs.shape[0]
    nc = INPROJ_WEIGHT_CHUNKS
    last = nc - 1
    assert w_in.shape[0] % (nc * 16) == 0
    full = lambda shape: pl.BlockSpec(shape, lambda i: (0,) * len(shape))
    chunk = pl.BlockSpec((w_in.shape[0] // nc, IN_COLS), lambda i: (jnp.minimum(i, last), 0))
    out_shapes = (
        jax.ShapeDtypeStruct((n_dec, RET_W), F32),
        jax.ShapeDtypeStruct((n_dec, RET_W), F32),
        jax.ShapeDtypeStruct((n_dec, CONV_CH), F32),
        jax.ShapeDtypeStruct((RET_W, n_dec), F32),
        jax.ShapeDtypeStruct((RET_W, n_dec), F32),
        jax.ShapeDtypeStruct((RET_HEADS, HEAD_DIM, HEAD_DIM), F32),
        jax.ShapeDtypeStruct((TAIL_ROWS, CONV_CH), F32),
        jax.ShapeDtypeStruct(w_in.shape, BF16),
    )
    return pl.pallas_call(
        _small_inproj_kernel,
        grid=(nc + 1,),
        in_specs=[full(xs.shape), full(nw.shape), chunk, full(cos_s.shape),
                  full(sin_s.shape), full(dec_end.shape)],
        out_specs=tuple(full(s.shape) for s in out_shapes[:-1]) + (chunk,),
        out_shape=out_shapes,
        scratch_shapes=[pltpu.VMEM((rows, IN_COLS), F32), pltpu.VMEM(w_in.shape, BF16)],
        compiler_params=pltpu.CompilerParams(
            dimension_semantics=("arbitrary",), vmem_limit_bytes=40 * MIB),
        name="small_inproj",
    )(*_hbm(xs, nw, w_in, cos_s, sin_s, dec_end))


def _prompt_mix_kernel(x_ref, nw_ref, w_ref, cos_ref, sin_ref, mask_ref, ds_ref, de_ref, cd_ref,
                       r0_ref, tail0_ref, cw_ref, cb_ref, lg_ref, lb_ref,
                       mix_ref, retp_ref, convp_ref,
                       p_scr, r_scr, u_scr, y_scr):
    tile = x_ref.shape[0]
    j = pl.program_id(1)

    n_slabs = CONV_CH // HEAD_DIM

    @pl.when(j == 0)
    def _():
        r_scr[...] = r0_ref[...]
        for c in range(n_slabs):
            u_scr[c, 0:TAIL_ROWS, :] = tail0_ref[:, _head(c)]

    xn = _rms(x_ref[...], nw_ref[...]).astype(BF16)

    ab = _dot(xn, w_ref[:, 4 * RET_W:])
    u = ab[:, :CONV_CH] * jax.nn.sigmoid(ab[:, CONV_CH:])
    for c in range(n_slabs):
        u_scr[c, TAIL_ROWS:TAIL_ROWS + tile, :] = u[:, _head(c)]
    p_scr[...] = _dot(xn, w_ref[:, :4 * RET_W])
    block = CONV_ROW_STRIDE * SUBLANES
    for c in range(n_slabs):
        taps = [cw_ref[t:t + 1, _head(c)] for t in range(CONV_WIDTH)]
        bias = cb_ref[:, _head(c)]
        for b0 in range(0, tile, block):
            acc = [None] * CONV_ROW_STRIDE
            for p in range(CONV_ROW_STRIDE + CONV_WIDTH - 1):
                win = u_scr[c, pl.ds(TAIL_SKIP + b0 + p, SUBLANES, stride=CONV_ROW_STRIDE), :]
                for r in range(CONV_ROW_STRIDE):
                    t = p - r
                    if 0 <= t < CONV_WIDTH:
                        term = taps[t] * win
                        acc[r] = term if acc[r] is None else acc[r] + term
            for r in range(CONV_ROW_STRIDE):
                y_scr[c, pl.ds(b0 + r, SUBLANES, stride=CONV_ROW_STRIDE), :] = acc[r] + bias
    ys = [y_scr[c] for c in range(n_slabs)]
    mu = sum(jnp.sum(y, axis=-1, keepdims=True) for y in ys) * (1.0 / CONV_CH)
    dev = [y - mu for y in ys]
    var = sum(jnp.sum(d * d, axis=-1, keepdims=True) for d in dev) * (1.0 / CONV_CH)
    inv = lax.rsqrt(var + EPS)
    for c in range(n_slabs):
        yn = dev[c] * inv * lg_ref[:, _head(c)] + lb_ref[:, _head(c)]
        mix_ref[:, RET_W + c * HEAD_DIM:RET_W + (c + 1) * HEAD_DIM] = jax.nn.silu(yn).astype(BF16)
        u_scr[c, 0:TAIL_ROWS, :] = u_scr[c, tile:tile + TAIL_ROWS, :]

    for c in range(tile // CHUNK):
        rows = slice(c * CHUNK, (c + 1) * CHUNK)
        cos2 = cos_ref[rows, :]
        sin2 = sin_ref[rows, :]
        for h in range(RET_HEADS):
            q = _rot(p_scr[rows, _head(h)], cos2, sin2)
            k = _rot(p_scr[rows, _head(RET_HEADS + h)], cos2, sin2) * K_SCALE
            vb = p_scr[rows, _head(2 * RET_HEADS + h)].astype(BF16)
            gate = p_scr[rows, _head(3 * RET_HEADS + h)]
            qb = q.astype(BF16)
            scores = _dot_nt(qb, k.astype(BF16)) * mask_ref[h]
            r_prev = r_scr[h]
            o = _dot(scores.astype(BF16), vb) + _dot(qb, r_prev.astype(BF16)) * ds_ref[h]
            kv = _dot_tn((k * de_ref[h]).astype(BF16), vb)
            r_scr[h] = cd_ref[h] * r_prev + kv
            o = o * lax.rsqrt(jnp.mean(o * o, axis=-1, keepdims=True) + EPS)
            mix_ref[rows, _head(h)] = (o * jax.nn.silu(gate)).astype(BF16)

    @pl.when(j == pl.num_programs(1) - 1)
    def _():
        retp_ref[...] = r_scr[...]
        for c in range(n_slabs):
            convp_ref[:, _head(c)] = u_scr[c, TAIL_SKIP:TAIL_ROWS, :]


def _prompt_mix(x2d, nw, w_in, cos_p, sin_p, mask, dec_start, dec_end, chunk_dec, r0, tail0,
                conv_w, conv_b, ln_g, ln_b, batch, seq):
    tile = PROMPT_TILE
    nt = seq // tile
    const = lambda shape: pl.BlockSpec(shape, lambda b, j: (0,) * len(shape))
    out_shapes = (
        jax.ShapeDtypeStruct((batch * seq, D_MODEL), BF16),
        jax.ShapeDtypeStruct((batch, RET_HEADS, HEAD_DIM, HEAD_DIM), F32),
        jax.ShapeDtypeStruct((batch, CONV_TAIL, CONV_CH), F32),
    )
    return pl.pallas_call(
        _prompt_mix_kernel,
        grid=(batch, nt),
        in_specs=[
            pl.BlockSpec((tile, D_MODEL), lambda b, j: (b * nt + j, 0)),
            const(nw.shape), const(w_in.shape),
            pl.BlockSpec((tile, HEAD_DIM), lambda b, j: (j, 0)),
            pl.BlockSpec((tile, HEAD_DIM), lambda b, j: (j, 0)),
            const(mask.shape), const(dec_start.shape), const(dec_end.shape), const(chunk_dec.shape),
            const(r0.shape), const(tail0.shape), const(conv_w.shape), const(conv_b.shape),
            const(ln_g.shape), const(ln_b.shape),
        ],
        out_specs=(
            pl.BlockSpec((tile, D_MODEL), lambda b, j: (b * nt + j, 0)),
            pl.BlockSpec((None, RET_HEADS, HEAD_DIM, HEAD_DIM), lambda b, j: (b, 0, 0, 0)),
            pl.BlockSpec((None, CONV_TAIL, CONV_CH), lambda b, j: (b, 0, 0)),
        ),
        out_shape=out_shapes,
        scratch_shapes=[
            pltpu.VMEM((tile, 4 * RET_W), F32),
            pltpu.VMEM((RET_HEADS, HEAD_DIM, HEAD_DIM), F32),
            pltpu.VMEM((CONV_CH // HEAD_DIM, TAIL_ROWS + tile, HEAD_DIM), F32),
            pltpu.VMEM((CONV_CH // HEAD_DIM, tile, HEAD_DIM), F32),
        ],
        compiler_params=pltpu.CompilerParams(
            dimension_semantics=("arbitrary", "arbitrary"), vmem_limit_bytes=56 * MIB),
        name="prompt_mix",
    )(*_hbm(x2d, nw, w_in, cos_p, sin_p, mask, dec_start, dec_end, chunk_dec, r0, tail0,
            conv_w, conv_b, ln_g, ln_b))


def _sample_mix_kernel(qt_ref, kt_ref, v_ref, g_ref, u_ref, gam_ref, sr_ref, sc_ref,
                       cw_ref, cb_ref, lg_ref, lb_ref,
                       mix_ref, so_ref, sco_ref, o_scr):
    nb = v_ref.shape[0]
    for t in range(nb):
        for h in range(RET_HEADS):
            kc = kt_ref[_head(h), t:t + 1]
            qc = qt_ref[_head(h), t:t + 1]
            vr = v_ref[t:t + 1, _head(h)]
            s_new = gam_ref[h:h + 1, :] * sr_ref[t, h] + kc * vr
            so_ref[t, h] = s_new
            o_scr[t:t + 1, _head(h)] = jnp.sum(qc * s_new, axis=0, keepdims=True)
    for h in range(RET_HEADS):
        o = o_scr[:, _head(h)]
        o = o * lax.rsqrt(jnp.mean(o * o, axis=-1, keepdims=True) + EPS)
        mix_ref[:, _head(h)] = o * jax.nn.silu(g_ref[:, _head(h)])
    u = u_ref[...]
    y = u * cw_ref[CONV_TAIL:CONV_WIDTH, :] + cb_ref[...]
    for t in range(CONV_TAIL):
        y = y + sc_ref[t] * cw_ref[t:t + 1, :]
    mix_ref[:, RET_W:] = _layer_norm_swish(y, lg_ref[...], lb_ref[...])
    for t in range(CONV_TAIL - 1):
        sco_ref[t] = sc_ref[t + 1]
    sco_ref[CONV_TAIL - 1] = u


def _sample_mix(qt3, kt3, v, g, u, gam, state_ret, state_conv, conv_w, conv_b, ln_g, ln_b):
    n = v.shape[0]
    nb = SAMPLE_BLOCK
    const = lambda shape: pl.BlockSpec(shape, lambda i: (0,) * len(shape))
    rows = lambda width: pl.BlockSpec((nb, width), lambda i: (i, 0))
    out_shapes = (
        jax.ShapeDtypeStruct((n, D_MODEL), F32),
        jax.ShapeDtypeStruct(state_ret.shape, F32),
        jax.ShapeDtypeStruct(state_conv.shape, F32),
    )
    st_spec = pl.BlockSpec((nb, RET_HEADS, HEAD_DIM, HEAD_DIM), lambda i: (i, 0, 0, 0))
    cv_spec = pl.BlockSpec((CONV_TAIL, nb, CONV_CH), lambda i: (0, i, 0))
    return pl.pallas_call(
        _sample_mix_kernel,
        grid=(n // nb,),
        in_specs=[
            pl.BlockSpec((None, RET_W, nb), lambda i: (i, 0, 0)),
            pl.BlockSpec((None, RET_W, nb), lambda i: (i, 0, 0)),
            rows(RET_W), rows(RET_W), rows(CONV_CH), const(gam.shape), st_spec, cv_spec,
            const(conv_w.shape), const(conv_b.shape), const(ln_g.shape), const(ln_b.shape),
        ],
        out_specs=(rows(D_MODEL), st_spec, cv_spec),
        out_shape=out_shapes,
        scratch_shapes=[pltpu.VMEM((nb, RET_W), F32)],
        compiler_params=pltpu.CompilerParams(
            dimension_semantics=("arbitrary",), vmem_limit_bytes=32 * MIB),
        name="sample_mix",
    )(*_hbm(qt3, kt3, v, g, u, gam, state_ret, state_conv, conv_w, conv_b, ln_g, ln_b))


def _out_ffn_kernel(mix_ref, x_ref, wo_ref, wi_ref, wd_ref, npost_ref, nfpre_ref, nfpost_ref, y_ref):
    tile = x_ref.shape[0]
    n_groups = max(1, tile // FFN_GROUP_ROWS)
    groups = [slice(i * tile // n_groups, (i + 1) * tile // n_groups) for i in range(n_groups)]
    a = [_dot(mix_ref[r, :].astype(BF16), wo_ref[...]) for r in groups]
    h1 = [x_ref[r, :] + _rms(a[i], npost_ref[...]) for i, r in enumerate(groups)]
    xf = [_rms(h, nfpre_ref[...]).astype(BF16) for h in h1]
    dn = [None] * n_groups
    for c0 in range(0, D_FF, FFN_CHUNK):
        c1 = min(c0 + FFN_CHUNK, D_FF)
        for i in range(n_groups):
            gate = _dot(xf[i], wi_ref[:, c0:c1])
            up = _dot(xf[i], wi_ref[:, D_FF + c0:D_FF + c1])
            part = _dot((jax.nn.silu(gate) * up).astype(BF16), wd_ref[c0:c1, :])
            dn[i] = part if dn[i] is None else dn[i] + part
    for i, r in enumerate(groups):
        y_ref[r, :] = h1[i] + _rms(dn[i], nfpost_ref[...])


def _out_ffn(mix, x2d, w_out, w_ffn_in, w_ffn_out, n_post, n_fpre, n_fpost, tile):
    n = x2d.shape[0]
    const = lambda shape: pl.BlockSpec(shape, lambda i: (0,) * len(shape),
                                       pipeline_mode=pl.Buffered(1))
    rows = pl.BlockSpec((tile, D_MODEL), lambda i: (i, 0))
    return pl.pallas_call(
        _out_ffn_kernel,
        grid=(n // tile,),
        in_specs=[rows, rows, const(w_out.shape), const(w_ffn_in.shape), const(w_ffn_out.shape),
                  const(n_post.shape), const(n_fpre.shape), const(n_fpost.shape)],
        out_specs=rows,
        out_shape=jax.ShapeDtypeStruct((n, D_MODEL), F32),
        compiler_params=pltpu.CompilerParams(
            dimension_semantics=("arbitrary",), vmem_limit_bytes=60 * MIB),
        name="out_ffn",
    )(*_hbm(mix, x2d, w_out, w_ffn_in, w_ffn_out, n_post, n_fpre, n_fpost))


def _decode_ffn_kernel(mix_ref, x_ref, woc_ref, wic_ref, wdc_ref, npost_ref, nfpre_ref, nfpost_ref,
                       y_ref, wob_ref, wib_ref, wdb_ref, wo_scr, wi_scr, wd_scr):
    i = pl.program_id(0)
    n_chunks = pl.num_programs(0) - 1

    @pl.when(i < n_chunks)
    def _():
        for src, dst, out in ((woc_ref, wo_scr, wob_ref), (wic_ref, wi_scr, wib_ref),
                              (wdc_ref, wd_scr, wdb_ref)):
            rows = src.shape[0]
            chunk = src[...].astype(BF16)
            dst[pl.ds(pl.multiple_of(i * rows, rows), rows), :] = chunk
            out[...] = chunk

    @pl.when(i == n_chunks)
    def _():
        _out_ffn_kernel(mix_ref, x_ref, wo_scr, wi_scr, wd_scr, npost_ref, nfpre_ref, nfpost_ref, y_ref)


def _decode_ffn(mix, x2d, w_out, w_ffn_in, w_ffn_out, n_post, n_fpre, n_fpost):
    n = x2d.shape[0]
    nc = WEIGHT_CHUNKS
    last = nc - 1
    const = lambda shape: pl.BlockSpec(shape, lambda i: (0,) * len(shape))
    chunk = lambda w: pl.BlockSpec((w.shape[0] // nc, w.shape[1]), lambda i: (jnp.minimum(i, last), 0))
    weights = (w_out, w_ffn_in, w_ffn_out)
    assert all(w.shape[0] % (nc * 16) == 0 for w in weights)
    return pl.pallas_call(
        _decode_ffn_kernel,
        grid=(nc + 1,),
        in_specs=[const(mix.shape), const(x2d.shape)] + [chunk(w) for w in weights]
                 + [const(n_post.shape), const(n_fpre.shape), const(n_fpost.shape)],
        out_specs=(const((n, D_MODEL)),) + tuple(chunk(w) for w in weights),
        out_shape=(jax.ShapeDtypeStruct((n, D_MODEL), F32),)
                  + tuple(jax.ShapeDtypeStruct(w.shape, BF16) for w in weights),
        scratch_shapes=[pltpu.VMEM(w.shape, BF16) for w in weights],
        compiler_params=pltpu.CompilerParams(
            dimension_semantics=("arbitrary",), vmem_limit_bytes=52 * MIB),
        name="decode_ffn",
    )(*_hbm(mix, x2d, w_out, w_ffn_in, w_ffn_out, n_post, n_fpre, n_fpost))


def _rotary_tables(pos):
    half = HEAD_DIM // 2
    inv = 1.0 / (ROPE_BASE ** np.linspace(0.0, 1.0, half))
    ang = np.asarray(pos, np.float64)[:, None] * inv[None, :]
    cos, sin = np.cos(ang), np.sin(ang)
    return (np.concatenate([cos, cos], axis=-1).astype(np.float32),
            np.concatenate([-sin, sin], axis=-1).astype(np.float32))


def _decay_tables():
    lg = np.log(1.0 - 2.0 ** (-5.0 - np.arange(RET_HEADS, dtype=np.float64)))
    idx = np.arange(CHUNK, dtype=np.float64)
    diff = idx[:, None] - idx[None, :]
    mask = np.where(diff[None] >= 0, np.exp(lg[:, None, None] * np.maximum(diff, 0.0)[None]), 0.0)
    wide = lambda col: np.broadcast_to(col[:, :, None], (RET_HEADS, CHUNK, HEAD_DIM))
    dec_end = wide(np.exp(lg[:, None] * (CHUNK - 1 - idx)[None, :]))
    dec_start = wide(np.exp(lg[:, None] * (idx + 1.0)[None, :]))
    chunk_dec = np.broadcast_to(np.exp(lg * CHUNK)[:, None, None], (RET_HEADS, HEAD_DIM, HEAD_DIM))
    gam = np.broadcast_to(np.exp(lg)[:, None], (RET_HEADS, HEAD_DIM))
    return tuple(np.ascontiguousarray(t, np.float32) for t in (mask, dec_start, dec_end, chunk_dec, gam))


def kernel(x_prompt, x_sample, state_ret, state_conv, meta_tokens, norm_mix_pre, norm_mix_post,
           norm_ffn_pre, norm_ffn_post, w_in, conv_w, conv_b, conv_ln_g, conv_ln_b, w_out,
           w_ffn_in, w_ffn_out):
    batch, seq, _ = x_prompt.shape
    n_dec, dec_seq, _ = x_sample.shape
    depth = w_in.shape[0]
    assert depth == 1 and dec_seq == 1 and seq % PROMPT_TILE == 0
    assert n_dec % SAMPLE_BLOCK == 0 and (batch * seq) % FFN_TILE == 0

    row = lambda v: v.reshape(1, -1)
    nw_pre, nw_post = row(norm_mix_pre[0]), row(norm_mix_post[0])
    nf_pre, nf_post = row(norm_ffn_pre[0]), row(norm_ffn_post[0])
    cw, cb, lg, lb = conv_w[0], row(conv_b[0]), row(conv_ln_g[0]), row(conv_ln_b[0])

    mask, dec_start, dec_end, chunk_dec, gam = _decay_tables()
    pos_small = np.concatenate([
        np.full((n_dec,), PAST_LEN), np.zeros((CHUNK - N_META,)), np.arange(N_META)])
    cos_s, sin_s = _rotary_tables(pos_small)
    cos_p, sin_p = _rotary_tables(N_META + np.arange(seq))

    xs = jnp.concatenate([
        x_sample.reshape(n_dec, D_MODEL),
        jnp.zeros((CHUNK - N_META, D_MODEL), x_sample.dtype),
        meta_tokens.astype(x_prompt.dtype)], axis=0)
    v_s, g_s, u_s, qt, kt, r0, tail0, w_in_b = _small_inproj(
        xs, nw_pre, w_in[0], cos_s, sin_s, dec_end, n_dec)

    x2d = x_prompt.reshape(batch * seq, D_MODEL)
    mix_p, ret_p, conv_p = _prompt_mix(
        x2d, nw_pre, w_in_b, cos_p, sin_p, mask, dec_start, dec_end, chunk_dec, r0, tail0,
        cw, cb, lg, lb, batch, seq)

    blocks = n_dec // SAMPLE_BLOCK
    to_blocks = lambda t: t.reshape(RET_W, blocks, SAMPLE_BLOCK).transpose(1, 0, 2)
    mix_s, ret_s, conv_s = _sample_mix(
        to_blocks(qt), to_blocks(kt), v_s, g_s, u_s, gam, state_ret[0],
        state_conv[0].transpose(1, 0, 2), cw, cb, lg, lb)
    conv_s = conv_s.transpose(1, 0, 2)

    y_s, w_out_b, w_fi_b, w_fo_b = _decode_ffn(
        mix_s, x_sample.reshape(n_dec, D_MODEL), w_out[0], w_ffn_in[0], w_ffn_out[0],
        nw_post, nf_pre, nf_post)
    y_p = _out_ffn(mix_p, x2d, w_out_b, w_fi_b, w_fo_b, nw_post, nf_pre, nf_post, FFN_TILE)

    return (y_p.reshape(batch, seq, D_MODEL), y_s.reshape(n_dec, 1, D_MODEL),
            ret_p[None], conv_p[None], ret_s[None], conv_s[None])
```

```python
import jax
import jax.numpy as jnp
import numpy as np
from jax import lax
from jax.experimental import pallas as pl
from jax.experimental.pallas import tpu as pltpu

F32 = jnp.float32
BF16 = jnp.bfloat16

D_MODEL = 1024
N_META = 16
PAST_LEN = 16384
RET_HEADS = 4
HEAD_DIM = 128
RET_W = RET_HEADS * HEAD_DIM
CONV_CH = D_MODEL - RET_W
CONV_WIDTH = 31
CONV_TAIL = CONV_WIDTH - 1
CHUNK = 128
ROPE_BASE = 10000.0
D_FF = 2816
IN_COLS = 4 * RET_W + 2 * CONV_CH
EPS = 1e-6
K_SCALE = HEAD_DIM ** -0.5

SUBLANES = 8
TAIL_ROWS = 32
TAIL_SKIP = TAIL_ROWS - CONV_TAIL

PROMPT_TILE = 1024
CONV_ROW_STRIDE = 4
FFN_TILE = 1024
FFN_GROUP_ROWS = 256
FFN_CHUNK = 1024
WEIGHT_CHUNKS = 8
INPROJ_WEIGHT_CHUNKS = 4
MIB = 1024 * 1024


def _hbm(*xs):
    return tuple(pltpu.with_memory_space_constraint(jnp.asarray(x), pltpu.HBM) for x in xs)


def _rms(x, w):
    return x * lax.rsqrt(jnp.mean(x * x, axis=-1, keepdims=True) + EPS) * w


def _rot(xh, cos2, sin2):
    return xh * cos2 + pltpu.roll(xh, HEAD_DIM // 2, 1) * sin2


def _head(i):
    return slice(i * HEAD_DIM, (i + 1) * HEAD_DIM)


def _dot(a, b):
    return jnp.dot(a, b, preferred_element_type=F32)


def _dot_nt(a, b):
    return lax.dot_general(a, b, (((1,), (1,)), ((), ())), preferred_element_type=F32)


def _dot_tn(a, b):
    return lax.dot_general(a, b, (((0,), (0,)), ((), ())), preferred_element_type=F32)


def _layer_norm_swish(y, g, b):
    mu = jnp.mean(y, axis=-1, keepdims=True)
    var = jnp.mean(jnp.square(y - mu), axis=-1, keepdims=True)
    yn = (y - mu) * lax.rsqrt(var + EPS) * g + b
    return jax.nn.silu(yn)


def _small_inproj_kernel(x_ref, nw_ref, wc_ref, cos_ref, sin_ref, de_ref,
                         v_ref, g_ref, u_ref, qt_ref, kt_ref, r0_ref, tail0_ref, wb_ref,
                         p_scr, w_scr):
    i = pl.program_id(0)
    n_chunks = pl.num_programs(0) - 1

    @pl.when(i < n_chunks)
    def _():
        rows = wc_ref.shape[0]
        chunk = wc_ref[...].astype(BF16)
        w_scr[pl.ds(pl.multiple_of(i * rows, rows), rows), :] = chunk
        wb_ref[...] = chunk

    @pl.when(i == n_chunks)
    def _():
        n = v_ref.shape[0]
        xn = _rms(x_ref[...], nw_ref[...]).astype(BF16)
        p_scr[...] = _dot(xn, w_scr[...])
        dec = slice(0, n)
        met = slice(n, n + CHUNK)
        for h in range(RET_HEADS):
            q = _rot(p_scr[dec, _head(h)], cos_ref[dec, :], sin_ref[dec, :])
            k = _rot(p_scr[dec, _head(RET_HEADS + h)], cos_ref[dec, :], sin_ref[dec, :]) * K_SCALE
            qt_ref[_head(h), :] = q.T
            kt_ref[_head(h), :] = k.T
            km = _rot(p_scr[met, _head(RET_HEADS + h)], cos_ref[met, :], sin_ref[met, :]) * K_SCALE
            vm = p_scr[met, _head(2 * RET_HEADS + h)]
            r0_ref[h] = _dot_tn((km * de_ref[h]).astype(BF16), vm.astype(BF16))
        v_ref[...] = p_scr[dec, 2 * RET_W:3 * RET_W]
        g_ref[...] = p_scr[dec, 3 * RET_W:4 * RET_W]
        a = p_scr[:, 4 * RET_W:4 * RET_W + CONV_CH]
        b = p_scr[:, 4 * RET_W + CONV_CH:]
        u = a * jax.nn.sigmoid(b)
        u_ref[...] = u[dec, :]
        tail0_ref[...] = u[n + CHUNK - TAIL_ROWS:n + CHUNK, :]


def _small_inproj(xs, nw, w_in, cos_s, sin_s, dec_end, n_dec):
    rows = xs.shape[0]
    nc = INPROJ_WEIGHT_CHUNKS
    last = nc - 1
    assert w_in.shape[0] % (nc * 16) == 0
    full = lambda shape: pl.BlockSpec(shape, lambda i: (0,) * len(shape))
    chunk = pl.BlockSpec((w_in.shape[0] // nc, IN_COLS), lambda i: (jnp.minimum(i, last), 0))
    out_shapes = (
        jax.ShapeDtypeStruct((n_dec, RET_W), F32),
        jax.ShapeDtypeStruct((n_dec, RET_W), F32),
        jax.ShapeDtypeStruct((n_dec, CONV_CH), F32),
        jax.ShapeDtypeStruct((RET_W, n_dec), F32),
        jax.ShapeDtypeStruct((RET_W, n_dec), F32),
        jax.ShapeDtypeStruct((RET_HEADS, HEAD_DIM, HEAD_DIM), F32),
        jax.ShapeDtypeStruct((TAIL_ROWS, CONV_CH), F32),
        jax.ShapeDtypeStruct(w_in.shape, BF16),
    )
    return pl.pallas_call(
        _small_inproj_kernel,
        grid=(nc + 1,),
        in_specs=[full(xs.shape), full(nw.shape), chunk, full(cos_s.shape),
                  full(sin_s.shape), full(dec_end.shape)],
        out_specs=tuple(full(s.shape) for s in out_shapes[:-1]) + (chunk,),
        out_shape=out_shapes,
        scratch_shapes=[pltpu.VMEM((rows, IN_COLS), F32), pltpu.VMEM(w_in.shape, BF16)],
        compiler_params=pltpu.CompilerParams(
            dimension_semantics=("arbitrary",), vmem_limit_bytes=40 * MIB),
        name="small_inproj",
    )(*_hbm(xs, nw, w_in, cos_s, sin_s, dec_end))


def _prompt_mix_kernel(x_ref, nw_ref, w_ref, cos_ref, sin_ref, mask_ref, ds_ref, de_ref, cd_ref,
                       r0_ref, tail0_ref, cw_ref, cb_ref, lg_ref, lb_ref,
                       qt_ref, kt_ref, vs_ref, gs_ref, us_ref, gam_ref, sr_ref, sc_ref,
                       mix_ref, retp_ref, convp_ref, mixs_ref, so_ref, sco_ref,
                       p_scr, r_scr, u_scr, y_scr, o_scr):
    tile = x_ref.shape[0]
    j = pl.program_id(1)

    _sample_mix_kernel(qt_ref, kt_ref, vs_ref, gs_ref, us_ref, gam_ref, sr_ref, sc_ref,
                       cw_ref, cb_ref, lg_ref, lb_ref, mixs_ref, so_ref, sco_ref, o_scr)

    n_slabs = CONV_CH // HEAD_DIM

    @pl.when(j == 0)
    def _():
        r_scr[...] = r0_ref[...]
        for c in range(n_slabs):
            u_scr[c, 0:TAIL_ROWS, :] = tail0_ref[:, _head(c)]

    xn = _rms(x_ref[...], nw_ref[...]).astype(BF16)

    ab = _dot(xn, w_ref[:, 4 * RET_W:])
    u = ab[:, :CONV_CH] * jax.nn.sigmoid(ab[:, CONV_CH:])
    for c in range(n_slabs):
        u_scr[c, TAIL_ROWS:TAIL_ROWS + tile, :] = u[:, _head(c)]
    p_scr[...] = _dot(xn, w_ref[:, :4 * RET_W])
    block = CONV_ROW_STRIDE * SUBLANES
    for c in range(n_slabs):
        taps = [cw_ref[t:t + 1, _head(c)] for t in range(CONV_WIDTH)]
        bias = cb_ref[:, _head(c)]
        for b0 in range(0, tile, block):
            acc = [None] * CONV_ROW_STRIDE
            for p in range(CONV_ROW_STRIDE + CONV_WIDTH - 1):
                win = u_scr[c, pl.ds(TAIL_SKIP + b0 + p, SUBLANES, stride=CONV_ROW_STRIDE), :]
                for r in range(CONV_ROW_STRIDE):
                    t = p - r
                    if 0 <= t < CONV_WIDTH:
                        term = taps[t] * win
                        acc[r] = term if acc[r] is None else acc[r] + term
            for r in range(CONV_ROW_STRIDE):
                y_scr[c, pl.ds(b0 + r, SUBLANES, stride=CONV_ROW_STRIDE), :] = acc[r] + bias
    ys = [y_scr[c] for c in range(n_slabs)]
    mu = sum(jnp.sum(y, axis=-1, keepdims=True) for y in ys) * (1.0 / CONV_CH)
    dev = [y - mu for y in ys]
    var = sum(jnp.sum(d * d, axis=-1, keepdims=True) for d in dev) * (1.0 / CONV_CH)
    inv = lax.rsqrt(var + EPS)
    for c in range(n_slabs):
        yn = dev[c] * inv * lg_ref[:, _head(c)] + lb_ref[:, _head(c)]
        mix_ref[:, RET_W + c * HEAD_DIM:RET_W + (c + 1) * HEAD_DIM] = jax.nn.silu(yn).astype(BF16)
        u_scr[c, 0:TAIL_ROWS, :] = u_scr[c, tile:tile + TAIL_ROWS, :]

    for c in range(tile // CHUNK):
        rows = slice(c * CHUNK, (c + 1) * CHUNK)
        cos2 = cos_ref[rows, :]
        sin2 = sin_ref[rows, :]
        for h in range(RET_HEADS):
            q = _rot(p_scr[rows, _head(h)], cos2, sin2)
            k = _rot(p_scr[rows, _head(RET_HEADS + h)], cos2, sin2) * K_SCALE
            vb = p_scr[rows, _head(2 * RET_HEADS + h)].astype(BF16)
            gate = p_scr[rows, _head(3 * RET_HEADS + h)]
            qb = q.astype(BF16)
            scores = _dot_nt(qb, k.astype(BF16)) * mask_ref[h]
            r_prev = r_scr[h]
            o = _dot(scores.astype(BF16), vb) + _dot(qb, r_prev.astype(BF16)) * ds_ref[h]
            kv = _dot_tn((k * de_ref[h]).astype(BF16), vb)
            r_scr[h] = cd_ref[h] * r_prev + kv
            o = o * lax.rsqrt(jnp.mean(o * o, axis=-1, keepdims=True) + EPS)
            mix_ref[rows, _head(h)] = (o * jax.nn.silu(gate)).astype(BF16)

    @pl.when(j == pl.num_programs(1) - 1)
    def _():
        retp_ref[...] = r_scr[...]
        for c in range(n_slabs):
            convp_ref[:, _head(c)] = u_scr[c, TAIL_SKIP:TAIL_ROWS, :]


def _prompt_mix(x2d, nw, w_in, cos_p, sin_p, mask, dec_start, dec_end, chunk_dec, r0, tail0,
                conv_w, conv_b, ln_g, ln_b, qt3, kt3, v_s, g_s, u_s, gam, state_ret, state_conv,
                batch, seq):
    tile = PROMPT_TILE
    nt = seq // tile
    n_dec = v_s.shape[0]
    nb = n_dec // (batch * nt)
    assert nb * batch * nt == n_dec and nb % SUBLANES == 0 and qt3.shape == (batch * nt, RET_W, nb)
    const = lambda shape: pl.BlockSpec(shape, lambda b, j: (0,) * len(shape))
    step = lambda b, j: b * nt + j
    dec_rows = lambda width: pl.BlockSpec((nb, width), lambda b, j: (step(b, j), 0))
    dec_cols = pl.BlockSpec((None, RET_W, nb), lambda b, j: (step(b, j), 0, 0))
    st_spec = pl.BlockSpec((nb, RET_HEADS, HEAD_DIM, HEAD_DIM), lambda b, j: (step(b, j), 0, 0, 0))
    cv_spec = pl.BlockSpec((CONV_TAIL, nb, CONV_CH), lambda b, j: (0, step(b, j), 0))
    out_shapes = (
        jax.ShapeDtypeStruct((batch * seq, D_MODEL), BF16),
        jax.ShapeDtypeStruct((batch, RET_HEADS, HEAD_DIM, HEAD_DIM), F32),
        jax.ShapeDtypeStruct((batch, CONV_TAIL, CONV_CH), F32),
        jax.ShapeDtypeStruct((n_dec, D_MODEL), F32),
        jax.ShapeDtypeStruct(state_ret.shape, F32),
        jax.ShapeDtypeStruct(state_conv.shape, F32),
    )
    return pl.pallas_call(
        _prompt_mix_kernel,
        grid=(batch, nt),
        in_specs=[
            pl.BlockSpec((tile, D_MODEL), lambda b, j: (b * nt + j, 0)),
            const(nw.shape), const(w_in.shape),
            pl.BlockSpec((tile, HEAD_DIM), lambda b, j: (j, 0)),
            pl.BlockSpec((tile, HEAD_DIM), lambda b, j: (j, 0)),
            const(mask.shape), const(dec_start.shape), const(dec_end.shape), const(chunk_dec.shape),
            const(r0.shape), const(tail0.shape), const(conv_w.shape), const(conv_b.shape),
            const(ln_g.shape), const(ln_b.shape),
            dec_cols, dec_cols, dec_rows(RET_W), dec_rows(RET_W), dec_rows(CONV_CH), const(gam.shape),
            st_spec, cv_spec,
        ],
        out_specs=(
            pl.BlockSpec((tile, D_MODEL), lambda b, j: (b * nt + j, 0)),
            pl.BlockSpec((None, RET_HEADS, HEAD_DIM, HEAD_DIM), lambda b, j: (b, 0, 0, 0)),
            pl.BlockSpec((None, CONV_TAIL, CONV_CH), lambda b, j: (b, 0, 0)),
            dec_rows(D_MODEL), st_spec, cv_spec,
        ),
        out_shape=out_shapes,
        scratch_shapes=[
            pltpu.VMEM((tile, 4 * RET_W), F32),
            pltpu.VMEM((RET_HEADS, HEAD_DIM, HEAD_DIM), F32),
            pltpu.VMEM((CONV_CH // HEAD_DIM, TAIL_ROWS + tile, HEAD_DIM), F32),
            pltpu.VMEM((CONV_CH // HEAD_DIM, tile, HEAD_DIM), F32),
            pltpu.VMEM((nb, RET_W), F32),
        ],
        compiler_params=pltpu.CompilerParams(
            dimension_semantics=("arbitrary", "arbitrary"), vmem_limit_bytes=56 * MIB),
        name="prompt_mix",
    )(*_hbm(x2d, nw, w_in, cos_p, sin_p, mask, dec_start, dec_end, chunk_dec, r0, tail0,
            conv_w, conv_b, ln_g, ln_b, qt3, kt3, v_s, g_s, u_s, gam, state_ret, state_conv))


def _sample_mix_kernel(qt_ref, kt_ref, v_ref, g_ref, u_ref, gam_ref, sr_ref, sc_ref,
                       cw_ref, cb_ref, lg_ref, lb_ref,
                       mix_ref, so_ref, sco_ref, o_scr):
    nb = v_ref.shape[0]
    for t in range(nb):
        for h in range(RET_HEADS):
            kc = kt_ref[_head(h), t:t + 1]
            qc = qt_ref[_head(h), t:t + 1]
            vr = v_ref[t:t + 1, _head(h)]
            s_new = gam_ref[h:h + 1, :] * sr_ref[t, h] + kc * vr
            so_ref[t, h] = s_new
            o_scr[t:t + 1, _head(h)] = jnp.sum(qc * s_new, axis=0, keepdims=True)
    for h in range(RET_HEADS):
        o = o_scr[:, _head(h)]
        o = o * lax.rsqrt(jnp.mean(o * o, axis=-1, keepdims=True) + EPS)
        mix_ref[:, _head(h)] = o * jax.nn.silu(g_ref[:, _head(h)])
    u = u_ref[...]
    y = u * cw_ref[CONV_TAIL:CONV_WIDTH, :] + cb_ref[...]
    for t in range(CONV_TAIL):
        y = y + sc_ref[t] * cw_ref[t:t + 1, :]
    mix_ref[:, RET_W:] = _layer_norm_swish(y, lg_ref[...], lb_ref[...])
    for t in range(CONV_TAIL - 1):
        sco_ref[t] = sc_ref[t + 1]
    sco_ref[CONV_TAIL - 1] = u


def _out_ffn_kernel(mix_ref, x_ref, wo_ref, wi_ref, wd_ref, npost_ref, nfpre_ref, nfpost_ref, y_ref):
    tile = x_ref.shape[0]
    n_groups = max(1, tile // FFN_GROUP_ROWS)
    groups = [slice(i * tile // n_groups, (i + 1) * tile // n_groups) for i in range(n_groups)]
    a = [_dot(mix_ref[r, :].astype(BF16), wo_ref[...]) for r in groups]
    h1 = [x_ref[r, :] + _rms(a[i], npost_ref[...]) for i, r in enumerate(groups)]
    xf = [_rms(h, nfpre_ref[...]).astype(BF16) for h in h1]
    dn = [None] * n_groups
    for c0 in range(0, D_FF, FFN_CHUNK):
        c1 = min(c0 + FFN_CHUNK, D_FF)
        for i in range(n_groups):
            gate = _dot(xf[i], wi_ref[:, c0:c1])
            up = _dot(xf[i], wi_ref[:, D_FF + c0:D_FF + c1])
            part = _dot((jax.nn.silu(gate) * up).astype(BF16), wd_ref[c0:c1, :])
            dn[i] = part if dn[i] is None else dn[i] + part
    for i, r in enumerate(groups):
        y_ref[r, :] = h1[i] + _rms(dn[i], nfpost_ref[...])


def _out_ffn(mix, x2d, w_out, w_ffn_in, w_ffn_out, n_post, n_fpre, n_fpost, tile):
    n = x2d.shape[0]
    const = lambda shape: pl.BlockSpec(shape, lambda i: (0,) * len(shape),
                                       pipeline_mode=pl.Buffered(1))
    rows = pl.BlockSpec((tile, D_MODEL), lambda i: (i, 0))
    return pl.pallas_call(
        _out_ffn_kernel,
        grid=(n // tile,),
        in_specs=[rows, rows, const(w_out.shape), const(w_ffn_in.shape), const(w_ffn_out.shape),
                  const(n_post.shape), const(n_fpre.shape), const(n_fpost.shape)],
        out_specs=rows,
        out_shape=jax.ShapeDtypeStruct((n, D_MODEL), F32),
        compiler_params=pltpu.CompilerParams(
            dimension_semantics=("arbitrary",), vmem_limit_bytes=60 * MIB),
        name="out_ffn",
    )(*_hbm(mix, x2d, w_out, w_ffn_in, w_ffn_out, n_post, n_fpre, n_fpost))


def _decode_ffn_kernel(mix_ref, x_ref, woc_ref, wic_ref, wdc_ref, npost_ref, nfpre_ref, nfpost_ref,
                       y_ref, wob_ref, wib_ref, wdb_ref, wo_scr, wi_scr, wd_scr):
    i = pl.program_id(0)
    n_chunks = pl.num_programs(0) - 1

    @pl.when(i < n_chunks)
    def _():
        for src, dst, out in ((woc_ref, wo_scr, wob_ref), (wic_ref, wi_scr, wib_ref),
                              (wdc_ref, wd_scr, wdb_ref)):
            rows = src.shape[0]
            chunk = src[...].astype(BF16)
            dst[pl.ds(pl.multiple_of(i * rows, rows), rows), :] = chunk
            out[...] = chunk

    @pl.when(i == n_chunks)
    def _():
        _out_ffn_kernel(mix_ref, x_ref, wo_scr, wi_scr, wd_scr, npost_ref, nfpre_ref, nfpost_ref, y_ref)


def _decode_ffn(mix, x2d, w_out, w_ffn_in, w_ffn_out, n_post, n_fpre, n_fpost):
    n = x2d.shape[0]
    nc = WEIGHT_CHUNKS
    last = nc - 1
    const = lambda shape: pl.BlockSpec(shape, lambda i: (0,) * len(shape))
    chunk = lambda w: pl.BlockSpec((w.shape[0] // nc, w.shape[1]), lambda i: (jnp.minimum(i, last), 0))
    weights = (w_out, w_ffn_in, w_ffn_out)
    assert all(w.shape[0] % (nc * 16) == 0 for w in weights)
    return pl.pallas_call(
        _decode_ffn_kernel,
        grid=(nc + 1,),
        in_specs=[const(mix.shape), const(x2d.shape)] + [chunk(w) for w in weights]
                 + [const(n_post.shape), const(n_fpre.shape), const(n_fpost.shape)],
        out_specs=(const((n, D_MODEL)),) + tuple(chunk(w) for w in weights),
        out_shape=(jax.ShapeDtypeStruct((n, D_MODEL), F32),)
                  + tuple(jax.ShapeDtypeStruct(w.shape, BF16) for w in weights),
        scratch_shapes=[pltpu.VMEM(w.shape, BF16) for w in weights],
        compiler_params=pltpu.CompilerParams(
            dimension_semantics=("arbitrary",), vmem_limit_bytes=52 * MIB),
        name="decode_ffn",
    )(*_hbm(mix, x2d, w_out, w_ffn_in, w_ffn_out, n_post, n_fpre, n_fpost))


def _rotary_tables(pos):
    half = HEAD_DIM // 2
    inv = 1.0 / (ROPE_BASE ** np.linspace(0.0, 1.0, half))
    ang = np.asarray(pos, np.float64)[:, None] * inv[None, :]
    cos, sin = np.cos(ang), np.sin(ang)
    return (np.concatenate([cos, cos], axis=-1).astype(np.float32),
            np.concatenate([-sin, sin], axis=-1).astype(np.float32))


def _decay_tables():
    lg = np.log(1.0 - 2.0 ** (-5.0 - np.arange(RET_HEADS, dtype=np.float64)))
    idx = np.arange(CHUNK, dtype=np.float64)
    diff = idx[:, None] - idx[None, :]
    mask = np.where(diff[None] >= 0, np.exp(lg[:, None, None] * np.maximum(diff, 0.0)[None]), 0.0)
    wide = lambda col: np.broadcast_to(col[:, :, None], (RET_HEADS, CHUNK, HEAD_DIM))
    dec_end = wide(np.exp(lg[:, None] * (CHUNK - 1 - idx)[None, :]))
    dec_start = wide(np.exp(lg[:, None] * (idx + 1.0)[None, :]))
    chunk_dec = np.broadcast_to(np.exp(lg * CHUNK)[:, None, None], (RET_HEADS, HEAD_DIM, HEAD_DIM))
    gam = np.broadcast_to(np.exp(lg)[:, None], (RET_HEADS, HEAD_DIM))
    return tuple(np.ascontiguousarray(t, np.float32) for t in (mask, dec_start, dec_end, chunk_dec, gam))


def kernel(x_prompt, x_sample, state_ret, state_conv, meta_tokens, norm_mix_pre, norm_mix_post,
           norm_ffn_pre, norm_ffn_post, w_in, conv_w, conv_b, conv_ln_g, conv_ln_b, w_out,
           w_ffn_in, w_ffn_out):
    batch, seq, _ = x_prompt.shape
    n_dec, dec_seq, _ = x_sample.shape
    depth = w_in.shape[0]
    assert depth == 1 and dec_seq == 1 and seq % PROMPT_TILE == 0
    assert (batch * seq) % FFN_TILE == 0

    row = lambda v: v.reshape(1, -1)
    nw_pre, nw_post = row(norm_mix_pre[0]), row(norm_mix_post[0])
    nf_pre, nf_post = row(norm_ffn_pre[0]), row(norm_ffn_post[0])
    cw, cb, lg, lb = conv_w[0], row(conv_b[0]), row(conv_ln_g[0]), row(conv_ln_b[0])

    mask, dec_start, dec_end, chunk_dec, gam = _decay_tables()
    pos_small = np.concatenate([
        np.full((n_dec,), PAST_LEN), np.zeros((CHUNK - N_META,)), np.arange(N_META)])
    cos_s, sin_s = _rotary_tables(pos_small)
    cos_p, sin_p = _rotary_tables(N_META + np.arange(seq))

    xs = jnp.concatenate([
        x_sample.reshape(n_dec, D_MODEL),
        jnp.zeros((CHUNK - N_META, D_MODEL), x_sample.dtype),
        meta_tokens.astype(x_prompt.dtype)], axis=0)
    v_s, g_s, u_s, qt, kt, r0, tail0, w_in_b = _small_inproj(
        xs, nw_pre, w_in[0], cos_s, sin_s, dec_end, n_dec)

    x2d = x_prompt.reshape(batch * seq, D_MODEL)
    blocks = batch * (seq // PROMPT_TILE)
    to_blocks = lambda t: t.reshape(RET_W, blocks, n_dec // blocks).transpose(1, 0, 2)
    mix_p, ret_p, conv_p, mix_s, ret_s, conv_s = _prompt_mix(
        x2d, nw_pre, w_in_b, cos_p, sin_p, mask, dec_start, dec_end, chunk_dec, r0, tail0,
        cw, cb, lg, lb, to_blocks(qt), to_blocks(kt), v_s, g_s, u_s, gam, state_ret[0],
        state_conv[0].transpose(1, 0, 2), batch, seq)
    conv_s = conv_s.transpose(1, 0, 2)

    y_s, w_out_b, w_fi_b, w_fo_b = _decode_ffn(
        mix_s, x_sample.reshape(n_dec, D_MODEL), w_out[0], w_ffn_in[0], w_ffn_out[0],
        nw_post, nf_pre, nf_post)
    y_p = _out_ffn(mix_p, x2d, w_out_b, w_fi_b, w_fo_b, nw_post, nf_pre, nf_post, FFN_TILE)

    return (y_p.reshape(batch, seq, D_MODEL), y_s.reshape(n_dec, 1, D_MODEL),
            ret_p[None], conv_p[None], ret_s[None], conv_s[None])
```

```python
import jax
import jax.numpy as jnp
import numpy as np
from jax import lax
from jax.experimental import pallas as pl
from jax.experimental.pallas import tpu as pltpu

F32 = jnp.float32
BF16 = jnp.bfloat16

D_MODEL = 1024
N_META = 16
PAST_LEN = 16384
RET_HEADS = 4
HEAD_DIM = 128
RET_W = RET_HEADS * HEAD_DIM
CONV_CH = D_MODEL - RET_W
CONV_WIDTH = 31
CONV_TAIL = CONV_WIDTH - 1
CHUNK = 128
ROPE_BASE = 10000.0
D_FF = 2816
IN_COLS = 4 * RET_W + 2 * CONV_CH
EPS = 1e-6
K_SCALE = HEAD_DIM ** -0.5

SUBLANES = 8
TAIL_ROWS = 32
TAIL_SKIP = TAIL_ROWS - CONV_TAIL

PROMPT_TILE = 1024
CONV_ROW_STRIDE = 4
FFN_TILE = 1024
FFN_GROUP_ROWS = 256
FFN_CHUNK = 768
WEIGHT_CHUNKS = 8
MIB = 1024 * 1024


def _hbm(*xs):
    return tuple(pltpu.with_memory_space_constraint(jnp.asarray(x), pltpu.HBM) for x in xs)


def _rms(x, w):
    return x * lax.rsqrt(jnp.mean(x * x, axis=-1, keepdims=True) + EPS) * w


def _rot(xh, cos2, sin2):
    return xh * cos2 + pltpu.roll(xh, HEAD_DIM // 2, 1) * sin2


def _head(i):
    return slice(i * HEAD_DIM, (i + 1) * HEAD_DIM)


def _dot(a, b):
    return jnp.dot(a, b, preferred_element_type=F32)


def _dot_nt(a, b):
    return lax.dot_general(a, b, (((1,), (1,)), ((), ())), preferred_element_type=F32)


def _dot_tn(a, b):
    return lax.dot_general(a, b, (((0,), (0,)), ((), ())), preferred_element_type=F32)


def _layer_norm_swish(y, g, b):
    mu = jnp.mean(y, axis=-1, keepdims=True)
    var = jnp.mean(jnp.square(y - mu), axis=-1, keepdims=True)
    yn = (y - mu) * lax.rsqrt(var + EPS) * g + b
    return jax.nn.silu(yn)


def _small_inproj_kernel(x_ref, nw_ref, wc_ref, cos_ref, sin_ref, de_ref, woc_ref, wic_ref, wdc_ref,
                         v_ref, g_ref, u_ref, qt_ref, kt_ref, r0_ref, tail0_ref, wb_ref,
                         wob_ref, wib_ref, wdb_ref, p_scr, w_scr):
    i = pl.program_id(0)
    n_chunks = pl.num_programs(0) - 1

    @pl.when(i < n_chunks)
    def _():
        rows = wc_ref.shape[0]
        chunk = wc_ref[...].astype(BF16)
        w_scr[pl.ds(pl.multiple_of(i * rows, rows), rows), :] = chunk
        wb_ref[...] = chunk
        wob_ref[...] = woc_ref[...].astype(BF16)
        wib_ref[...] = wic_ref[...].astype(BF16)
        wdb_ref[...] = wdc_ref[...].astype(BF16)

    @pl.when(i == n_chunks)
    def _():
        n = v_ref.shape[0]
        xn = _rms(x_ref[...], nw_ref[...]).astype(BF16)
        p_scr[...] = _dot(xn, w_scr[...])
        dec = slice(0, n)
        met = slice(n, n + CHUNK)
        for h in range(RET_HEADS):
            q = _rot(p_scr[dec, _head(h)], cos_ref[dec, :], sin_ref[dec, :])
            k = _rot(p_scr[dec, _head(RET_HEADS + h)], cos_ref[dec, :], sin_ref[dec, :]) * K_SCALE
            qt_ref[_head(h), :] = q.T
            kt_ref[_head(h), :] = k.T
            km = _rot(p_scr[met, _head(RET_HEADS + h)], cos_ref[met, :], sin_ref[met, :]) * K_SCALE
            vm = p_scr[met, _head(2 * RET_HEADS + h)]
            r0_ref[h] = _dot_tn((km * de_ref[h]).astype(BF16), vm.astype(BF16))
        v_ref[...] = p_scr[dec, 2 * RET_W:3 * RET_W]
        g_ref[...] = p_scr[dec, 3 * RET_W:4 * RET_W]
        a = p_scr[:, 4 * RET_W:4 * RET_W + CONV_CH]
        b = p_scr[:, 4 * RET_W + CONV_CH:]
        u = a * jax.nn.sigmoid(b)
        u_ref[...] = u[dec, :]
        tail0_ref[...] = u[n + CHUNK - TAIL_ROWS:n + CHUNK, :]


def _small_inproj(xs, nw, w_in, cos_s, sin_s, dec_end, w_out, w_ffn_in, w_ffn_out, n_dec):
    rows = xs.shape[0]
    nc = WEIGHT_CHUNKS
    last = nc - 1
    weights = (w_in, w_out, w_ffn_in, w_ffn_out)
    assert all(w.shape[0] % (nc * 16) == 0 for w in weights)
    full = lambda shape: pl.BlockSpec(shape, lambda i: (0,) * len(shape))
    chunk_of = lambda w: pl.BlockSpec((w.shape[0] // nc, w.shape[1]), lambda i: (jnp.minimum(i, last), 0))
    chunk = chunk_of(w_in)
    out_shapes = (
        jax.ShapeDtypeStruct((n_dec, RET_W), F32),
        jax.ShapeDtypeStruct((n_dec, RET_W), F32),
        jax.ShapeDtypeStruct((n_dec, CONV_CH), F32),
        jax.ShapeDtypeStruct((RET_W, n_dec), F32),
        jax.ShapeDtypeStruct((RET_W, n_dec), F32),
        jax.ShapeDtypeStruct((RET_HEADS, HEAD_DIM, HEAD_DIM), F32),
        jax.ShapeDtypeStruct((TAIL_ROWS, CONV_CH), F32),
        jax.ShapeDtypeStruct(w_in.shape, BF16),
        jax.ShapeDtypeStruct(w_out.shape, BF16),
        jax.ShapeDtypeStruct(w_ffn_in.shape, BF16),
        jax.ShapeDtypeStruct(w_ffn_out.shape, BF16),
    )
    return pl.pallas_call(
        _small_inproj_kernel,
        grid=(nc + 1,),
        in_specs=[full(xs.shape), full(nw.shape), chunk, full(cos_s.shape),
                  full(sin_s.shape), full(dec_end.shape)] + [chunk_of(w) for w in weights[1:]],
        out_specs=tuple(full(s.shape) for s in out_shapes[:-4]) + tuple(chunk_of(w) for w in weights),
        out_shape=out_shapes,
        scratch_shapes=[pltpu.VMEM((rows, IN_COLS), F32), pltpu.VMEM(w_in.shape, BF16)],
        compiler_params=pltpu.CompilerParams(
            dimension_semantics=("arbitrary",), vmem_limit_bytes=40 * MIB),
        name="small_inproj",
    )(*_hbm(xs, nw, w_in, cos_s, sin_s, dec_end, w_out, w_ffn_in, w_ffn_out))


def _prompt_mix_kernel(x_ref, nw_ref, w_ref, cos_ref, sin_ref, mask_ref, ds_ref, de_ref, cd_ref,
                       r0_ref, tail0_ref, cw_ref, cb_ref, lg_ref, lb_ref,
                       mix_ref, retp_ref, convp_ref,
                       p_scr, r_scr, u_scr, y_scr):
    tile = x_ref.shape[0]
    j = pl.program_id(1)

    n_slabs = CONV_CH // HEAD_DIM

    @pl.when(j == 0)
    def _():
        r_scr[...] = r0_ref[...]
        for c in range(n_slabs):
            u_scr[c, 0:TAIL_ROWS, :] = tail0_ref[:, _head(c)]

    xn = _rms(x_ref[...], nw_ref[...]).astype(BF16)

    ab = _dot(xn, w_ref[:, 4 * RET_W:])
    u = ab[:, :CONV_CH] * jax.nn.sigmoid(ab[:, CONV_CH:])
    for c in range(n_slabs):
        u_scr[c, TAIL_ROWS:TAIL_ROWS + tile, :] = u[:, _head(c)]
    p_scr[...] = _dot(xn, w_ref[:, :4 * RET_W])
    block = CONV_ROW_STRIDE * SUBLANES
    for c in range(n_slabs):
        taps = [cw_ref[t:t + 1, _head(c)] for t in range(CONV_WIDTH)]
        bias = cb_ref[:, _head(c)]
        for b0 in range(0, tile, block):
            acc = [None] * CONV_ROW_STRIDE
            for p in range(CONV_ROW_STRIDE + CONV_WIDTH - 1):
                win = u_scr[c, pl.ds(TAIL_SKIP + b0 + p, SUBLANES, stride=CONV_ROW_STRIDE), :]
                for r in range(CONV_ROW_STRIDE):
                    t = p - r
                    if 0 <= t < CONV_WIDTH:
                        term = taps[t] * win
                        acc[r] = term if acc[r] is None else acc[r] + term
            for r in range(CONV_ROW_STRIDE):
                y_scr[c, pl.ds(b0 + r, SUBLANES, stride=CONV_ROW_STRIDE), :] = acc[r] + bias
    ys = [y_scr[c] for c in range(n_slabs)]
    mu = sum(jnp.sum(y, axis=-1, keepdims=True) for y in ys) * (1.0 / CONV_CH)
    dev = [y - mu for y in ys]
    var = sum(jnp.sum(d * d, axis=-1, keepdims=True) for d in dev) * (1.0 / CONV_CH)
    inv = lax.rsqrt(var + EPS)
    for c in range(n_slabs):
        yn = dev[c] * inv * lg_ref[:, _head(c)] + lb_ref[:, _head(c)]
        mix_ref[:, RET_W + c * HEAD_DIM:RET_W + (c + 1) * HEAD_DIM] = jax.nn.silu(yn).astype(BF16)
        u_scr[c, 0:TAIL_ROWS, :] = u_scr[c, tile:tile + TAIL_ROWS, :]

    for c in range(tile // CHUNK):
        rows = slice(c * CHUNK, (c + 1) * CHUNK)
        cos2 = cos_ref[rows, :]
        sin2 = sin_ref[rows, :]
        for h in range(RET_HEADS):
            q = _rot(p_scr[rows, _head(h)], cos2, sin2)
            k = _rot(p_scr[rows, _head(RET_HEADS + h)], cos2, sin2) * K_SCALE
            vb = p_scr[rows, _head(2 * RET_HEADS + h)].astype(BF16)
            gate = p_scr[rows, _head(3 * RET_HEADS + h)]
            qb = q.astype(BF16)
            scores = _dot_nt(qb, k.astype(BF16)) * mask_ref[h]
            r_prev = r_scr[h]
            o = _dot(scores.astype(BF16), vb) + _dot(qb, r_prev.astype(BF16)) * ds_ref[h]
            kv = _dot_tn((k * de_ref[h]).astype(BF16), vb)
            r_scr[h] = cd_ref[h] * r_prev + kv
            o = o * lax.rsqrt(jnp.mean(o * o, axis=-1, keepdims=True) + EPS)
            mix_ref[rows, _head(h)] = (o * jax.nn.silu(gate)).astype(BF16)

    @pl.when(j == pl.num_programs(1) - 1)
    def _():
        retp_ref[...] = r_scr[...]
        for c in range(n_slabs):
            convp_ref[:, _head(c)] = u_scr[c, TAIL_SKIP:TAIL_ROWS, :]


def _prompt_mix(x2d, nw, w_in, cos_p, sin_p, mask, dec_start, dec_end, chunk_dec, r0, tail0,
                conv_w, conv_b, ln_g, ln_b, batch, seq):
    tile = PROMPT_TILE
    nt = seq // tile
    const = lambda shape: pl.BlockSpec(shape, lambda b, j: (0,) * len(shape))
    out_shapes = (
        jax.ShapeDtypeStruct((batch * seq, D_MODEL), BF16),
        jax.ShapeDtypeStruct((batch, RET_HEADS, HEAD_DIM, HEAD_DIM), F32),
        jax.ShapeDtypeStruct((batch, CONV_TAIL, CONV_CH), F32),
    )
    return pl.pallas_call(
        _prompt_mix_kernel,
        grid=(batch, nt),
        in_specs=[
            pl.BlockSpec((tile, D_MODEL), lambda b, j: (b * nt + j, 0)),
            const(nw.shape), const(w_in.shape),
            pl.BlockSpec((tile, HEAD_DIM), lambda b, j: (j, 0)),
            pl.BlockSpec((tile, HEAD_DIM), lambda b, j: (j, 0)),
            const(mask.shape), const(dec_start.shape), const(dec_end.shape), const(chunk_dec.shape),
            const(r0.shape), const(tail0.shape), const(conv_w.shape), const(conv_b.shape),
            const(ln_g.shape), const(ln_b.shape),
        ],
        out_specs=(
            pl.BlockSpec((tile, D_MODEL), lambda b, j: (b * nt + j, 0)),
            pl.BlockSpec((None, RET_HEADS, HEAD_DIM, HEAD_DIM), lambda b, j: (b, 0, 0, 0)),
            pl.BlockSpec((None, CONV_TAIL, CONV_CH), lambda b, j: (b, 0, 0)),
        ),
        out_shape=out_shapes,
        scratch_shapes=[
            pltpu.VMEM((tile, 4 * RET_W), F32),
            pltpu.VMEM((RET_HEADS, HEAD_DIM, HEAD_DIM), F32),
            pltpu.VMEM((CONV_CH // HEAD_DIM, TAIL_ROWS + tile, HEAD_DIM), F32),
            pltpu.VMEM((CONV_CH // HEAD_DIM, tile, HEAD_DIM), F32),
        ],
        compiler_params=pltpu.CompilerParams(
            dimension_semantics=("arbitrary", "arbitrary"), vmem_limit_bytes=56 * MIB),
        name="prompt_mix",
    )(*_hbm(x2d, nw, w_in, cos_p, sin_p, mask, dec_start, dec_end, chunk_dec, r0, tail0,
            conv_w, conv_b, ln_g, ln_b))


def _sample_mix_kernel(qt_ref, kt_ref, v_ref, g_ref, u_ref, gam_ref, sr_ref, sc_ref,
                       cw_ref, cb_ref, lg_ref, lb_ref,
                       mix_ref, so_ref, sco_ref, o_scr):
    nb = v_ref.shape[0]
    for t in range(nb):
        for h in range(RET_HEADS):
            kc = kt_ref[_head(h), t:t + 1]
            qc = qt_ref[_head(h), t:t + 1]
            vr = v_ref[t:t + 1, _head(h)]
            s_new = gam_ref[h:h + 1, :] * sr_ref[t, h] + kc * vr
            so_ref[t, h] = s_new
            o_scr[t:t + 1, _head(h)] = jnp.sum(qc * s_new, axis=0, keepdims=True)
    for h in range(RET_HEADS):
        o = o_scr[:, _head(h)]
        o = o * lax.rsqrt(jnp.mean(o * o, axis=-1, keepdims=True) + EPS)
        mix_ref[:, _head(h)] = o * jax.nn.silu(g_ref[:, _head(h)])
    u = u_ref[...]
    y = u * cw_ref[CONV_TAIL:CONV_WIDTH, :] + cb_ref[...]
    for t in range(CONV_TAIL):
        y = y + sc_ref[t] * cw_ref[t:t + 1, :]
    mix_ref[:, RET_W:] = _layer_norm_swish(y, lg_ref[...], lb_ref[...])
    for t in range(CONV_TAIL - 1):
        sco_ref[t] = sc_ref[t + 1]
    sco_ref[CONV_TAIL - 1] = u


def _out_ffn_kernel(mix_ref, x_ref, wo_ref, wi_ref, wd_ref, npost_ref, nfpre_ref, nfpost_ref, y_ref):
    tile = x_ref.shape[0]
    n_groups = max(1, tile // FFN_GROUP_ROWS)
    groups = [slice(i * tile // n_groups, (i + 1) * tile // n_groups) for i in range(n_groups)]
    a = [_dot(mix_ref[r, :].astype(BF16), wo_ref[...]) for r in groups]
    h1 = [x_ref[r, :] + _rms(a[i], npost_ref[...]) for i, r in enumerate(groups)]
    xf = [_rms(h, nfpre_ref[...]).astype(BF16) for h in h1]
    dn = [None] * n_groups
    for c0 in range(0, D_FF, FFN_CHUNK):
        c1 = min(c0 + FFN_CHUNK, D_FF)
        for i in range(n_groups):
            gate = _dot(xf[i], wi_ref[:, c0:c1])
            up = _dot(xf[i], wi_ref[:, D_FF + c0:D_FF + c1])
            part = _dot((jax.nn.silu(gate) * up).astype(BF16), wd_ref[c0:c1, :])
            dn[i] = part if dn[i] is None else dn[i] + part
    for i, r in enumerate(groups):
        y_ref[r, :] = h1[i] + _rms(dn[i], nfpost_ref[...])


def _prompt_ffn_kernel(mix_ref, x_ref, wo_ref, wi_ref, wd_ref, npost_ref, nfpre_ref, nfpost_ref,
                       qt_ref, kt_ref, vs_ref, gs_ref, us_ref, gam_ref, sr_ref, sc_ref,
                       cw_ref, cb_ref, lg_ref, lb_ref,
                       y_ref, mixs_ref, so_ref, sco_ref, o_scr):
    _sample_mix_kernel(qt_ref, kt_ref, vs_ref, gs_ref, us_ref, gam_ref, sr_ref, sc_ref,
                       cw_ref, cb_ref, lg_ref, lb_ref, mixs_ref, so_ref, sco_ref, o_scr)
    _out_ffn_kernel(mix_ref, x_ref, wo_ref, wi_ref, wd_ref, npost_ref, nfpre_ref, nfpost_ref, y_ref)


def _prompt_ffn(mix, x2d, w_out, w_ffn_in, w_ffn_out, n_post, n_fpre, n_fpost,
                qt3, kt3, v_s, g_s, u_s, gam, state_ret, state_conv, conv_w, conv_b, ln_g, ln_b, tile):
    n = x2d.shape[0]
    steps = n // tile
    n_dec = v_s.shape[0]
    nb = n_dec // steps
    assert nb * steps == n_dec and nb % SUBLANES == 0 and qt3.shape == (steps, RET_W, nb)
    const = lambda shape: pl.BlockSpec(shape, lambda i: (0,) * len(shape),
                                       pipeline_mode=pl.Buffered(1))
    rows = pl.BlockSpec((tile, D_MODEL), lambda i: (i, 0))
    dec_rows = lambda width: pl.BlockSpec((nb, width), lambda i: (i, 0))
    dec_cols = pl.BlockSpec((None, RET_W, nb), lambda i: (i, 0, 0))
    st_spec = pl.BlockSpec((nb, RET_HEADS, HEAD_DIM, HEAD_DIM), lambda i: (i, 0, 0, 0))
    cv_spec = pl.BlockSpec((CONV_TAIL, nb, CONV_CH), lambda i: (0, i, 0))
    return pl.pallas_call(
        _prompt_ffn_kernel,
        grid=(steps,),
        in_specs=[rows, rows, const(w_out.shape), const(w_ffn_in.shape), const(w_ffn_out.shape),
                  const(n_post.shape), const(n_fpre.shape), const(n_fpost.shape),
                  dec_cols, dec_cols, dec_rows(RET_W), dec_rows(RET_W), dec_rows(CONV_CH),
                  const(gam.shape), st_spec, cv_spec,
                  const(conv_w.shape), const(conv_b.shape), const(ln_g.shape), const(ln_b.shape)],
        out_specs=(rows, dec_rows(D_MODEL), st_spec, cv_spec),
        out_shape=(jax.ShapeDtypeStruct((n, D_MODEL), F32),
                   jax.ShapeDtypeStruct((n_dec, D_MODEL), F32),
                   jax.ShapeDtypeStruct(state_ret.shape, F32),
                   jax.ShapeDtypeStruct(state_conv.shape, F32)),
        scratch_shapes=[pltpu.VMEM((nb, RET_W), F32)],
        compiler_params=pltpu.CompilerParams(
            dimension_semantics=("arbitrary",), vmem_limit_bytes=62 * MIB),
        name="out_ffn",
    )(*_hbm(mix, x2d, w_out, w_ffn_in, w_ffn_out, n_post, n_fpre, n_fpost,
            qt3, kt3, v_s, g_s, u_s, gam, state_ret, state_conv, conv_w, conv_b, ln_g, ln_b))


def _decode_ffn(mix, x2d, w_out, w_ffn_in, w_ffn_out, n_post, n_fpre, n_fpost):
    n = x2d.shape[0]
    full = lambda shape: pl.BlockSpec(shape, lambda i: (0,) * len(shape), pipeline_mode=pl.Buffered(1))
    return pl.pallas_call(
        _out_ffn_kernel,
        grid=(1,),
        in_specs=[full(a.shape) for a in (mix, x2d, w_out, w_ffn_in, w_ffn_out, n_post, n_fpre, n_fpost)],
        out_specs=pl.BlockSpec((n, D_MODEL), lambda i: (0, 0)),
        out_shape=jax.ShapeDtypeStruct((n, D_MODEL), F32),
        compiler_params=pltpu.CompilerParams(
            dimension_semantics=("arbitrary",), vmem_limit_bytes=40 * MIB),
        name="decode_ffn",
    )(*_hbm(mix, x2d, w_out, w_ffn_in, w_ffn_out, n_post, n_fpre, n_fpost))


def _rotary_tables(pos):
    half = HEAD_DIM // 2
    inv = 1.0 / (ROPE_BASE ** np.linspace(0.0, 1.0, half))
    ang = np.asarray(pos, np.float64)[:, None] * inv[None, :]
    cos, sin = np.cos(ang), np.sin(ang)
    return (np.concatenate([cos, cos], axis=-1).astype(np.float32),
            np.concatenate([-sin, sin], axis=-1).astype(np.float32))


def _decay_tables():
    lg = np.log(1.0 - 2.0 ** (-5.0 - np.arange(RET_HEADS, dtype=np.float64)))
    idx = np.arange(CHUNK, dtype=np.float64)
    diff = idx[:, None] - idx[None, :]
    mask = np.where(diff[None] >= 0, np.exp(lg[:, None, None] * np.maximum(diff, 0.0)[None]), 0.0)
    wide = lambda col: np.broadcast_to(col[:, :, None], (RET_HEADS, CHUNK, HEAD_DIM))
    dec_end = wide(np.exp(lg[:, None] * (CHUNK - 1 - idx)[None, :]))
    dec_start = wide(np.exp(lg[:, None] * (idx + 1.0)[None, :]))
    chunk_dec = np.broadcast_to(np.exp(lg * CHUNK)[:, None, None], (RET_HEADS, HEAD_DIM, HEAD_DIM))
    gam = np.broadcast_to(np.exp(lg)[:, None], (RET_HEADS, HEAD_DIM))
    return tuple(np.ascontiguousarray(t, np.float32) for t in (mask, dec_start, dec_end, chunk_dec, gam))


def kernel(x_prompt, x_sample, state_ret, state_conv, meta_tokens, norm_mix_pre, norm_mix_post,
           norm_ffn_pre, norm_ffn_post, w_in, conv_w, conv_b, conv_ln_g, conv_ln_b, w_out,
           w_ffn_in, w_ffn_out):
    batch, seq, _ = x_prompt.shape
    n_dec, dec_seq, _ = x_sample.shape
    depth = w_in.shape[0]
    assert depth == 1 and dec_seq == 1 and seq % PROMPT_TILE == 0
    assert (batch * seq) % FFN_TILE == 0

    row = lambda v: v.reshape(1, -1)
    nw_pre, nw_post = row(norm_mix_pre[0]), row(norm_mix_post[0])
    nf_pre, nf_post = row(norm_ffn_pre[0]), row(norm_ffn_post[0])
    cw, cb, lg, lb = conv_w[0], row(conv_b[0]), row(conv_ln_g[0]), row(conv_ln_b[0])

    mask, dec_start, dec_end, chunk_dec, gam = _decay_tables()
    pos_small = np.concatenate([
        np.full((n_dec,), PAST_LEN), np.zeros((CHUNK - N_META,)), np.arange(N_META)])
    cos_s, sin_s = _rotary_tables(pos_small)
    cos_p, sin_p = _rotary_tables(N_META + np.arange(seq))

    xs = jnp.concatenate([
        x_sample.reshape(n_dec, D_MODEL),
        jnp.zeros((CHUNK - N_META, D_MODEL), x_sample.dtype),
        meta_tokens.astype(x_prompt.dtype)], axis=0)
    v_s, g_s, u_s, qt, kt, r0, tail0, w_in_b, w_out_b, w_fi_b, w_fo_b = _small_inproj(
        xs, nw_pre, w_in[0], cos_s, sin_s, dec_end, w_out[0], w_ffn_in[0], w_ffn_out[0], n_dec)

    x2d = x_prompt.reshape(batch * seq, D_MODEL)
    mix_p, ret_p, conv_p = _prompt_mix(
        x2d, nw_pre, w_in_b, cos_p, sin_p, mask, dec_start, dec_end, chunk_dec, r0, tail0,
        cw, cb, lg, lb, batch, seq)

    blocks = (batch * seq) // FFN_TILE
    to_blocks = lambda t: t.reshape(RET_W, blocks, n_dec // blocks).transpose(1, 0, 2)
    y_p, mix_s, ret_s, conv_s = _prompt_ffn(
        mix_p, x2d, w_out_b, w_fi_b, w_fo_b, nw_post, nf_pre, nf_post,
        to_blocks(qt), to_blocks(kt), v_s, g_s, u_s, gam, state_ret[0],
        state_conv[0].transpose(1, 0, 2), cw, cb, lg, lb, FFN_TILE)
    conv_s = conv_s.transpose(1, 0, 2)
    y_s = _decode_ffn(mix_s, x_sample.reshape(n_dec, D_MODEL), w_out_b, w_fi_b, w_fo_b,
                      nw_post, nf_pre, nf_post)

    return (y_p.reshape(batch, seq, D_MODEL), y_s.reshape(n_dec, 1, D_MODEL),
            ret_p[None], conv_p[None], ret_s[None], conv_s[None])
```

```python
import jax
import jax.numpy as jnp
import numpy as np
from jax import lax
from jax.experimental import pallas as pl
from jax.experimental.pallas import tpu as pltpu

F32 = jnp.float32
BF16 = jnp.bfloat16

D_MODEL = 1024
N_META = 16
PAST_LEN = 16384
RET_HEADS = 4
HEAD_DIM = 128
RET_W = RET_HEADS * HEAD_DIM
CONV_CH = D_MODEL - RET_W
CONV_WIDTH = 31
CONV_TAIL = CONV_WIDTH - 1
CHUNK = 128
ROPE_BASE = 10000.0
D_FF = 2816
IN_COLS = 4 * RET_W + 2 * CONV_CH
EPS = 1e-6
K_SCALE = HEAD_DIM ** -0.5

SUBLANES = 8
TAIL_ROWS = 32
TAIL_SKIP = TAIL_ROWS - CONV_TAIL

PROMPT_TILE = 1024
CONV_ROW_STRIDE = 4
FFN_TILE = 1024
FFN_GROUP_ROWS = 256
FFN_CHUNK = 768
WEIGHT_CHUNKS = 8
MIB = 1024 * 1024


def _hbm(*xs):
    return tuple(pltpu.with_memory_space_constraint(jnp.asarray(x), pltpu.HBM) for x in xs)


def _rms(x, w):
    return x * lax.rsqrt(jnp.mean(x * x, axis=-1, keepdims=True) + EPS) * w


def _rot(xh, cos2, sin2):
    return xh * cos2 + pltpu.roll(xh, HEAD_DIM // 2, 1) * sin2


def _head(i):
    return slice(i * HEAD_DIM, (i + 1) * HEAD_DIM)


def _dot(a, b):
    return jnp.dot(a, b, preferred_element_type=F32)


def _dot_nt(a, b):
    return lax.dot_general(a, b, (((1,), (1,)), ((), ())), preferred_element_type=F32)


def _dot_tn(a, b):
    return lax.dot_general(a, b, (((0,), (0,)), ((), ())), preferred_element_type=F32)


def _layer_norm_swish(y, g, b):
    mu = jnp.mean(y, axis=-1, keepdims=True)
    var = jnp.mean(jnp.square(y - mu), axis=-1, keepdims=True)
    yn = (y - mu) * lax.rsqrt(var + EPS) * g + b
    return jax.nn.silu(yn)


def _small_inproj_kernel(xd_ref, meta_ref, nw_ref, wc_ref, cos_ref, sin_ref, de_ref, woc_ref, wic_ref,
                         wdc_ref, vgu_ref, qkt_ref, r0_ref, tail0_ref, wb_ref,
                         wob_ref, wib_ref, wdb_ref, p_scr, w_scr, x_scr):
    i = pl.program_id(0)
    n_chunks = pl.num_programs(0) - 1

    @pl.when(i < n_chunks)
    def _():
        rows = wc_ref.shape[0]
        chunk = wc_ref[...].astype(BF16)
        w_scr[pl.ds(pl.multiple_of(i * rows, rows), rows), :] = chunk
        wb_ref[...] = chunk
        wob_ref[...] = woc_ref[...].astype(BF16)
        wib_ref[...] = wic_ref[...].astype(BF16)
        wdb_ref[...] = wdc_ref[...].astype(BF16)

    @pl.when(i == n_chunks)
    def _():
        n = xd_ref.shape[0]
        n_meta = meta_ref.shape[0]
        x_scr[0:n, :] = xd_ref[...]
        x_scr[n:n + CHUNK - n_meta, :] = jnp.zeros((CHUNK - n_meta, D_MODEL), F32)
        x_scr[n + CHUNK - n_meta:n + CHUNK, :] = meta_ref[...]
        xn = _rms(x_scr[...], nw_ref[...]).astype(BF16)
        p_scr[...] = _dot(xn, w_scr[...])
        dec = slice(0, n)
        met = slice(n, n + CHUNK)
        for h in range(RET_HEADS):
            q = _rot(p_scr[dec, _head(h)], cos_ref[dec, :], sin_ref[dec, :])
            k = _rot(p_scr[dec, _head(RET_HEADS + h)], cos_ref[dec, :], sin_ref[dec, :]) * K_SCALE
            qkt_ref[0, _head(h), :] = q.T
            qkt_ref[1, _head(h), :] = k.T
            km = _rot(p_scr[met, _head(RET_HEADS + h)], cos_ref[met, :], sin_ref[met, :]) * K_SCALE
            vm = p_scr[met, _head(2 * RET_HEADS + h)]
            r0_ref[h] = _dot_tn((km * de_ref[h]).astype(BF16), vm.astype(BF16))
        vgu_ref[:, 0:2 * RET_W] = p_scr[dec, 2 * RET_W:4 * RET_W]
        a = p_scr[:, 4 * RET_W:4 * RET_W + CONV_CH]
        b = p_scr[:, 4 * RET_W + CONV_CH:]
        u = a * jax.nn.sigmoid(b)
        vgu_ref[:, 2 * RET_W:] = u[dec, :]
        tail0_ref[...] = u[n + CHUNK - TAIL_ROWS:n + CHUNK, :]


def _small_inproj(x_dec, meta, nw, w_in, cos_s, sin_s, dec_end, w_out, w_ffn_in, w_ffn_out):
    n_dec = x_dec.shape[0]
    rows = n_dec + CHUNK
    nc = WEIGHT_CHUNKS
    last = nc - 1
    weights = (w_in, w_out, w_ffn_in, w_ffn_out)
    assert all(w.shape[0] % (nc * 16) == 0 for w in weights)
    full = lambda shape: pl.BlockSpec(shape, lambda i: (0,) * len(shape))
    chunk_of = lambda w: pl.BlockSpec((w.shape[0] // nc, w.shape[1]), lambda i: (jnp.minimum(i, last), 0))
    chunk = chunk_of(w_in)
    out_shapes = (
        jax.ShapeDtypeStruct((n_dec, 2 * RET_W + CONV_CH), F32),
        jax.ShapeDtypeStruct((2, RET_W, n_dec), F32),
        jax.ShapeDtypeStruct((RET_HEADS, HEAD_DIM, HEAD_DIM), F32),
        jax.ShapeDtypeStruct((TAIL_ROWS, CONV_CH), F32),
        jax.ShapeDtypeStruct(w_in.shape, BF16),
        jax.ShapeDtypeStruct(w_out.shape, BF16),
        jax.ShapeDtypeStruct(w_ffn_in.shape, BF16),
        jax.ShapeDtypeStruct(w_ffn_out.shape, BF16),
    )
    return pl.pallas_call(
        _small_inproj_kernel,
        grid=(nc + 1,),
        in_specs=[full(x_dec.shape), full(meta.shape), full(nw.shape), chunk, full(cos_s.shape),
                  full(sin_s.shape), full(dec_end.shape)] + [chunk_of(w) for w in weights[1:]],
        out_specs=tuple(full(s.shape) for s in out_shapes[:-4]) + tuple(chunk_of(w) for w in weights),
        out_shape=out_shapes,
        scratch_shapes=[pltpu.VMEM((rows, IN_COLS), F32), pltpu.VMEM(w_in.shape, BF16),
                        pltpu.VMEM((rows, D_MODEL), F32)],
        compiler_params=pltpu.CompilerParams(
            dimension_semantics=("arbitrary",), vmem_limit_bytes=40 * MIB),
        name="small_inproj",
    )(*_hbm(x_dec, meta, nw, w_in, cos_s, sin_s, dec_end, w_out, w_ffn_in, w_ffn_out))


def _prompt_mix_kernel(x_ref, nw_ref, w_ref, cos_ref, sin_ref, mask_ref, ds_ref, de_ref, cd_ref,
                       r0_ref, tail0_ref, cw_ref, cb_ref, lg_ref, lb_ref,
                       mix_ref, retp_ref, convp_ref,
                       p_scr, r_scr, u_scr, y_scr):
    tile = x_ref.shape[0]
    j = pl.program_id(1)

    n_slabs = CONV_CH // HEAD_DIM

    @pl.when(j == 0)
    def _():
        r_scr[...] = r0_ref[...]
        for c in range(n_slabs):
            u_scr[c, 0:TAIL_ROWS, :] = tail0_ref[:, _head(c)]

    xn = _rms(x_ref[...], nw_ref[...]).astype(BF16)

    ab = _dot(xn, w_ref[:, 4 * RET_W:])
    u = ab[:, :CONV_CH] * jax.nn.sigmoid(ab[:, CONV_CH:])
    for c in range(n_slabs):
        u_scr[c, TAIL_ROWS:TAIL_ROWS + tile, :] = u[:, _head(c)]
    p_scr[...] = _dot(xn, w_ref[:, :4 * RET_W])
    block = CONV_ROW_STRIDE * SUBLANES
    for c in range(n_slabs):
        taps = [cw_ref[t:t + 1, _head(c)] for t in range(CONV_WIDTH)]
        bias = cb_ref[:, _head(c)]
        for b0 in range(0, tile, block):
            acc = [None] * CONV_ROW_STRIDE
            for p in range(CONV_ROW_STRIDE + CONV_WIDTH - 1):
                win = u_scr[c, pl.ds(TAIL_SKIP + b0 + p, SUBLANES, stride=CONV_ROW_STRIDE), :]
                for r in range(CONV_ROW_STRIDE):
                    t = p - r
                    if 0 <= t < CONV_WIDTH:
                        term = taps[t] * win
                        acc[r] = term if acc[r] is None else acc[r] + term
            for r in range(CONV_ROW_STRIDE):
                y_scr[c, pl.ds(b0 + r, SUBLANES, stride=CONV_ROW_STRIDE), :] = acc[r] + bias
    ys = [y_scr[c] for c in range(n_slabs)]
    mu = sum(jnp.sum(y, axis=-1, keepdims=True) for y in ys) * (1.0 / CONV_CH)
    dev = [y - mu for y in ys]
    var = sum(jnp.sum(d * d, axis=-1, keepdims=True) for d in dev) * (1.0 / CONV_CH)
    inv = lax.rsqrt(var + EPS)
    for c in range(n_slabs):
        yn = dev[c] * inv * lg_ref[:, _head(c)] + lb_ref[:, _head(c)]
        mix_ref[:, RET_W + c * HEAD_DIM:RET_W + (c + 1) * HEAD_DIM] = jax.nn.silu(yn).astype(BF16)
        u_scr[c, 0:TAIL_ROWS, :] = u_scr[c, tile:tile + TAIL_ROWS, :]

    for c in range(tile // CHUNK):
        rows = slice(c * CHUNK, (c + 1) * CHUNK)
        cos2 = cos_ref[rows, :]
        sin2 = sin_ref[rows, :]
        for h in range(RET_HEADS):
            q = _rot(p_scr[rows, _head(h)], cos2, sin2)
            k = _rot(p_scr[rows, _head(RET_HEADS + h)], cos2, sin2) * K_SCALE
            vb = p_scr[rows, _head(2 * RET_HEADS + h)].astype(BF16)
            gate = p_scr[rows, _head(3 * RET_HEADS + h)]
            qb = q.astype(BF16)
            scores = _dot_nt(qb, k.astype(BF16)) * mask_ref[h]
            r_prev = r_scr[h]
            o = _dot(scores.astype(BF16), vb) + _dot(qb, r_prev.astype(BF16)) * ds_ref[h]
            kv = _dot_tn((k * de_ref[h]).astype(BF16), vb)
            r_scr[h] = cd_ref[h] * r_prev + kv
            o = o * lax.rsqrt(jnp.mean(o * o, axis=-1, keepdims=True) + EPS)
            mix_ref[rows, _head(h)] = (o * jax.nn.silu(gate)).astype(BF16)

    @pl.when(j == pl.num_programs(1) - 1)
    def _():
        retp_ref[...] = r_scr[...]
        for c in range(n_slabs):
            convp_ref[:, _head(c)] = u_scr[c, TAIL_SKIP:TAIL_ROWS, :]


def _prompt_mix(x2d, nw, w_in, cos_p, sin_p, mask, dec_start, dec_end, chunk_dec, r0, tail0,
                conv_w, conv_b, ln_g, ln_b, batch, seq):
    tile = PROMPT_TILE
    nt = seq // tile
    const = lambda shape: pl.BlockSpec(shape, lambda b, j: (0,) * len(shape))
    out_shapes = (
        jax.ShapeDtypeStruct((batch * seq, D_MODEL), BF16),
        jax.ShapeDtypeStruct((batch, RET_HEADS, HEAD_DIM, HEAD_DIM), F32),
        jax.ShapeDtypeStruct((batch, CONV_TAIL, CONV_CH), F32),
    )
    return pl.pallas_call(
        _prompt_mix_kernel,
        grid=(batch, nt),
        in_specs=[
            pl.BlockSpec((tile, D_MODEL), lambda b, j: (b * nt + j, 0)),
            const(nw.shape), const(w_in.shape),
            pl.BlockSpec((tile, HEAD_DIM), lambda b, j: (j, 0)),
            pl.BlockSpec((tile, HEAD_DIM), lambda b, j: (j, 0)),
            const(mask.shape), const(dec_start.shape), const(dec_end.shape), const(chunk_dec.shape),
            const(r0.shape), const(tail0.shape), const(conv_w.shape), const(conv_b.shape),
            const(ln_g.shape), const(ln_b.shape),
        ],
        out_specs=(
            pl.BlockSpec((tile, D_MODEL), lambda b, j: (b * nt + j, 0)),
            pl.BlockSpec((None, RET_HEADS, HEAD_DIM, HEAD_DIM), lambda b, j: (b, 0, 0, 0)),
            pl.BlockSpec((None, CONV_TAIL, CONV_CH), lambda b, j: (b, 0, 0)),
        ),
        out_shape=out_shapes,
        scratch_shapes=[
            pltpu.VMEM((tile, 4 * RET_W), F32),
            pltpu.VMEM((RET_HEADS, HEAD_DIM, HEAD_DIM), F32),
            pltpu.VMEM((CONV_CH // HEAD_DIM, TAIL_ROWS + tile, HEAD_DIM), F32),
            pltpu.VMEM((CONV_CH // HEAD_DIM, tile, HEAD_DIM), F32),
        ],
        compiler_params=pltpu.CompilerParams(
            dimension_semantics=("arbitrary", "arbitrary"), vmem_limit_bytes=56 * MIB),
        name="prompt_mix",
    )(*_hbm(x2d, nw, w_in, cos_p, sin_p, mask, dec_start, dec_end, chunk_dec, r0, tail0,
            conv_w, conv_b, ln_g, ln_b))


def _sample_mix_kernel(qt_ref, kt_ref, vgu_ref, gam_ref, sr_ref, sc_ref,
                       cw_ref, cb_ref, lg_ref, lb_ref,
                       mix_ref, so_ref, sco_ref, o_scr):
    nb = vgu_ref.shape[0]
    for t in range(nb):
        for h in range(RET_HEADS):
            kc = kt_ref[_head(h), t:t + 1]
            qc = qt_ref[_head(h), t:t + 1]
            vr = vgu_ref[t:t + 1, _head(h)]
            s_new = gam_ref[h:h + 1, :] * sr_ref[t, h] + kc * vr
            so_ref[t, h] = s_new
            o_scr[t:t + 1, _head(h)] = jnp.sum(qc * s_new, axis=0, keepdims=True)
    for h in range(RET_HEADS):
        o = o_scr[:, _head(h)]
        o = o * lax.rsqrt(jnp.mean(o * o, axis=-1, keepdims=True) + EPS)
        mix_ref[:, _head(h)] = o * jax.nn.silu(vgu_ref[:, _head(RET_HEADS + h)])
    u = vgu_ref[:, 2 * RET_W:]
    y = u * cw_ref[CONV_TAIL:CONV_WIDTH, :] + cb_ref[...]
    for t in range(CONV_TAIL):
        y = y + sc_ref[t] * cw_ref[t:t + 1, :]
    mix_ref[:, RET_W:] = _layer_norm_swish(y, lg_ref[...], lb_ref[...])
    for t in range(CONV_TAIL - 1):
        sco_ref[t] = sc_ref[t + 1]
    sco_ref[CONV_TAIL - 1] = u


def _out_ffn_kernel(mix_ref, x_ref, wo_ref, wi_ref, wd_ref, npost_ref, nfpre_ref, nfpost_ref, y_ref):
    tile = x_ref.shape[0]
    n_groups = max(1, tile // FFN_GROUP_ROWS)
    groups = [slice(i * tile // n_groups, (i + 1) * tile // n_groups) for i in range(n_groups)]
    a = [_dot(mix_ref[r, :].astype(BF16), wo_ref[...]) for r in groups]
    h1 = [x_ref[r, :] + _rms(a[i], npost_ref[...]) for i, r in enumerate(groups)]
    xf = [_rms(h, nfpre_ref[...]).astype(BF16) for h in h1]
    dn = [None] * n_groups
    for c0 in range(0, D_FF, FFN_CHUNK):
        c1 = min(c0 + FFN_CHUNK, D_FF)
        for i in range(n_groups):
            gate = _dot(xf[i], wi_ref[:, c0:c1])
            up = _dot(xf[i], wi_ref[:, D_FF + c0:D_FF + c1])
            part = _dot((jax.nn.silu(gate) * up).astype(BF16), wd_ref[c0:c1, :])
            dn[i] = part if dn[i] is None else dn[i] + part
    for i, r in enumerate(groups):
        y_ref[r, :] = h1[i] + _rms(dn[i], nfpost_ref[...])


def _prompt_ffn_kernel(mix_ref, x_ref, wo_ref, wi_ref, wd_ref, npost_ref, nfpre_ref, nfpost_ref,
                       qt_ref, kt_ref, vgu_ref, gam_ref, sr_ref, sc_ref,
                       cw_ref, cb_ref, lg_ref, lb_ref,
                       y_ref, mixs_ref, so_ref, sco_ref, o_scr):
    _sample_mix_kernel(qt_ref, kt_ref, vgu_ref, gam_ref, sr_ref, sc_ref,
                       cw_ref, cb_ref, lg_ref, lb_ref, mixs_ref, so_ref, sco_ref, o_scr)
    _out_ffn_kernel(mix_ref, x_ref, wo_ref, wi_ref, wd_ref, npost_ref, nfpre_ref, nfpost_ref, y_ref)


def _prompt_ffn(mix, x2d, w_out, w_ffn_in, w_ffn_out, n_post, n_fpre, n_fpost,
                qkt4, vgu, gam, state_ret, state_conv, conv_w, conv_b, ln_g, ln_b, tile):
    n = x2d.shape[0]
    steps = n // tile
    n_dec = vgu.shape[0]
    nb = n_dec // steps
    assert nb * steps == n_dec and nb % SUBLANES == 0 and qkt4.shape == (2, steps, RET_W, nb)
    const = lambda shape: pl.BlockSpec(shape, lambda i: (0,) * len(shape),
                                       pipeline_mode=pl.Buffered(1))
    rows = pl.BlockSpec((tile, D_MODEL), lambda i: (i, 0))
    dec_rows = lambda width: pl.BlockSpec((nb, width), lambda i: (i, 0))
    dec_cols = lambda which: pl.BlockSpec((None, None, RET_W, nb), lambda i: (which, i, 0, 0))
    st_spec = pl.BlockSpec((nb, RET_HEADS, HEAD_DIM, HEAD_DIM), lambda i: (i, 0, 0, 0))
    cv_spec = pl.BlockSpec((CONV_TAIL, nb, CONV_CH), lambda i: (0, i, 0))
    return pl.pallas_call(
        _prompt_ffn_kernel,
        grid=(steps,),
        in_specs=[rows, rows, const(w_out.shape), const(w_ffn_in.shape), const(w_ffn_out.shape),
                  const(n_post.shape), const(n_fpre.shape), const(n_fpost.shape),
                  dec_cols(0), dec_cols(1), dec_rows(vgu.shape[1]),
                  const(gam.shape), st_spec, cv_spec,
                  const(conv_w.shape), const(conv_b.shape), const(ln_g.shape), const(ln_b.shape)],
        out_specs=(rows, dec_rows(D_MODEL), st_spec, cv_spec),
        out_shape=(jax.ShapeDtypeStruct((n, D_MODEL), F32),
                   jax.ShapeDtypeStruct((n_dec, D_MODEL), F32),
                   jax.ShapeDtypeStruct(state_ret.shape, F32),
                   jax.ShapeDtypeStruct(state_conv.shape, F32)),
        scratch_shapes=[pltpu.VMEM((nb, RET_W), F32)],
        compiler_params=pltpu.CompilerParams(
            dimension_semantics=("arbitrary",), vmem_limit_bytes=62 * MIB),
        name="out_ffn",
    )(*_hbm(mix, x2d, w_out, w_ffn_in, w_ffn_out, n_post, n_fpre, n_fpost,
            qkt4, qkt4, vgu, gam, state_ret, state_conv, conv_w, conv_b, ln_g, ln_b))


def _decode_ffn(mix, x2d, w_out, w_ffn_in, w_ffn_out, n_post, n_fpre, n_fpost):
    n = x2d.shape[0]
    full = lambda shape: pl.BlockSpec(shape, lambda i: (0,) * len(shape), pipeline_mode=pl.Buffered(1))
    return pl.pallas_call(
        _out_ffn_kernel,
        grid=(1,),
        in_specs=[full(a.shape) for a in (mix, x2d, w_out, w_ffn_in, w_ffn_out, n_post, n_fpre, n_fpost)],
        out_specs=pl.BlockSpec((n, D_MODEL), lambda i: (0, 0)),
        out_shape=jax.ShapeDtypeStruct((n, D_MODEL), F32),
        compiler_params=pltpu.CompilerParams(
            dimension_semantics=("arbitrary",), vmem_limit_bytes=40 * MIB),
        name="decode_ffn",
    )(*_hbm(mix, x2d, w_out, w_ffn_in, w_ffn_out, n_post, n_fpre, n_fpost))


def _rotary_tables(pos):
    half = HEAD_DIM // 2
    inv = 1.0 / (ROPE_BASE ** np.linspace(0.0, 1.0, half))
    ang = np.asarray(pos, np.float64)[:, None] * inv[None, :]
    cos, sin = np.cos(ang), np.sin(ang)
    return (np.concatenate([cos, cos], axis=-1).astype(np.float32),
            np.concatenate([-sin, sin], axis=-1).astype(np.float32))


def _decay_tables():
    lg = np.log(1.0 - 2.0 ** (-5.0 - np.arange(RET_HEADS, dtype=np.float64)))
    idx = np.arange(CHUNK, dtype=np.float64)
    diff = idx[:, None] - idx[None, :]
    mask = np.where(diff[None] >= 0, np.exp(lg[:, None, None] * np.maximum(diff, 0.0)[None]), 0.0)
    wide = lambda col: np.broadcast_to(col[:, :, None], (RET_HEADS, CHUNK, HEAD_DIM))
    dec_end = wide(np.exp(lg[:, None] * (CHUNK - 1 - idx)[None, :]))
    dec_start = wide(np.exp(lg[:, None] * (idx + 1.0)[None, :]))
    chunk_dec = np.broadcast_to(np.exp(lg * CHUNK)[:, None, None], (RET_HEADS, HEAD_DIM, HEAD_DIM))
    gam = np.broadcast_to(np.exp(lg)[:, None], (RET_HEADS, HEAD_DIM))
    return tuple(np.ascontiguousarray(t, np.float32) for t in (mask, dec_start, dec_end, chunk_dec, gam))


def kernel(x_prompt, x_sample, state_ret, state_conv, meta_tokens, norm_mix_pre, norm_mix_post,
           norm_ffn_pre, norm_ffn_post, w_in, conv_w, conv_b, conv_ln_g, conv_ln_b, w_out,
           w_ffn_in, w_ffn_out):
    batch, seq, _ = x_prompt.shape
    n_dec, dec_seq, _ = x_sample.shape
    depth = w_in.shape[0]
    assert depth == 1 and dec_seq == 1 and seq % PROMPT_TILE == 0
    assert (batch * seq) % FFN_TILE == 0

    row = lambda v: v.reshape(1, -1)
    nw_pre, nw_post = row(norm_mix_pre[0]), row(norm_mix_post[0])
    nf_pre, nf_post = row(norm_ffn_pre[0]), row(norm_ffn_post[0])
    cw, cb, lg, lb = conv_w[0], row(conv_b[0]), row(conv_ln_g[0]), row(conv_ln_b[0])

    mask, dec_start, dec_end, chunk_dec, gam = _decay_tables()
    pos_small = np.concatenate([
        np.full((n_dec,), PAST_LEN), np.zeros((CHUNK - N_META,)), np.arange(N_META)])
    cos_s, sin_s = _rotary_tables(pos_small)
    cos_p, sin_p = _rotary_tables(N_META + np.arange(seq))

    x_dec = x_sample.reshape(n_dec, D_MODEL)
    vgu, qkt, r0, tail0, w_in_b, w_out_b, w_fi_b, w_fo_b = _small_inproj(
        x_dec, meta_tokens.astype(x_prompt.dtype), nw_pre, w_in[0], cos_s, sin_s, dec_end,
        w_out[0], w_ffn_in[0], w_ffn_out[0])

    x2d = x_prompt.reshape(batch * seq, D_MODEL)
    mix_p, ret_p, conv_p = _prompt_mix(
        x2d, nw_pre, w_in_b, cos_p, sin_p, mask, dec_start, dec_end, chunk_dec, r0, tail0,
        cw, cb, lg, lb, batch, seq)

    blocks = (batch * seq) // FFN_TILE
    qkt4 = qkt.reshape(2, RET_W, blocks, n_dec // blocks).transpose(0, 2, 1, 3)
    y_p, mix_s, ret_s, conv_s = _prompt_ffn(
        mix_p, x2d, w_out_b, w_fi_b, w_fo_b, nw_post, nf_pre, nf_post,
        qkt4, vgu, gam, state_ret[0], state_conv[0].transpose(1, 0, 2), cw, cb, lg, lb, FFN_TILE)
    conv_s = conv_s.transpose(1, 0, 2)
    y_s = _decode_ffn(mix_s, x_dec, w_out_b, w_fi_b, w_fo_b, nw_post, nf_pre, nf_post)

    return (y_p.reshape(batch, seq, D_MODEL), y_s.reshape(n_dec, 1, D_MODEL),
            ret_p[None], conv_p[None], ret_s[None], conv_s[None])
```

```python
import jax
import jax.numpy as jnp
import numpy as np
from jax import lax
from jax.experimental import pallas as pl
from jax.experimental.pallas import tpu as pltpu

F32 = jnp.float32
BF16 = jnp.bfloat16

D_MODEL = 1024
N_META = 16
PAST_LEN = 16384
RET_HEADS = 4
HEAD_DIM = 128
RET_W = RET_HEADS * HEAD_DIM
CONV_CH = D_MODEL - RET_W
CONV_WIDTH = 31
CONV_TAIL = CONV_WIDTH - 1
CHUNK = 128
ROPE_BASE = 10000.0
D_FF = 2816
IN_COLS = 4 * RET_W + 2 * CONV_CH
EPS = 1e-6
K_SCALE = HEAD_DIM ** -0.5

SUBLANES = 8
TAIL_ROWS = 32
TAIL_SKIP = TAIL_ROWS - CONV_TAIL

PROMPT_TILE = 1024
CONV_ROW_STRIDE = 4
FFN_TILE = 1024
FFN_GROUP_ROWS = 256
FFN_CHUNK = 768
WEIGHT_CHUNKS = 8
MIB = 1024 * 1024


def _hbm(*xs):
    return tuple(pltpu.with_memory_space_constraint(jnp.asarray(x), pltpu.HBM) for x in xs)


def _rms(x, w):
    return x * lax.rsqrt(jnp.mean(x * x, axis=-1, keepdims=True) + EPS) * w


def _rot(xh, cos2, sin2):
    return xh * cos2 + pltpu.roll(xh, HEAD_DIM // 2, 1) * sin2


def _head(i):
    return slice(i * HEAD_DIM, (i + 1) * HEAD_DIM)


def _dot(a, b):
    return jnp.dot(a, b, preferred_element_type=F32)


def _dot_nt(a, b):
    return lax.dot_general(a, b, (((1,), (1,)), ((), ())), preferred_element_type=F32)


def _dot_tn(a, b):
    return lax.dot_general(a, b, (((0,), (0,)), ((), ())), preferred_element_type=F32)


def _layer_norm_swish(y, g, b):
    mu = jnp.mean(y, axis=-1, keepdims=True)
    var = jnp.mean(jnp.square(y - mu), axis=-1, keepdims=True)
    yn = (y - mu) * lax.rsqrt(var + EPS) * g + b
    return jax.nn.silu(yn)


def _small_inproj_kernel(xd_ref, meta_ref, nw_ref, wc_ref, cos_ref, sin_ref, de_ref,
                         vgu_ref, qkt_ref, r0_ref, tail0_ref, wb_ref, p_scr, w_scr, x_scr):
    i = pl.program_id(0)
    n_chunks = pl.num_programs(0) - 1

    @pl.when(i < n_chunks)
    def _():
        rows = wc_ref.shape[0]
        chunk = wc_ref[...].astype(BF16)
        w_scr[pl.ds(pl.multiple_of(i * rows, rows), rows), :] = chunk
        wb_ref[...] = chunk

    @pl.when(i == n_chunks)
    def _():
        n = xd_ref.shape[0]
        n_meta = meta_ref.shape[0]
        x_scr[0:n, :] = xd_ref[...]
        x_scr[n:n + CHUNK - n_meta, :] = jnp.zeros((CHUNK - n_meta, D_MODEL), F32)
        x_scr[n + CHUNK - n_meta:n + CHUNK, :] = meta_ref[...]
        xn = _rms(x_scr[...], nw_ref[...]).astype(BF16)
        p_scr[...] = _dot(xn, w_scr[...])
        dec = slice(0, n)
        met = slice(n, n + CHUNK)
        for h in range(RET_HEADS):
            q = _rot(p_scr[dec, _head(h)], cos_ref[dec, :], sin_ref[dec, :])
            k = _rot(p_scr[dec, _head(RET_HEADS + h)], cos_ref[dec, :], sin_ref[dec, :]) * K_SCALE
            qkt_ref[0, _head(h), :] = q.T
            qkt_ref[1, _head(h), :] = k.T
            km = _rot(p_scr[met, _head(RET_HEADS + h)], cos_ref[met, :], sin_ref[met, :]) * K_SCALE
            vm = p_scr[met, _head(2 * RET_HEADS + h)]
            r0_ref[h] = _dot_tn((km * de_ref[h]).astype(BF16), vm.astype(BF16))
        vgu_ref[:, 0:2 * RET_W] = p_scr[dec, 2 * RET_W:4 * RET_W]
        a = p_scr[:, 4 * RET_W:4 * RET_W + CONV_CH]
        b = p_scr[:, 4 * RET_W + CONV_CH:]
        u = a * jax.nn.sigmoid(b)
        vgu_ref[:, 2 * RET_W:] = u[dec, :]
        tail0_ref[...] = u[n + CHUNK - TAIL_ROWS:n + CHUNK, :]


def _small_inproj(x_dec, meta, nw, w_in, cos_s, sin_s, dec_end):
    n_dec = x_dec.shape[0]
    rows = n_dec + CHUNK
    nc = WEIGHT_CHUNKS
    last = nc - 1
    assert w_in.shape[0] % (nc * 16) == 0
    full = lambda shape: pl.BlockSpec(shape, lambda i: (0,) * len(shape))
    chunk = pl.BlockSpec((w_in.shape[0] // nc, IN_COLS), lambda i: (jnp.minimum(i, last), 0))
    out_shapes = (
        jax.ShapeDtypeStruct((n_dec, 2 * RET_W + CONV_CH), F32),
        jax.ShapeDtypeStruct((2, RET_W, n_dec), F32),
        jax.ShapeDtypeStruct((RET_HEADS, HEAD_DIM, HEAD_DIM), F32),
        jax.ShapeDtypeStruct((TAIL_ROWS, CONV_CH), F32),
        jax.ShapeDtypeStruct(w_in.shape, BF16),
    )
    return pl.pallas_call(
        _small_inproj_kernel,
        grid=(nc + 1,),
        in_specs=[full(x_dec.shape), full(meta.shape), full(nw.shape), chunk, full(cos_s.shape),
                  full(sin_s.shape), full(dec_end.shape)],
        out_specs=tuple(full(s.shape) for s in out_shapes[:-1]) + (chunk,),
        out_shape=out_shapes,
        scratch_shapes=[pltpu.VMEM((rows, IN_COLS), F32), pltpu.VMEM(w_in.shape, BF16),
                        pltpu.VMEM((rows, D_MODEL), F32)],
        compiler_params=pltpu.CompilerParams(
            dimension_semantics=("arbitrary",), vmem_limit_bytes=40 * MIB),
        name="small_inproj",
    )(*_hbm(x_dec, meta, nw, w_in, cos_s, sin_s, dec_end))


def _prompt_mix_kernel(x_ref, nw_ref, w_ref, cos_ref, sin_ref, mask_ref, ds_ref, de_ref, cd_ref,
                       r0_ref, tail0_ref, cw_ref, cb_ref, lg_ref, lb_ref, woc_ref, wic_ref, wdc_ref,
                       mix_ref, retp_ref, convp_ref, wob_ref, wib_ref, wdb_ref,
                       p_scr, r_scr, u_scr, y_scr):
    tile = x_ref.shape[0]
    j = pl.program_id(1)

    wob_ref[...] = woc_ref[...].astype(BF16)
    wib_ref[...] = wic_ref[...].astype(BF16)
    wdb_ref[...] = wdc_ref[...].astype(BF16)

    n_slabs = CONV_CH // HEAD_DIM

    @pl.when(j == 0)
    def _():
        r_scr[...] = r0_ref[...]
        for c in range(n_slabs):
            u_scr[c, 0:TAIL_ROWS, :] = tail0_ref[:, _head(c)]

    xn = _rms(x_ref[...], nw_ref[...]).astype(BF16)

    ab = _dot(xn, w_ref[:, 4 * RET_W:])
    u = ab[:, :CONV_CH] * jax.nn.sigmoid(ab[:, CONV_CH:])
    for c in range(n_slabs):
        u_scr[c, TAIL_ROWS:TAIL_ROWS + tile, :] = u[:, _head(c)]
    p_scr[...] = _dot(xn, w_ref[:, :4 * RET_W])
    block = CONV_ROW_STRIDE * SUBLANES
    for c in range(n_slabs):
        taps = [cw_ref[t:t + 1, _head(c)] for t in range(CONV_WIDTH)]
        bias = cb_ref[:, _head(c)]
        for b0 in range(0, tile, block):
            acc = [None] * CONV_ROW_STRIDE
            for p in range(CONV_ROW_STRIDE + CONV_WIDTH - 1):
                win = u_scr[c, pl.ds(TAIL_SKIP + b0 + p, SUBLANES, stride=CONV_ROW_STRIDE), :]
                for r in range(CONV_ROW_STRIDE):
                    t = p - r
                    if 0 <= t < CONV_WIDTH:
                        term = taps[t] * win
                        acc[r] = term if acc[r] is None else acc[r] + term
            for r in range(CONV_ROW_STRIDE):
                y_scr[c, pl.ds(b0 + r, SUBLANES, stride=CONV_ROW_STRIDE), :] = acc[r] + bias
    ys = [y_scr[c] for c in range(n_slabs)]
    mu = sum(jnp.sum(y, axis=-1, keepdims=True) for y in ys) * (1.0 / CONV_CH)
    dev = [y - mu for y in ys]
    var = sum(jnp.sum(d * d, axis=-1, keepdims=True) for d in dev) * (1.0 / CONV_CH)
    inv = lax.rsqrt(var + EPS)
    for c in range(n_slabs):
        yn = dev[c] * inv * lg_ref[:, _head(c)] + lb_ref[:, _head(c)]
        mix_ref[:, RET_W + c * HEAD_DIM:RET_W + (c + 1) * HEAD_DIM] = jax.nn.silu(yn).astype(BF16)
        u_scr[c, 0:TAIL_ROWS, :] = u_scr[c, tile:tile + TAIL_ROWS, :]

    for c in range(tile // CHUNK):
        rows = slice(c * CHUNK, (c + 1) * CHUNK)
        cos2 = cos_ref[rows, :]
        sin2 = sin_ref[rows, :]
        for h in range(RET_HEADS):
            q = _rot(p_scr[rows, _head(h)], cos2, sin2)
            k = _rot(p_scr[rows, _head(RET_HEADS + h)], cos2, sin2) * K_SCALE
            vb = p_scr[rows, _head(2 * RET_HEADS + h)].astype(BF16)
            gate = p_scr[rows, _head(3 * RET_HEADS + h)]
            qb = q.astype(BF16)
            scores = _dot_nt(qb, k.astype(BF16)) * mask_ref[h]
            r_prev = r_scr[h]
            o = _dot(scores.astype(BF16), vb) + _dot(qb, r_prev.astype(BF16)) * ds_ref[h]
            kv = _dot_tn((k * de_ref[h]).astype(BF16), vb)
            r_scr[h] = cd_ref[h] * r_prev + kv
            o = o * lax.rsqrt(jnp.mean(o * o, axis=-1, keepdims=True) + EPS)
            mix_ref[rows, _head(h)] = (o * jax.nn.silu(gate)).astype(BF16)

    @pl.when(j == pl.num_programs(1) - 1)
    def _():
        retp_ref[...] = r_scr[...]
        for c in range(n_slabs):
            convp_ref[:, _head(c)] = u_scr[c, TAIL_SKIP:TAIL_ROWS, :]


def _prompt_mix(x2d, nw, w_in, cos_p, sin_p, mask, dec_start, dec_end, chunk_dec, r0, tail0,
                conv_w, conv_b, ln_g, ln_b, w_out, w_ffn_in, w_ffn_out, batch, seq):
    tile = PROMPT_TILE
    nt = seq // tile
    nc = WEIGHT_CHUNKS
    weights = (w_out, w_ffn_in, w_ffn_out)
    assert batch * nt >= nc and all(w.shape[0] % (nc * 16) == 0 for w in weights)
    const = lambda shape: pl.BlockSpec(shape, lambda b, j: (0,) * len(shape))
    chunk_of = lambda w: pl.BlockSpec((w.shape[0] // nc, w.shape[1]),
                                      lambda b, j: (jnp.minimum(b * nt + j, nc - 1), 0))
    out_shapes = (
        jax.ShapeDtypeStruct((batch * seq, D_MODEL), BF16),
        jax.ShapeDtypeStruct((batch, RET_HEADS, HEAD_DIM, HEAD_DIM), F32),
        jax.ShapeDtypeStruct((batch, CONV_TAIL, CONV_CH), F32),
    ) + tuple(jax.ShapeDtypeStruct(w.shape, BF16) for w in weights)
    return pl.pallas_call(
        _prompt_mix_kernel,
        grid=(batch, nt),
        in_specs=[
            pl.BlockSpec((tile, D_MODEL), lambda b, j: (b * nt + j, 0)),
            const(nw.shape), const(w_in.shape),
            pl.BlockSpec((tile, HEAD_DIM), lambda b, j: (j, 0)),
            pl.BlockSpec((tile, HEAD_DIM), lambda b, j: (j, 0)),
            const(mask.shape), const(dec_start.shape), const(dec_end.shape), const(chunk_dec.shape),
            const(r0.shape), const(tail0.shape), const(conv_w.shape), const(conv_b.shape),
            const(ln_g.shape), const(ln_b.shape),
        ] + [chunk_of(w) for w in weights],
        out_specs=(
            pl.BlockSpec((tile, D_MODEL), lambda b, j: (b * nt + j, 0)),
            pl.BlockSpec((None, RET_HEADS, HEAD_DIM, HEAD_DIM), lambda b, j: (b, 0, 0, 0)),
            pl.BlockSpec((None, CONV_TAIL, CONV_CH), lambda b, j: (b, 0, 0)),
        ) + tuple(chunk_of(w) for w in weights),
        out_shape=out_shapes,
        scratch_shapes=[
            pltpu.VMEM((tile, 4 * RET_W), F32),
            pltpu.VMEM((RET_HEADS, HEAD_DIM, HEAD_DIM), F32),
            pltpu.VMEM((CONV_CH // HEAD_DIM, TAIL_ROWS + tile, HEAD_DIM), F32),
            pltpu.VMEM((CONV_CH // HEAD_DIM, tile, HEAD_DIM), F32),
        ],
        compiler_params=pltpu.CompilerParams(
            dimension_semantics=("arbitrary", "arbitrary"), vmem_limit_bytes=56 * MIB),
        name="prompt_mix",
    )(*_hbm(x2d, nw, w_in, cos_p, sin_p, mask, dec_start, dec_end, chunk_dec, r0, tail0,
            conv_w, conv_b, ln_g, ln_b, w_out, w_ffn_in, w_ffn_out))


def _sample_mix_kernel(qt_ref, kt_ref, vgu_ref, gam_ref, sr_ref, sc_ref,
                       cw_ref, cb_ref, lg_ref, lb_ref,
                       mix_ref, so_ref, sco_ref, o_scr):
    nb = vgu_ref.shape[0]
    for t in range(nb):
        for h in range(RET_HEADS):
            kc = kt_ref[_head(h), t:t + 1]
            qc = qt_ref[_head(h), t:t + 1]
            vr = vgu_ref[t:t + 1, _head(h)]
            s_new = gam_ref[h:h + 1, :] * sr_ref[t, h] + kc * vr
            so_ref[t, h] = s_new
            o_scr[t:t + 1, _head(h)] = jnp.sum(qc * s_new, axis=0, keepdims=True)
    for h in range(RET_HEADS):
        o = o_scr[:, _head(h)]
        o = o * lax.rsqrt(jnp.mean(o * o, axis=-1, keepdims=True) + EPS)
        mix_ref[:, _head(h)] = o * jax.nn.silu(vgu_ref[:, _head(RET_HEADS + h)])
    u = vgu_ref[:, 2 * RET_W:]
    y = u * cw_ref[CONV_TAIL:CONV_WIDTH, :] + cb_ref[...]
    for t in range(CONV_TAIL):
        y = y + sc_ref[t] * cw_ref[t:t + 1, :]
    mix_ref[:, RET_W:] = _layer_norm_swish(y, lg_ref[...], lb_ref[...])
    for t in range(CONV_TAIL - 1):
        sco_ref[t] = sc_ref[t + 1]
    sco_ref[CONV_TAIL - 1] = u


def _out_ffn_kernel(mix_ref, x_ref, wo_ref, wi_ref, wd_ref, npost_ref, nfpre_ref, nfpost_ref, y_ref):
    tile = x_ref.shape[0]
    n_groups = max(1, tile // FFN_GROUP_ROWS)
    groups = [slice(i * tile // n_groups, (i + 1) * tile // n_groups) for i in range(n_groups)]
    a = [_dot(mix_ref[r, :].astype(BF16), wo_ref[...]) for r in groups]
    h1 = [x_ref[r, :] + _rms(a[i], npost_ref[...]) for i, r in enumerate(groups)]
    xf = [_rms(h, nfpre_ref[...]).astype(BF16) for h in h1]
    dn = [None] * n_groups
    for c0 in range(0, D_FF, FFN_CHUNK):
        c1 = min(c0 + FFN_CHUNK, D_FF)
        for i in range(n_groups):
            gate = _dot(xf[i], wi_ref[:, c0:c1])
            up = _dot(xf[i], wi_ref[:, D_FF + c0:D_FF + c1])
            part = _dot((jax.nn.silu(gate) * up).astype(BF16), wd_ref[c0:c1, :])
            dn[i] = part if dn[i] is None else dn[i] + part
    for i, r in enumerate(groups):
        y_ref[r, :] = h1[i] + _rms(dn[i], nfpost_ref[...])


def _prompt_ffn_kernel(mix_ref, x_ref, wo_ref, wi_ref, wd_ref, npost_ref, nfpre_ref, nfpost_ref,
                       qt_ref, kt_ref, vgu_ref, gam_ref, sr_ref, sc_ref,
                       cw_ref, cb_ref, lg_ref, lb_ref,
                       y_ref, mixs_ref, so_ref, sco_ref, o_scr):
    _sample_mix_kernel(qt_ref, kt_ref, vgu_ref, gam_ref, sr_ref, sc_ref,
                       cw_ref, cb_ref, lg_ref, lb_ref, mixs_ref, so_ref, sco_ref, o_scr)
    _out_ffn_kernel(mix_ref, x_ref, wo_ref, wi_ref, wd_ref, npost_ref, nfpre_ref, nfpost_ref, y_ref)


def _prompt_ffn(mix, x2d, w_out, w_ffn_in, w_ffn_out, n_post, n_fpre, n_fpost,
                qkt4, vgu, gam, state_ret, state_conv, conv_w, conv_b, ln_g, ln_b, tile):
    n = x2d.shape[0]
    steps = n // tile
    n_dec = vgu.shape[0]
    nb = n_dec // steps
    assert nb * steps == n_dec and nb % SUBLANES == 0 and qkt4.shape == (2, steps, RET_W, nb)
    const = lambda shape: pl.BlockSpec(shape, lambda i: (0,) * len(shape),
                                       pipeline_mode=pl.Buffered(1))
    rows = pl.BlockSpec((tile, D_MODEL), lambda i: (i, 0))
    dec_rows = lambda width: pl.BlockSpec((nb, width), lambda i: (i, 0))
    dec_cols = lambda which: pl.BlockSpec((None, None, RET_W, nb), lambda i: (which, i, 0, 0))
    st_spec = pl.BlockSpec((nb, RET_HEADS, HEAD_DIM, HEAD_DIM), lambda i: (i, 0, 0, 0))
    cv_spec = pl.BlockSpec((CONV_TAIL, nb, CONV_CH), lambda i: (0, i, 0))
    return pl.pallas_call(
        _prompt_ffn_kernel,
        grid=(steps,),
        in_specs=[rows, rows, const(w_out.shape), const(w_ffn_in.shape), const(w_ffn_out.shape),
                  const(n_post.shape), const(n_fpre.shape), const(n_fpost.shape),
                  dec_cols(0), dec_cols(1), dec_rows(vgu.shape[1]),
                  const(gam.shape), st_spec, cv_spec,
                  const(conv_w.shape), const(conv_b.shape), const(ln_g.shape), const(ln_b.shape)],
        out_specs=(rows, dec_rows(D_MODEL), st_spec, cv_spec),
        out_shape=(jax.ShapeDtypeStruct((n, D_MODEL), F32),
                   jax.ShapeDtypeStruct((n_dec, D_MODEL), F32),
                   jax.ShapeDtypeStruct(state_ret.shape, F32),
                   jax.ShapeDtypeStruct(state_conv.shape, F32)),
        scratch_shapes=[pltpu.VMEM((nb, RET_W), F32)],
        compiler_params=pltpu.CompilerParams(
            dimension_semantics=("arbitrary",), vmem_limit_bytes=62 * MIB),
        name="out_ffn",
    )(*_hbm(mix, x2d, w_out, w_ffn_in, w_ffn_out, n_post, n_fpre, n_fpost,
            qkt4, qkt4, vgu, gam, state_ret, state_conv, conv_w, conv_b, ln_g, ln_b))


def _decode_ffn(mix, x2d, w_out, w_ffn_in, w_ffn_out, n_post, n_fpre, n_fpost):
    n = x2d.shape[0]
    full = lambda shape: pl.BlockSpec(shape, lambda i: (0,) * len(shape), pipeline_mode=pl.Buffered(1))
    return pl.pallas_call(
        _out_ffn_kernel,
        grid=(1,),
        in_specs=[full(a.shape) for a in (mix, x2d, w_out, w_ffn_in, w_ffn_out, n_post, n_fpre, n_fpost)],
        out_specs=pl.BlockSpec((n, D_MODEL), lambda i: (0, 0)),
        out_shape=jax.ShapeDtypeStruct((n, D_MODEL), F32),
        compiler_params=pltpu.CompilerParams(
            dimension_semantics=("arbitrary",), vmem_limit_bytes=40 * MIB),
        name="decode_ffn",
    )(*_hbm(mix, x2d, w_out, w_ffn_in, w_ffn_out, n_post, n_fpre, n_fpost))


def _rotary_tables(pos):
    half = HEAD_DIM // 2
    inv = 1.0 / (ROPE_BASE ** np.linspace(0.0, 1.0, half))
    ang = np.asarray(pos, np.float64)[:, None] * inv[None, :]
    cos, sin = np.cos(ang), np.sin(ang)
    return (np.concatenate([cos, cos], axis=-1).astype(np.float32),
            np.concatenate([-sin, sin], axis=-1).astype(np.float32))


def _decay_tables():
    lg = np.log(1.0 - 2.0 ** (-5.0 - np.arange(RET_HEADS, dtype=np.float64)))
    idx = np.arange(CHUNK, dtype=np.float64)
    diff = idx[:, None] - idx[None, :]
    mask = np.where(diff[None] >= 0, np.exp(lg[:, None, None] * np.maximum(diff, 0.0)[None]), 0.0)
    wide = lambda col: np.broadcast_to(col[:, :, None], (RET_HEADS, CHUNK, HEAD_DIM))
    dec_end = wide(np.exp(lg[:, None] * (CHUNK - 1 - idx)[None, :]))
    dec_start = wide(np.exp(lg[:, None] * (idx + 1.0)[None, :]))
    chunk_dec = np.broadcast_to(np.exp(lg * CHUNK)[:, None, None], (RET_HEADS, HEAD_DIM, HEAD_DIM))
    gam = np.broadcast_to(np.exp(lg)[:, None], (RET_HEADS, HEAD_DIM))
    return tuple(np.ascontiguousarray(t, np.float32) for t in (mask, dec_start, dec_end, chunk_dec, gam))


def kernel(x_prompt, x_sample, state_ret, state_conv, meta_tokens, norm_mix_pre, norm_mix_post,
           norm_ffn_pre, norm_ffn_post, w_in, conv_w, conv_b, conv_ln_g, conv_ln_b, w_out,
           w_ffn_in, w_ffn_out):
    batch, seq, _ = x_prompt.shape
    n_dec, dec_seq, _ = x_sample.shape
    depth = w_in.shape[0]
    assert depth == 1 and dec_seq == 1 and seq % PROMPT_TILE == 0
    assert (batch * seq) % FFN_TILE == 0

    row = lambda v: v.reshape(1, -1)
    nw_pre, nw_post = row(norm_mix_pre[0]), row(norm_mix_post[0])
    nf_pre, nf_post = row(norm_ffn_pre[0]), row(norm_ffn_post[0])
    cw, cb, lg, lb = conv_w[0], row(conv_b[0]), row(conv_ln_g[0]), row(conv_ln_b[0])

    mask, dec_start, dec_end, chunk_dec, gam = _decay_tables()
    pos_small = np.concatenate([
        np.full((n_dec,), PAST_LEN), np.zeros((CHUNK - N_META,)), np.arange(N_META)])
    cos_s, sin_s = _rotary_tables(pos_small)
    cos_p, sin_p = _rotary_tables(N_META + np.arange(seq))

    x_dec = x_sample.reshape(n_dec, D_MODEL)
    vgu, qkt, r0, tail0, w_in_b = _small_inproj(
        x_dec, meta_tokens.astype(x_prompt.dtype), nw_pre, w_in[0], cos_s, sin_s, dec_end)

    x2d = x_prompt.reshape(batch * seq, D_MODEL)
    mix_p, ret_p, conv_p, w_out_b, w_fi_b, w_fo_b = _prompt_mix(
        x2d, nw_pre, w_in_b, cos_p, sin_p, mask, dec_start, dec_end, chunk_dec, r0, tail0,
        cw, cb, lg, lb, w_out[0], w_ffn_in[0], w_ffn_out[0], batch, seq)

    blocks = (batch * seq) // FFN_TILE
    qkt4 = qkt.reshape(2, RET_W, blocks, n_dec // blocks).transpose(0, 2, 1, 3)
    y_p, mix_s, ret_s, conv_s = _prompt_ffn(
        mix_p, x2d, w_out_b, w_fi_b, w_fo_b, nw_post, nf_pre, nf_post,
        qkt4, vgu, gam, state_ret[0], state_conv[0].transpose(1, 0, 2), cw, cb, lg, lb, FFN_TILE)
    conv_s = conv_s.transpose(1, 0, 2)
    y_s = _decode_ffn(mix_s, x_dec, w_out_b, w_fi_b, w_fo_b, nw_post, nf_pre, nf_post)

    return (y_p.reshape(batch, seq, D_MODEL), y_s.reshape(n_dec, 1, D_MODEL),
            ret_p[None], conv_p[None], ret_s[None], conv_s[None])
```

```python
import jax
import jax.numpy as jnp
import numpy as np
from jax import lax
from jax.experimental import pallas as pl
from jax.experimental.pallas import tpu as pltpu

F32 = jnp.float32
BF16 = jnp.bfloat16

D_MODEL = 1024
N_META = 16
PAST_LEN = 16384
RET_HEADS = 4
HEAD_DIM = 128
RET_W = RET_HEADS * HEAD_DIM
CONV_CH = D_MODEL - RET_W
CONV_WIDTH = 31
CONV_TAIL = CONV_WIDTH - 1
CHUNK = 128
ROPE_BASE = 10000.0
D_FF = 2816
IN_COLS = 4 * RET_W + 2 * CONV_CH
EPS = 1e-6
K_SCALE = HEAD_DIM ** -0.5

SUBLANES = 8
TAIL_ROWS = 32
TAIL_SKIP = TAIL_ROWS - CONV_TAIL

PROMPT_TILE = 1024
CONV_ROW_STRIDE = 4
FFN_TILE = 1024
FFN_GROUP_ROWS = 256
FFN_CHUNK = 768
WEIGHT_CHUNKS = 8
MIB = 1024 * 1024


def _hbm(*xs):
    return tuple(pltpu.with_memory_space_constraint(jnp.asarray(x), pltpu.HBM) for x in xs)


def _rms(x, w):
    return x * lax.rsqrt(jnp.mean(x * x, axis=-1, keepdims=True) + EPS) * w


def _rot(xh, cos2, sin2):
    return xh * cos2 + pltpu.roll(xh, HEAD_DIM // 2, 1) * sin2


def _head(i):
    return slice(i * HEAD_DIM, (i + 1) * HEAD_DIM)


def _dot(a, b):
    return jnp.dot(a, b, preferred_element_type=F32)


def _dot_nt(a, b):
    return lax.dot_general(a, b, (((1,), (1,)), ((), ())), preferred_element_type=F32)


def _dot_tn(a, b):
    return lax.dot_general(a, b, (((0,), (0,)), ((), ())), preferred_element_type=F32)


def _layer_norm_swish(y, g, b):
    mu = jnp.mean(y, axis=-1, keepdims=True)
    var = jnp.mean(jnp.square(y - mu), axis=-1, keepdims=True)
    yn = (y - mu) * lax.rsqrt(var + EPS) * g + b
    return jax.nn.silu(yn)


def _small_inproj_kernel(xd_ref, meta_ref, nw_ref, wc_ref, cos_ref, sin_ref, de_ref,
                         vgu_ref, qkt_ref, r0_ref, tail0_ref, wb_ref, p_scr, w_scr, x_scr):
    i = pl.program_id(0)
    n_chunks = pl.num_programs(0) - 1

    @pl.when(i < n_chunks)
    def _():
        rows = wc_ref.shape[0]
        chunk = wc_ref[...].astype(BF16)
        w_scr[pl.ds(pl.multiple_of(i * rows, rows), rows), :] = chunk
        wb_ref[...] = chunk

    @pl.when(i == n_chunks)
    def _():
        n = xd_ref.shape[0]
        n_meta = meta_ref.shape[0]
        x_scr[0:n, :] = xd_ref[...]
        x_scr[n:n + CHUNK - n_meta, :] = jnp.zeros((CHUNK - n_meta, D_MODEL), F32)
        x_scr[n + CHUNK - n_meta:n + CHUNK, :] = meta_ref[...]
        xn = _rms(x_scr[...], nw_ref[...]).astype(BF16)
        p_scr[...] = _dot(xn, w_scr[...])
        dec = slice(0, n)
        met = slice(n, n + CHUNK)
        for h in range(RET_HEADS):
            q = _rot(p_scr[dec, _head(h)], cos_ref[dec, :], sin_ref[dec, :])
            k = _rot(p_scr[dec, _head(RET_HEADS + h)], cos_ref[dec, :], sin_ref[dec, :]) * K_SCALE
            qkt_ref[0, _head(h), :] = q.T
            qkt_ref[1, _head(h), :] = k.T
            km = _rot(p_scr[met, _head(RET_HEADS + h)], cos_ref[met, :], sin_ref[met, :]) * K_SCALE
            vm = p_scr[met, _head(2 * RET_HEADS + h)]
            r0_ref[h] = _dot_tn((km * de_ref[h]).astype(BF16), vm.astype(BF16))
        vgu_ref[:, 0:2 * RET_W] = p_scr[dec, 2 * RET_W:4 * RET_W]
        a = p_scr[:, 4 * RET_W:4 * RET_W + CONV_CH]
        b = p_scr[:, 4 * RET_W + CONV_CH:]
        u = a * jax.nn.sigmoid(b)
        vgu_ref[:, 2 * RET_W:] = u[dec, :]
        tail0_ref[...] = u[n + CHUNK - TAIL_ROWS:n + CHUNK, :]


def _small_inproj(x_dec, meta, nw, w_in, cos_s, sin_s, dec_end):
    n_dec = x_dec.shape[0]
    rows = n_dec + CHUNK
    nc = WEIGHT_CHUNKS
    last = nc - 1
    assert w_in.shape[0] % (nc * 16) == 0
    full = lambda shape: pl.BlockSpec(shape, lambda i: (0,) * len(shape))
    chunk = pl.BlockSpec((w_in.shape[0] // nc, IN_COLS), lambda i: (jnp.minimum(i, last), 0))
    out_shapes = (
        jax.ShapeDtypeStruct((n_dec, 2 * RET_W + CONV_CH), F32),
        jax.ShapeDtypeStruct((2, RET_W, n_dec), F32),
        jax.ShapeDtypeStruct((RET_HEADS, HEAD_DIM, HEAD_DIM), F32),
        jax.ShapeDtypeStruct((TAIL_ROWS, CONV_CH), F32),
        jax.ShapeDtypeStruct(w_in.shape, BF16),
    )
    return pl.pallas_call(
        _small_inproj_kernel,
        grid=(nc + 1,),
        in_specs=[full(x_dec.shape), full(meta.shape), full(nw.shape), chunk, full(cos_s.shape),
                  full(sin_s.shape), full(dec_end.shape)],
        out_specs=tuple(full(s.shape) for s in out_shapes[:-1]) + (chunk,),
        out_shape=out_shapes,
        scratch_shapes=[pltpu.VMEM((rows, IN_COLS), F32), pltpu.VMEM(w_in.shape, BF16),
                        pltpu.VMEM((rows, D_MODEL), F32)],
        compiler_params=pltpu.CompilerParams(
            dimension_semantics=("arbitrary",), vmem_limit_bytes=40 * MIB),
        name="small_inproj",
    )(*_hbm(x_dec, meta, nw, w_in, cos_s, sin_s, dec_end))


def _prompt_mix_kernel(x_ref, nw_ref, w_ref, cos_ref, sin_ref, mask_ref, ds_ref, de_ref, cd_ref,
                       r0_ref, tail0_ref, cw_ref, cb_ref, lg_ref, lb_ref, woc_ref, wic_ref, wdc_ref,
                       mix_ref, retp_ref, convp_ref, wob_ref, wib_ref, wdb_ref,
                       p_scr, r_scr, u_scr, y_scr, w_scr):
    tile = x_ref.shape[0]
    j = pl.program_id(1)

    @pl.when(jnp.logical_and(pl.program_id(0) == 0, j == 0))
    def _():
        w_scr[...] = w_ref[...]

    wob_ref[...] = woc_ref[...].astype(BF16)
    wib_ref[...] = wic_ref[...].astype(BF16)
    wdb_ref[...] = wdc_ref[...].astype(BF16)

    n_slabs = CONV_CH // HEAD_DIM

    @pl.when(j == 0)
    def _():
        r_scr[...] = r0_ref[...]
        for c in range(n_slabs):
            u_scr[c, 0:TAIL_ROWS, :] = tail0_ref[:, _head(c)]

    xn = _rms(x_ref[...], nw_ref[...]).astype(BF16)

    ab = _dot(xn, w_scr[:, 4 * RET_W:])
    u = ab[:, :CONV_CH] * jax.nn.sigmoid(ab[:, CONV_CH:])
    for c in range(n_slabs):
        u_scr[c, TAIL_ROWS:TAIL_ROWS + tile, :] = u[:, _head(c)]
    p_scr[...] = _dot(xn, w_scr[:, :4 * RET_W])
    block = CONV_ROW_STRIDE * SUBLANES
    for c in range(n_slabs):
        taps = [cw_ref[t:t + 1, _head(c)] for t in range(CONV_WIDTH)]
        bias = cb_ref[:, _head(c)]
        for b0 in range(0, tile, block):
            acc = [None] * CONV_ROW_STRIDE
            for p in range(CONV_ROW_STRIDE + CONV_WIDTH - 1):
                win = u_scr[c, pl.ds(TAIL_SKIP + b0 + p, SUBLANES, stride=CONV_ROW_STRIDE), :]
                for r in range(CONV_ROW_STRIDE):
                    t = p - r
                    if 0 <= t < CONV_WIDTH:
                        term = taps[t] * win
                        acc[r] = term if acc[r] is None else acc[r] + term
            for r in range(CONV_ROW_STRIDE):
                y_scr[c, pl.ds(b0 + r, SUBLANES, stride=CONV_ROW_STRIDE), :] = acc[r] + bias
    ys = [y_scr[c] for c in range(n_slabs)]
    mu = sum(jnp.sum(y, axis=-1, keepdims=True) for y in ys) * (1.0 / CONV_CH)
    dev = [y - mu for y in ys]
    var = sum(jnp.sum(d * d, axis=-1, keepdims=True) for d in dev) * (1.0 / CONV_CH)
    inv = lax.rsqrt(var + EPS)
    for c in range(n_slabs):
        yn = dev[c] * inv * lg_ref[:, _head(c)] + lb_ref[:, _head(c)]
        mix_ref[:, RET_W + c * HEAD_DIM:RET_W + (c + 1) * HEAD_DIM] = jax.nn.silu(yn).astype(BF16)
        u_scr[c, 0:TAIL_ROWS, :] = u_scr[c, tile:tile + TAIL_ROWS, :]

    for c in range(tile // CHUNK):
        rows = slice(c * CHUNK, (c + 1) * CHUNK)
        cos2 = cos_ref[rows, :]
        sin2 = sin_ref[rows, :]
        for h in range(RET_HEADS):
            q = _rot(p_scr[rows, _head(h)], cos2, sin2)
            k = _rot(p_scr[rows, _head(RET_HEADS + h)], cos2, sin2) * K_SCALE
            vb = p_scr[rows, _head(2 * RET_HEADS + h)].astype(BF16)
            gate = p_scr[rows, _head(3 * RET_HEADS + h)]
            qb = q.astype(BF16)
            scores = _dot_nt(qb, k.astype(BF16)) * mask_ref[h]
            r_prev = r_scr[h]
            o = _dot(scores.astype(BF16), vb) + _dot(qb, r_prev.astype(BF16)) * ds_ref[h]
            kv = _dot_tn((k * de_ref[h]).astype(BF16), vb)
            r_scr[h] = cd_ref[h] * r_prev + kv
            o = o * lax.rsqrt(jnp.mean(o * o, axis=-1, keepdims=True) + EPS)
            mix_ref[rows, _head(h)] = (o * jax.nn.silu(gate)).astype(BF16)

    @pl.when(j == pl.num_programs(1) - 1)
    def _():
        retp_ref[...] = r_scr[...]
        for c in range(n_slabs):
            convp_ref[:, _head(c)] = u_scr[c, TAIL_SKIP:TAIL_ROWS, :]


def _prompt_mix(x2d, nw, w_in, cos_p, sin_p, mask, dec_start, dec_end, chunk_dec, r0, tail0,
                conv_w, conv_b, ln_g, ln_b, w_out, w_ffn_in, w_ffn_out, batch, seq):
    tile = PROMPT_TILE
    nt = seq // tile
    nc = WEIGHT_CHUNKS
    weights = (w_out, w_ffn_in, w_ffn_out)
    assert batch * nt >= nc and all(w.shape[0] % (nc * 16) == 0 for w in weights)
    const = lambda shape: pl.BlockSpec(shape, lambda b, j: (0,) * len(shape))
    chunk_of = lambda w: pl.BlockSpec((w.shape[0] // nc, w.shape[1]),
                                      lambda b, j: (jnp.minimum(b * nt + j, nc - 1), 0))
    out_shapes = (
        jax.ShapeDtypeStruct((batch * seq, D_MODEL), BF16),
        jax.ShapeDtypeStruct((batch, RET_HEADS, HEAD_DIM, HEAD_DIM), F32),
        jax.ShapeDtypeStruct((batch, CONV_TAIL, CONV_CH), F32),
    ) + tuple(jax.ShapeDtypeStruct(w.shape, BF16) for w in weights)
    return pl.pallas_call(
        _prompt_mix_kernel,
        grid=(batch, nt),
        in_specs=[
            pl.BlockSpec((tile, D_MODEL), lambda b, j: (b * nt + j, 0)),
            const(nw.shape),
            pl.BlockSpec(w_in.shape, lambda b, j: (0, 0), pipeline_mode=pl.Buffered(1)),
            pl.BlockSpec((tile, HEAD_DIM), lambda b, j: (j, 0)),
            pl.BlockSpec((tile, HEAD_DIM), lambda b, j: (j, 0)),
            const(mask.shape), const(dec_start.shape), const(dec_end.shape), const(chunk_dec.shape),
            const(r0.shape), const(tail0.shape), const(conv_w.shape), const(conv_b.shape),
            const(ln_g.shape), const(ln_b.shape),
        ] + [chunk_of(w) for w in weights],
        out_specs=(
            pl.BlockSpec((tile, D_MODEL), lambda b, j: (b * nt + j, 0)),
            pl.BlockSpec((None, RET_HEADS, HEAD_DIM, HEAD_DIM), lambda b, j: (b, 0, 0, 0)),
            pl.BlockSpec((None, CONV_TAIL, CONV_CH), lambda b, j: (b, 0, 0)),
        ) + tuple(chunk_of(w) for w in weights),
        out_shape=out_shapes,
        scratch_shapes=[
            pltpu.VMEM((tile, 4 * RET_W), F32),
            pltpu.VMEM((RET_HEADS, HEAD_DIM, HEAD_DIM), F32),
            pltpu.VMEM((CONV_CH // HEAD_DIM, TAIL_ROWS + tile, HEAD_DIM), F32),
            pltpu.VMEM((CONV_CH // HEAD_DIM, tile, HEAD_DIM), F32),
            pltpu.VMEM((D_MODEL, IN_COLS), BF16),
        ],
        compiler_params=pltpu.CompilerParams(
            dimension_semantics=("arbitrary", "arbitrary"), vmem_limit_bytes=60 * MIB),
        name="prompt_mix",
    )(*_hbm(x2d, nw, w_in, cos_p, sin_p, mask, dec_start, dec_end, chunk_dec, r0, tail0,
            conv_w, conv_b, ln_g, ln_b, w_out, w_ffn_in, w_ffn_out))


def _sample_mix_kernel(qt_ref, kt_ref, vgu_ref, gam_ref, sr_ref, sc_ref,
                       cw_ref, cb_ref, lg_ref, lb_ref,
                       mix_ref, so_ref, sco_ref, o_scr):
    nb = vgu_ref.shape[0]
    for t in range(nb):
        for h in range(RET_HEADS):
            kc = kt_ref[_head(h), t:t + 1]
            qc = qt_ref[_head(h), t:t + 1]
            vr = vgu_ref[t:t + 1, _head(h)]
            s_new = gam_ref[h:h + 1, :] * sr_ref[t, h] + kc * vr
            so_ref[t, h] = s_new
            o_scr[t:t + 1, _head(h)] = jnp.sum(qc * s_new, axis=0, keepdims=True)
    for h in range(RET_HEADS):
        o = o_scr[:, _head(h)]
        o = o * lax.rsqrt(jnp.mean(o * o, axis=-1, keepdims=True) + EPS)
        mix_ref[:, _head(h)] = o * jax.nn.silu(vgu_ref[:, _head(RET_HEADS + h)])
    u = vgu_ref[:, 2 * RET_W:]
    y = u * cw_ref[CONV_TAIL:CONV_WIDTH, :] + cb_ref[...]
    for t in range(CONV_TAIL):
        y = y + sc_ref[t] * cw_ref[t:t + 1, :]
    mix_ref[:, RET_W:] = _layer_norm_swish(y, lg_ref[...], lb_ref[...])
    for t in range(CONV_TAIL - 1):
        sco_ref[t] = sc_ref[t + 1]
    sco_ref[CONV_TAIL - 1] = u


def _out_ffn_kernel(mix_ref, x_ref, wo_ref, wi_ref, wd_ref, npost_ref, nfpre_ref, nfpost_ref, y_ref):
    tile = x_ref.shape[0]
    n_groups = max(1, tile // FFN_GROUP_ROWS)
    groups = [slice(i * tile // n_groups, (i + 1) * tile // n_groups) for i in range(n_groups)]
    a = [_dot(mix_ref[r, :].astype(BF16), wo_ref[...]) for r in groups]
    h1 = [x_ref[r, :] + _rms(a[i], npost_ref[...]) for i, r in enumerate(groups)]
    xf = [_rms(h, nfpre_ref[...]).astype(BF16) for h in h1]
    dn = [None] * n_groups
    for c0 in range(0, D_FF, FFN_CHUNK):
        c1 = min(c0 + FFN_CHUNK, D_FF)
        for i in range(n_groups):
            gate = _dot(xf[i], wi_ref[:, c0:c1])
            up = _dot(xf[i], wi_ref[:, D_FF + c0:D_FF + c1])
            part = _dot((jax.nn.silu(gate) * up).astype(BF16), wd_ref[c0:c1, :])
            dn[i] = part if dn[i] is None else dn[i] + part
    for i, r in enumerate(groups):
        y_ref[r, :] = h1[i] + _rms(dn[i], nfpost_ref[...])


def _prompt_ffn_kernel(mix_ref, x_ref, wo_ref, wi_ref, wd_ref, npost_ref, nfpre_ref, nfpost_ref,
                       qt_ref, kt_ref, vgu_ref, gam_ref, sr_ref, sc_ref,
                       cw_ref, cb_ref, lg_ref, lb_ref,
                       y_ref, mixs_ref, so_ref, sco_ref, o_scr):
    _sample_mix_kernel(qt_ref, kt_ref, vgu_ref, gam_ref, sr_ref, sc_ref,
                       cw_ref, cb_ref, lg_ref, lb_ref, mixs_ref, so_ref, sco_ref, o_scr)
    _out_ffn_kernel(mix_ref, x_ref, wo_ref, wi_ref, wd_ref, npost_ref, nfpre_ref, nfpost_ref, y_ref)


def _prompt_ffn(mix, x2d, w_out, w_ffn_in, w_ffn_out, n_post, n_fpre, n_fpost,
                qkt4, vgu, gam, state_ret, state_conv, conv_w, conv_b, ln_g, ln_b, tile):
    n = x2d.shape[0]
    steps = n // tile
    n_dec = vgu.shape[0]
    nb = n_dec // steps
    assert nb * steps == n_dec and nb % SUBLANES == 0 and qkt4.shape == (2, steps, RET_W, nb)
    const = lambda shape: pl.BlockSpec(shape, lambda i: (0,) * len(shape),
                                       pipeline_mode=pl.Buffered(1))
    rows = pl.BlockSpec((tile, D_MODEL), lambda i: (i, 0))
    dec_rows = lambda width: pl.BlockSpec((nb, width), lambda i: (i, 0))
    dec_cols = lambda which: pl.BlockSpec((None, None, RET_W, nb), lambda i: (which, i, 0, 0))
    st_spec = pl.BlockSpec((nb, RET_HEADS, HEAD_DIM, HEAD_DIM), lambda i: (i, 0, 0, 0))
    cv_spec = pl.BlockSpec((CONV_TAIL, nb, CONV_CH), lambda i: (0, i, 0))
    return pl.pallas_call(
        _prompt_ffn_kernel,
        grid=(steps,),
        in_specs=[rows, rows, const(w_out.shape), const(w_ffn_in.shape), const(w_ffn_out.shape),
                  const(n_post.shape), const(n_fpre.shape), const(n_fpost.shape),
                  dec_cols(0), dec_cols(1), dec_rows(vgu.shape[1]),
                  const(gam.shape), st_spec, cv_spec,
                  const(conv_w.shape), const(conv_b.shape), const(ln_g.shape), const(ln_b.shape)],
        out_specs=(rows, dec_rows(D_MODEL), st_spec, cv_spec),
        out_shape=(jax.ShapeDtypeStruct((n, D_MODEL), F32),
                   jax.ShapeDtypeStruct((n_dec, D_MODEL), F32),
                   jax.ShapeDtypeStruct(state_ret.shape, F32),
                   jax.ShapeDtypeStruct(state_conv.shape, F32)),
        scratch_shapes=[pltpu.VMEM((nb, RET_W), F32)],
        compiler_params=pltpu.CompilerParams(
            dimension_semantics=("arbitrary",), vmem_limit_bytes=62 * MIB),
        name="out_ffn",
    )(*_hbm(mix, x2d, w_out, w_ffn_in, w_ffn_out, n_post, n_fpre, n_fpost,
            qkt4, qkt4, vgu, gam, state_ret, state_conv, conv_w, conv_b, ln_g, ln_b))


def _decode_ffn(mix, x2d, w_out, w_ffn_in, w_ffn_out, n_post, n_fpre, n_fpost):
    n = x2d.shape[0]
    full = lambda shape: pl.BlockSpec(shape, lambda i: (0,) * len(shape), pipeline_mode=pl.Buffered(1))
    return pl.pallas_call(
        _out_ffn_kernel,
        grid=(1,),
        in_specs=[full(a.shape) for a in (mix, x2d, w_out, w_ffn_in, w_ffn_out, n_post, n_fpre, n_fpost)],
        out_specs=pl.BlockSpec((n, D_MODEL), lambda i: (0, 0)),
        out_shape=jax.ShapeDtypeStruct((n, D_MODEL), F32),
        compiler_params=pltpu.CompilerParams(
            dimension_semantics=("arbitrary",), vmem_limit_bytes=40 * MIB),
        name="decode_ffn",
    )(*_hbm(mix, x2d, w_out, w_ffn_in, w_ffn_out, n_post, n_fpre, n_fpost))


def _rotary_tables(pos):
    half = HEAD_DIM // 2
    inv = 1.0 / (ROPE_BASE ** np.linspace(0.0, 1.0, half))
    ang = np.asarray(pos, np.float64)[:, None] * inv[None, :]
    cos, sin = np.cos(ang), np.sin(ang)
    return (np.concatenate([cos, cos], axis=-1).astype(np.float32),
            np.concatenate([-sin, sin], axis=-1).astype(np.float32))


def _decay_tables():
    lg = np.log(1.0 - 2.0 ** (-5.0 - np.arange(RET_HEADS, dtype=np.float64)))
    idx = np.arange(CHUNK, dtype=np.float64)
    diff = idx[:, None] - idx[None, :]
    mask = np.where(diff[None] >= 0, np.exp(lg[:, None, None] * np.maximum(diff, 0.0)[None]), 0.0)
    wide = lambda col: np.broadcast_to(col[:, :, None], (RET_HEADS, CHUNK, HEAD_DIM))
    dec_end = wide(np.exp(lg[:, None] * (CHUNK - 1 - idx)[None, :]))
    dec_start = wide(np.exp(lg[:, None] * (idx + 1.0)[None, :]))
    chunk_dec = np.broadcast_to(np.exp(lg * CHUNK)[:, None, None], (RET_HEADS, HEAD_DIM, HEAD_DIM))
    gam = np.broadcast_to(np.exp(lg)[:, None], (RET_HEADS, HEAD_DIM))
    return tuple(np.ascontiguousarray(t, np.float32) for t in (mask, dec_start, dec_end, chunk_dec, gam))


def kernel(x_prompt, x_sample, state_ret, state_conv, meta_tokens, norm_mix_pre, norm_mix_post,
           norm_ffn_pre, norm_ffn_post, w_in, conv_w, conv_b, conv_ln_g, conv_ln_b, w_out,
           w_ffn_in, w_ffn_out):
    batch, seq, _ = x_prompt.shape
    n_dec, dec_seq, _ = x_sample.shape
    depth = w_in.shape[0]
    assert depth == 1 and dec_seq == 1 and seq % PROMPT_TILE == 0
    assert (batch * seq) % FFN_TILE == 0

    row = lambda v: v.reshape(1, -1)
    nw_pre, nw_post = row(norm_mix_pre[0]), row(norm_mix_post[0])
    nf_pre, nf_post = row(norm_ffn_pre[0]), row(norm_ffn_post[0])
    cw, cb, lg, lb = conv_w[0], row(conv_b[0]), row(conv_ln_g[0]), row(conv_ln_b[0])

    mask, dec_start, dec_end, chunk_dec, gam = _decay_tables()
    pos_small = np.concatenate([
        np.full((n_dec,), PAST_LEN), np.zeros((CHUNK - N_META,)), np.arange(N_META)])
    cos_s, sin_s = _rotary_tables(pos_small)
    cos_p, sin_p = _rotary_tables(N_META + np.arange(seq))

    x_dec = x_sample.reshape(n_dec, D_MODEL)
    vgu, qkt, r0, tail0, w_in_b = _small_inproj(
        x_dec, meta_tokens.astype(x_prompt.dtype), nw_pre, w_in[0], cos_s, sin_s, dec_end)

    x2d = x_prompt.reshape(batch * seq, D_MODEL)
    mix_p, ret_p, conv_p, w_out_b, w_fi_b, w_fo_b = _prompt_mix(
        x2d, nw_pre, w_in_b, cos_p, sin_p, mask, dec_start, dec_end, chunk_dec, r0, tail0,
        cw, cb, lg, lb, w_out[0], w_ffn_in[0], w_ffn_out[0], batch, seq)

    blocks = (batch * seq) // FFN_TILE
    qkt4 = qkt.reshape(2, RET_W, blocks, n_dec // blocks).transpose(0, 2, 1, 3)
    y_p, mix_s, ret_s, conv_s = _prompt_ffn(
        mix_p, x2d, w_out_b, w_fi_b, w_fo_b, nw_post, nf_pre, nf_post,
        qkt4, vgu, gam, state_ret[0], state_conv[0].transpose(1, 0, 2), cw, cb, lg, lb, FFN_TILE)
    conv_s = conv_s.transpose(1, 0, 2)
    y_s = _decode_ffn(mix_s, x_dec, w_out_b, w_fi_b, w_fo_b, nw_post, nf_pre, nf_post)

    return (y_p.reshape(batch, seq, D_MODEL), y_s.reshape(n_dec, 1, D_MODEL),
            ret_p[None], conv_p[None], ret_s[None], conv_s[None])
```

```python
import jax
import jax.numpy as jnp
import numpy as np
from jax import lax
from jax.experimental import pallas as pl
from jax.experimental.pallas import tpu as pltpu

F32 = jnp.float32
BF16 = jnp.bfloat16

D_MODEL = 1024
N_META = 16
PAST_LEN = 16384
RET_HEADS = 4
HEAD_DIM = 128
RET_W = RET_HEADS * HEAD_DIM
CONV_CH = D_MODEL - RET_W
CONV_WIDTH = 31
CONV_TAIL = CONV_WIDTH - 1
CHUNK = 128
ROPE_BASE = 10000.0
D_FF = 2816
IN_COLS = 4 * RET_W + 2 * CONV_CH
EPS = 1e-6
K_SCALE = HEAD_DIM ** -0.5

SUBLANES = 8
TAIL_ROWS = 32
TAIL_SKIP = TAIL_ROWS - CONV_TAIL

PROMPT_TILE = 1024
CONV_ROW_STRIDE = 4
FFN_TILE = 1024
FFN_GROUP_ROWS = 256
FFN_CHUNK = 768
WEIGHT_CHUNKS = 8
MIB = 1024 * 1024


def _hbm(*xs):
    return tuple(pltpu.with_memory_space_constraint(jnp.asarray(x), pltpu.HBM) for x in xs)


def _rms(x, w):
    return x * lax.rsqrt(jnp.mean(x * x, axis=-1, keepdims=True) + EPS) * w


def _rot(xh, cos2, sin2):
    return xh * cos2 + pltpu.roll(xh, HEAD_DIM // 2, 1) * sin2


def _head(i):
    return slice(i * HEAD_DIM, (i + 1) * HEAD_DIM)


def _dot(a, b):
    return jnp.dot(a, b, preferred_element_type=F32)


def _dot_nt(a, b):
    return lax.dot_general(a, b, (((1,), (1,)), ((), ())), preferred_element_type=F32)


def _dot_tn(a, b):
    return lax.dot_general(a, b, (((0,), (0,)), ((), ())), preferred_element_type=F32)


def _layer_norm_swish(y, g, b):
    mu = jnp.mean(y, axis=-1, keepdims=True)
    var = jnp.mean(jnp.square(y - mu), axis=-1, keepdims=True)
    yn = (y - mu) * lax.rsqrt(var + EPS) * g + b
    return jax.nn.silu(yn)


def _small_inproj_kernel(xd_ref, meta_ref, nw_ref, wc_ref, cos_ref, sin_ref, de_ref,
                         vgu_ref, qkt_ref, r0_ref, tail0_ref, wb_ref, p_scr, w_scr, x_scr):
    i = pl.program_id(0)
    n_chunks = pl.num_programs(0) - 1

    @pl.when(i < n_chunks)
    def _():
        rows = wc_ref.shape[0]
        chunk = wc_ref[...].astype(BF16)
        w_scr[pl.ds(pl.multiple_of(i * rows, rows), rows), :] = chunk
        wb_ref[...] = chunk

    @pl.when(i == n_chunks)
    def _():
        n = xd_ref.shape[0]
        n_meta = meta_ref.shape[0]
        x_scr[0:n, :] = xd_ref[...]
        x_scr[n:n + CHUNK - n_meta, :] = jnp.zeros((CHUNK - n_meta, D_MODEL), F32)
        x_scr[n + CHUNK - n_meta:n + CHUNK, :] = meta_ref[...]
        xn = _rms(x_scr[...], nw_ref[...]).astype(BF16)
        p_scr[...] = _dot(xn, w_scr[...])
        dec = slice(0, n)
        met = slice(n, n + CHUNK)
        for h in range(RET_HEADS):
            q = _rot(p_scr[dec, _head(h)], cos_ref[dec, :], sin_ref[dec, :])
            k = _rot(p_scr[dec, _head(RET_HEADS + h)], cos_ref[dec, :], sin_ref[dec, :]) * K_SCALE
            qt, kt = q.T, k.T
            nb = qkt_ref.shape[3]
            for blk in range(qkt_ref.shape[1]):
                qkt_ref[0, blk, _head(h), :] = qt[:, blk * nb:(blk + 1) * nb]
                qkt_ref[1, blk, _head(h), :] = kt[:, blk * nb:(blk + 1) * nb]
            km = _rot(p_scr[met, _head(RET_HEADS + h)], cos_ref[met, :], sin_ref[met, :]) * K_SCALE
            vm = p_scr[met, _head(2 * RET_HEADS + h)]
            r0_ref[h] = _dot_tn((km * de_ref[h]).astype(BF16), vm.astype(BF16))
        vgu_ref[:, 0:2 * RET_W] = p_scr[dec, 2 * RET_W:4 * RET_W]
        a = p_scr[:, 4 * RET_W:4 * RET_W + CONV_CH]
        b = p_scr[:, 4 * RET_W + CONV_CH:]
        u = a * jax.nn.sigmoid(b)
        vgu_ref[:, 2 * RET_W:] = u[dec, :]
        tail0_ref[...] = u[n + CHUNK - TAIL_ROWS:n + CHUNK, :]


def _small_inproj(x_dec, meta, nw, w_in, cos_s, sin_s, dec_end, dec_blocks):
    n_dec = x_dec.shape[0]
    assert n_dec % dec_blocks == 0
    rows = n_dec + CHUNK
    nc = WEIGHT_CHUNKS
    last = nc - 1
    assert w_in.shape[0] % (nc * 16) == 0
    full = lambda shape: pl.BlockSpec(shape, lambda i: (0,) * len(shape))
    chunk = pl.BlockSpec((w_in.shape[0] // nc, IN_COLS), lambda i: (jnp.minimum(i, last), 0))
    out_shapes = (
        jax.ShapeDtypeStruct((n_dec, 2 * RET_W + CONV_CH), F32),
        jax.ShapeDtypeStruct((2, dec_blocks, RET_W, n_dec // dec_blocks), F32),
        jax.ShapeDtypeStruct((RET_HEADS, HEAD_DIM, HEAD_DIM), F32),
        jax.ShapeDtypeStruct((TAIL_ROWS, CONV_CH), F32),
        jax.ShapeDtypeStruct(w_in.shape, BF16),
    )
    return pl.pallas_call(
        _small_inproj_kernel,
        grid=(nc + 1,),
        in_specs=[full(x_dec.shape), full(meta.shape), full(nw.shape), chunk, full(cos_s.shape),
                  full(sin_s.shape), full(dec_end.shape)],
        out_specs=tuple(full(s.shape) for s in out_shapes[:-1]) + (chunk,),
        out_shape=out_shapes,
        scratch_shapes=[pltpu.VMEM((rows, IN_COLS), F32), pltpu.VMEM(w_in.shape, BF16),
                        pltpu.VMEM((rows, D_MODEL), F32)],
        compiler_params=pltpu.CompilerParams(
            dimension_semantics=("arbitrary",), vmem_limit_bytes=40 * MIB),
        name="small_inproj",
    )(*_hbm(x_dec, meta, nw, w_in, cos_s, sin_s, dec_end))


def _prompt_mix_kernel(x_ref, nw_ref, w_ref, cos_ref, sin_ref, mask_ref, ds_ref, de_ref, cd_ref,
                       r0_ref, tail0_ref, cw_ref, cb_ref, lg_ref, lb_ref, woc_ref, wic_ref, wdc_ref,
                       mix_ref, retp_ref, convp_ref, wob_ref, wib_ref, wdb_ref,
                       p_scr, r_scr, u_scr, y_scr):
    tile = x_ref.shape[0]
    j = pl.program_id(1)

    wob_ref[...] = woc_ref[...].astype(BF16)
    wib_ref[...] = wic_ref[...].astype(BF16)
    wdb_ref[...] = wdc_ref[...].astype(BF16)

    n_slabs = CONV_CH // HEAD_DIM

    @pl.when(j == 0)
    def _():
        r_scr[...] = r0_ref[...]
        for c in range(n_slabs):
            u_scr[c, 0:TAIL_ROWS, :] = tail0_ref[:, _head(c)]

    xn = _rms(x_ref[...], nw_ref[...]).astype(BF16)

    ab = _dot(xn, w_ref[:, 4 * RET_W:])
    u = ab[:, :CONV_CH] * jax.nn.sigmoid(ab[:, CONV_CH:])
    for c in range(n_slabs):
        u_scr[c, TAIL_ROWS:TAIL_ROWS + tile, :] = u[:, _head(c)]
    p_scr[...] = _dot(xn, w_ref[:, :4 * RET_W])
    block = CONV_ROW_STRIDE * SUBLANES
    for c in range(n_slabs):
        taps = [cw_ref[t:t + 1, _head(c)] for t in range(CONV_WIDTH)]
        bias = cb_ref[:, _head(c)]
        for b0 in range(0, tile, block):
            acc = [None] * CONV_ROW_STRIDE
            for p in range(CONV_ROW_STRIDE + CONV_WIDTH - 1):
                win = u_scr[c, pl.ds(TAIL_SKIP + b0 + p, SUBLANES, stride=CONV_ROW_STRIDE), :]
                for r in range(CONV_ROW_STRIDE):
                    t = p - r
                    if 0 <= t < CONV_WIDTH:
                        term = taps[t] * win
                        acc[r] = term if acc[r] is None else acc[r] + term
            for r in range(CONV_ROW_STRIDE):
                y_scr[c, pl.ds(b0 + r, SUBLANES, stride=CONV_ROW_STRIDE), :] = acc[r] + bias
    ys = [y_scr[c] for c in range(n_slabs)]
    mu = sum(jnp.sum(y, axis=-1, keepdims=True) for y in ys) * (1.0 / CONV_CH)
    dev = [y - mu for y in ys]
    var = sum(jnp.sum(d * d, axis=-1, keepdims=True) for d in dev) * (1.0 / CONV_CH)
    inv = lax.rsqrt(var + EPS)
    for c in range(n_slabs):
        yn = dev[c] * inv * lg_ref[:, _head(c)] + lb_ref[:, _head(c)]
        mix_ref[:, RET_W + c * HEAD_DIM:RET_W + (c + 1) * HEAD_DIM] = jax.nn.silu(yn).astype(BF16)
        u_scr[c, 0:TAIL_ROWS, :] = u_scr[c, tile:tile + TAIL_ROWS, :]

    for c in range(tile // CHUNK):
        rows = slice(c * CHUNK, (c + 1) * CHUNK)
        cos2 = cos_ref[rows, :]
        sin2 = sin_ref[rows, :]
        for h in range(RET_HEADS):
            q = _rot(p_scr[rows, _head(h)], cos2, sin2)
            k = _rot(p_scr[rows, _head(RET_HEADS + h)], cos2, sin2) * K_SCALE
            vb = p_scr[rows, _head(2 * RET_HEADS + h)].astype(BF16)
            gate = p_scr[rows, _head(3 * RET_HEADS + h)]
            qb = q.astype(BF16)
            scores = _dot_nt(qb, k.astype(BF16)) * mask_ref[h]
            r_prev = r_scr[h]
            o = _dot(scores.astype(BF16), vb) + _dot(qb, r_prev.astype(BF16)) * ds_ref[h]
            kv = _dot_tn((k * de_ref[h]).astype(BF16), vb)
            r_scr[h] = cd_ref[h] * r_prev + kv
            o = o * lax.rsqrt(jnp.mean(o * o, axis=-1, keepdims=True) + EPS)
            mix_ref[rows, _head(h)] = (o * jax.nn.silu(gate)).astype(BF16)

    @pl.when(j == pl.num_programs(1) - 1)
    def _():
        retp_ref[...] = r_scr[...]
        for c in range(n_slabs):
            convp_ref[:, _head(c)] = u_scr[c, TAIL_SKIP:TAIL_ROWS, :]


def _prompt_mix(x2d, nw, w_in, cos_p, sin_p, mask, dec_start, dec_end, chunk_dec, r0, tail0,
                conv_w, conv_b, ln_g, ln_b, w_out, w_ffn_in, w_ffn_out, batch, seq):
    tile = PROMPT_TILE
    nt = seq // tile
    nc = WEIGHT_CHUNKS
    weights = (w_out, w_ffn_in, w_ffn_out)
    assert batch * nt >= nc and all(w.shape[0] % (nc * 16) == 0 for w in weights)
    const = lambda shape: pl.BlockSpec(shape, lambda b, j: (0,) * len(shape))
    chunk_of = lambda w: pl.BlockSpec((w.shape[0] // nc, w.shape[1]),
                                      lambda b, j: (jnp.minimum(b * nt + j, nc - 1), 0))
    out_shapes = (
        jax.ShapeDtypeStruct((batch * seq, D_MODEL), BF16),
        jax.ShapeDtypeStruct((batch, RET_HEADS, HEAD_DIM, HEAD_DIM), F32),
        jax.ShapeDtypeStruct((batch, CONV_TAIL, CONV_CH), F32),
    ) + tuple(jax.ShapeDtypeStruct(w.shape, BF16) for w in weights)
    return pl.pallas_call(
        _prompt_mix_kernel,
        grid=(batch, nt),
        in_specs=[
            pl.BlockSpec((tile, D_MODEL), lambda b, j: (b * nt + j, 0)),
            const(nw.shape), const(w_in.shape),
            pl.BlockSpec((tile, HEAD_DIM), lambda b, j: (j, 0)),
            pl.BlockSpec((tile, HEAD_DIM), lambda b, j: (j, 0)),
            const(mask.shape), const(dec_start.shape), const(dec_end.shape), const(chunk_dec.shape),
            const(r0.shape), const(tail0.shape), const(conv_w.shape), const(conv_b.shape),
            const(ln_g.shape), const(ln_b.shape),
        ] + [chunk_of(w) for w in weights],
        out_specs=(
            pl.BlockSpec((tile, D_MODEL), lambda b, j: (b * nt + j, 0)),
            pl.BlockSpec((None, RET_HEADS, HEAD_DIM, HEAD_DIM), lambda b, j: (b, 0, 0, 0)),
            pl.BlockSpec((None, CONV_TAIL, CONV_CH), lambda b, j: (b, 0, 0)),
        ) + tuple(chunk_of(w) for w in weights),
        out_shape=out_shapes,
        scratch_shapes=[
            pltpu.VMEM((tile, 4 * RET_W), F32),
            pltpu.VMEM((RET_HEADS, HEAD_DIM, HEAD_DIM), F32),
            pltpu.VMEM((CONV_CH // HEAD_DIM, TAIL_ROWS + tile, HEAD_DIM), F32),
            pltpu.VMEM((CONV_CH // HEAD_DIM, tile, HEAD_DIM), F32),
        ],
        compiler_params=pltpu.CompilerParams(
            dimension_semantics=("arbitrary", "arbitrary"), vmem_limit_bytes=56 * MIB),
        name="prompt_mix",
    )(*_hbm(x2d, nw, w_in, cos_p, sin_p, mask, dec_start, dec_end, chunk_dec, r0, tail0,
            conv_w, conv_b, ln_g, ln_b, w_out, w_ffn_in, w_ffn_out))


def _sample_mix_kernel(qt_ref, kt_ref, vgu_ref, gam_ref, sr_ref, sc_ref,
                       cw_ref, cb_ref, lg_ref, lb_ref,
                       mix_ref, so_ref, sco_ref, o_scr):
    nb = vgu_ref.shape[0]
    for t in range(nb):
        for h in range(RET_HEADS):
            kc = kt_ref[_head(h), t:t + 1]
            qc = qt_ref[_head(h), t:t + 1]
            vr = vgu_ref[t:t + 1, _head(h)]
            s_new = gam_ref[h:h + 1, :] * sr_ref[t, h] + kc * vr
            so_ref[t, h] = s_new
            o_scr[t:t + 1, _head(h)] = jnp.sum(qc * s_new, axis=0, keepdims=True)
    for h in range(RET_HEADS):
        o = o_scr[:, _head(h)]
        o = o * lax.rsqrt(jnp.mean(o * o, axis=-1, keepdims=True) + EPS)
        mix_ref[:, _head(h)] = o * jax.nn.silu(vgu_ref[:, _head(RET_HEADS + h)])
    u = vgu_ref[:, 2 * RET_W:]
    y = u * cw_ref[CONV_TAIL:CONV_WIDTH, :] + cb_ref[...]
    for t in range(CONV_TAIL):
        y = y + sc_ref[t] * cw_ref[t:t + 1, :]
    mix_ref[:, RET_W:] = _layer_norm_swish(y, lg_ref[...], lb_ref[...])
    for t in range(CONV_TAIL - 1):
        sco_ref[t] = sc_ref[t + 1]
    sco_ref[CONV_TAIL - 1] = u


def _out_ffn_kernel(mix_ref, x_ref, wo_ref, wi_ref, wd_ref, npost_ref, nfpre_ref, nfpost_ref, y_ref):
    tile = x_ref.shape[0]
    n_groups = max(1, tile // FFN_GROUP_ROWS)
    groups = [slice(i * tile // n_groups, (i + 1) * tile // n_groups) for i in range(n_groups)]
    a = [_dot(mix_ref[r, :].astype(BF16), wo_ref[...]) for r in groups]
    h1 = [x_ref[r, :] + _rms(a[i], npost_ref[...]) for i, r in enumerate(groups)]
    xf = [_rms(h, nfpre_ref[...]).astype(BF16) for h in h1]
    dn = [None] * n_groups
    for c0 in range(0, D_FF, FFN_CHUNK):
        c1 = min(c0 + FFN_CHUNK, D_FF)
        for i in range(n_groups):
            gate = _dot(xf[i], wi_ref[:, c0:c1])
            up = _dot(xf[i], wi_ref[:, D_FF + c0:D_FF + c1])
            part = _dot((jax.nn.silu(gate) * up).astype(BF16), wd_ref[c0:c1, :])
            dn[i] = part if dn[i] is None else dn[i] + part
    for i, r in enumerate(groups):
        y_ref[r, :] = h1[i] + _rms(dn[i], nfpost_ref[...])


def _prompt_ffn_kernel(mix_ref, x_ref, wo_ref, wi_ref, wd_ref, npost_ref, nfpre_ref, nfpost_ref,
                       qt_ref, kt_ref, vgu_ref, gam_ref, sr_ref, sc_ref,
                       cw_ref, cb_ref, lg_ref, lb_ref,
                       y_ref, mixs_ref, so_ref, sco_ref, o_scr):
    _sample_mix_kernel(qt_ref, kt_ref, vgu_ref, gam_ref, sr_ref, sc_ref,
                       cw_ref, cb_ref, lg_ref, lb_ref, mixs_ref, so_ref, sco_ref, o_scr)
    _out_ffn_kernel(mix_ref, x_ref, wo_ref, wi_ref, wd_ref, npost_ref, nfpre_ref, nfpost_ref, y_ref)


def _prompt_ffn(mix, x2d, w_out, w_ffn_in, w_ffn_out, n_post, n_fpre, n_fpost,
                qkt4, vgu, gam, state_ret, state_conv, conv_w, conv_b, ln_g, ln_b, tile):
    n = x2d.shape[0]
    steps = n // tile
    n_dec = vgu.shape[0]
    nb = n_dec // steps
    assert nb * steps == n_dec and nb % SUBLANES == 0 and qkt4.shape == (2, steps, RET_W, nb)
    const = lambda shape: pl.BlockSpec(shape, lambda i: (0,) * len(shape),
                                       pipeline_mode=pl.Buffered(1))
    rows = pl.BlockSpec((tile, D_MODEL), lambda i: (i, 0))
    dec_rows = lambda width: pl.BlockSpec((nb, width), lambda i: (i, 0))
    dec_cols = lambda which: pl.BlockSpec((None, None, RET_W, nb), lambda i: (which, i, 0, 0))
    st_spec = pl.BlockSpec((nb, RET_HEADS, HEAD_DIM, HEAD_DIM), lambda i: (i, 0, 0, 0))
    cv_spec = pl.BlockSpec((CONV_TAIL, nb, CONV_CH), lambda i: (0, i, 0))
    return pl.pallas_call(
        _prompt_ffn_kernel,
        grid=(steps,),
        in_specs=[rows, rows, const(w_out.shape), const(w_ffn_in.shape), const(w_ffn_out.shape),
                  const(n_post.shape), const(n_fpre.shape), const(n_fpost.shape),
                  dec_cols(0), dec_cols(1), dec_rows(vgu.shape[1]),
                  const(gam.shape), st_spec, cv_spec,
                  const(conv_w.shape), const(conv_b.shape), const(ln_g.shape), const(ln_b.shape)],
        out_specs=(rows, dec_rows(D_MODEL), st_spec, cv_spec),
        out_shape=(jax.ShapeDtypeStruct((n, D_MODEL), F32),
                   jax.ShapeDtypeStruct((n_dec, D_MODEL), F32),
                   jax.ShapeDtypeStruct(state_ret.shape, F32),
                   jax.ShapeDtypeStruct(state_conv.shape, F32)),
        scratch_shapes=[pltpu.VMEM((nb, RET_W), F32)],
        compiler_params=pltpu.CompilerParams(
            dimension_semantics=("arbitrary",), vmem_limit_bytes=62 * MIB),
        name="out_ffn",
    )(*_hbm(mix, x2d, w_out, w_ffn_in, w_ffn_out, n_post, n_fpre, n_fpost,
            qkt4, qkt4, vgu, gam, state_ret, state_conv, conv_w, conv_b, ln_g, ln_b))


def _decode_ffn(mix, x2d, w_out, w_ffn_in, w_ffn_out, n_post, n_fpre, n_fpost):
    n = x2d.shape[0]
    full = lambda shape: pl.BlockSpec(shape, lambda i: (0,) * len(shape), pipeline_mode=pl.Buffered(1))
    return pl.pallas_call(
        _out_ffn_kernel,
        grid=(1,),
        in_specs=[full(a.shape) for a in (mix, x2d, w_out, w_ffn_in, w_ffn_out, n_post, n_fpre, n_fpost)],
        out_specs=pl.BlockSpec((n, D_MODEL), lambda i: (0, 0)),
        out_shape=jax.ShapeDtypeStruct((n, D_MODEL), F32),
        compiler_params=pltpu.CompilerParams(
            dimension_semantics=("arbitrary",), vmem_limit_bytes=40 * MIB),
        name="decode_ffn",
    )(*_hbm(mix, x2d, w_out, w_ffn_in, w_ffn_out, n_post, n_fpre, n_fpost))


def _rotary_tables(pos):
    half = HEAD_DIM // 2
    inv = 1.0 / (ROPE_BASE ** np.linspace(0.0, 1.0, half))
    ang = np.asarray(pos, np.float64)[:, None] * inv[None, :]
    cos, sin = np.cos(ang), np.sin(ang)
    return (np.concatenate([cos, cos], axis=-1).astype(np.float32),
            np.concatenate([-sin, sin], axis=-1).astype(np.float32))


def _decay_tables():
    lg = np.log(1.0 - 2.0 ** (-5.0 - np.arange(RET_HEADS, dtype=np.float64)))
    idx = np.arange(CHUNK, dtype=np.float64)
    diff = idx[:, None] - idx[None, :]
    mask = np.where(diff[None] >= 0, np.exp(lg[:, None, None] * np.maximum(diff, 0.0)[None]), 0.0)
    wide = lambda col: np.broadcast_to(col[:, :, None], (RET_HEADS, CHUNK, HEAD_DIM))
    dec_end = wide(np.exp(lg[:, None] * (CHUNK - 1 - idx)[None, :]))
    dec_start = wide(np.exp(lg[:, None] * (idx + 1.0)[None, :]))
    chunk_dec = np.broadcast_to(np.exp(lg * CHUNK)[:, None, None], (RET_HEADS, HEAD_DIM, HEAD_DIM))
    gam = np.broadcast_to(np.exp(lg)[:, None], (RET_HEADS, HEAD_DIM))
    return tuple(np.ascontiguousarray(t, np.float32) for t in (mask, dec_start, dec_end, chunk_dec, gam))


def kernel(x_prompt, x_sample, state_ret, state_conv, meta_tokens, norm_mix_pre, norm_mix_post,
           norm_ffn_pre, norm_ffn_post, w_in, conv_w, conv_b, conv_ln_g, conv_ln_b, w_out,
           w_ffn_in, w_ffn_out):
    batch, seq, _ = x_prompt.shape
    n_dec, dec_seq, _ = x_sample.shape
    depth = w_in.shape[0]
    assert depth == 1 and dec_seq == 1 and seq % PROMPT_TILE == 0
    assert (batch * seq) % FFN_TILE == 0

    row = lambda v: v.reshape(1, -1)
    nw_pre, nw_post = row(norm_mix_pre[0]), row(norm_mix_post[0])
    nf_pre, nf_post = row(norm_ffn_pre[0]), row(norm_ffn_post[0])
    cw, cb, lg, lb = conv_w[0], row(conv_b[0]), row(conv_ln_g[0]), row(conv_ln_b[0])

    mask, dec_start, dec_end, chunk_dec, gam = _decay_tables()
    pos_small = np.concatenate([
        np.full((n_dec,), PAST_LEN), np.zeros((CHUNK - N_META,)), np.arange(N_META)])
    cos_s, sin_s = _rotary_tables(pos_small)
    cos_p, sin_p = _rotary_tables(N_META + np.arange(seq))

    x_dec = x_sample.reshape(n_dec, D_MODEL)
    vgu, qkt4, r0, tail0, w_in_b = _small_inproj(
        x_dec, meta_tokens.astype(x_prompt.dtype), nw_pre, w_in[0], cos_s, sin_s, dec_end,
        (batch * seq) // FFN_TILE)

    x2d = x_prompt.reshape(batch * seq, D_MODEL)
    mix_p, ret_p, conv_p, w_out_b, w_fi_b, w_fo_b = _prompt_mix(
        x2d, nw_pre, w_in_b, cos_p, sin_p, mask, dec_start, dec_end, chunk_dec, r0, tail0,
        cw, cb, lg, lb, w_out[0], w_ffn_in[0], w_ffn_out[0], batch, seq)

    y_p, mix_s, ret_s, conv_s = _prompt_ffn(
        mix_p, x2d, w_out_b, w_fi_b, w_fo_b, nw_post, nf_pre, nf_post,
        qkt4, vgu, gam, state_ret[0], state_conv[0].transpose(1, 0, 2), cw, cb, lg, lb, FFN_TILE)
    conv_s = conv_s.transpose(1, 0, 2)
    y_s = _decode_ffn(mix_s, x_dec, w_out_b, w_fi_b, w_fo_b, nw_post, nf_pre, nf_post)

    return (y_p.reshape(batch, seq, D_MODEL), y_s.reshape(n_dec, 1, D_MODEL),
            ret_p[None], conv_p[None], ret_s[None], conv_s[None])
```

```python
import jax
import jax.numpy as jnp
import numpy as np
from jax import lax
from jax.experimental import pallas as pl
from jax.experimental.pallas import tpu as pltpu

F32 = jnp.float32
BF16 = jnp.bfloat16

D_MODEL = 1024
N_META = 16
PAST_LEN = 16384
RET_HEADS = 4
HEAD_DIM = 128
RET_W = RET_HEADS * HEAD_DIM
CONV_CH = D_MODEL - RET_W
CONV_WIDTH = 31
CONV_TAIL = CONV_WIDTH - 1
CHUNK = 128
ROPE_BASE = 10000.0
D_FF = 2816
IN_COLS = 4 * RET_W + 2 * CONV_CH
EPS = 1e-6
K_SCALE = HEAD_DIM ** -0.5

SUBLANES = 8
TAIL_ROWS = 32
TAIL_SKIP = TAIL_ROWS - CONV_TAIL

PROMPT_TILE = 1024
CONV_ROW_STRIDE = 4
FFN_TILE = 1024
FFN_GROUP_ROWS = 256
FFN_CHUNK = 768
WEIGHT_CHUNKS = 8
INPROJ_WEIGHT_CHUNKS = 4

MIB = 1024 * 1024
V7X_VMEM_BYTES = 64 * MIB
SMALL_INPROJ_VMEM = 40 * MIB
PROMPT_MIX_VMEM = 56 * MIB
OUT_FFN_VMEM = V7X_VMEM_BYTES - 2 * MIB
DECODE_FFN_VMEM = 40 * MIB


def _hbm(*xs):
    return tuple(pltpu.with_memory_space_constraint(jnp.asarray(x), pltpu.HBM) for x in xs)


def _rms(x, w):
    return x * lax.rsqrt(jnp.mean(x * x, axis=-1, keepdims=True) + EPS) * w


def _rot(xh, cos2, sin2):
    return xh * cos2 + pltpu.roll(xh, HEAD_DIM // 2, 1) * sin2


def _head(i):
    return slice(i * HEAD_DIM, (i + 1) * HEAD_DIM)


def _dot(a, b):
    return jnp.dot(a, b, preferred_element_type=F32)


def _dot_nt(a, b):
    return lax.dot_general(a, b, (((1,), (1,)), ((), ())), preferred_element_type=F32)


def _dot_tn(a, b):
    return lax.dot_general(a, b, (((0,), (0,)), ((), ())), preferred_element_type=F32)


def _layer_norm_swish(y, g, b):
    mu = jnp.mean(y, axis=-1, keepdims=True)
    var = jnp.mean(jnp.square(y - mu), axis=-1, keepdims=True)
    yn = (y - mu) * lax.rsqrt(var + EPS) * g + b
    return jax.nn.silu(yn)


def _small_inproj_kernel(xd_ref, meta_ref, nw_ref, wc_ref, cos_ref, sin_ref, de_ref,
                         vgu_ref, qkt_ref, r0_ref, tail0_ref, wb_ref, p_scr, w_scr, x_scr):
    i = pl.program_id(0)
    n_chunks = pl.num_programs(0) - 1

    @pl.when(i < n_chunks)
    def _():
        rows = wc_ref.shape[0]
        chunk = wc_ref[...].astype(BF16)
        w_scr[pl.ds(pl.multiple_of(i * rows, rows), rows), :] = chunk
        wb_ref[...] = chunk

    @pl.when(i == n_chunks)
    def _():
        n = xd_ref.shape[0]
        n_meta = meta_ref.shape[0]
        x_scr[0:n, :] = xd_ref[...]
        x_scr[n:n + CHUNK - n_meta, :] = jnp.zeros((CHUNK - n_meta, D_MODEL), F32)
        x_scr[n + CHUNK - n_meta:n + CHUNK, :] = meta_ref[...]
        xn = _rms(x_scr[...], nw_ref[...]).astype(BF16)
        p_scr[...] = _dot(xn, w_scr[...])
        dec = slice(0, n)
        met = slice(n, n + CHUNK)
        for h in range(RET_HEADS):
            q = _rot(p_scr[dec, _head(h)], cos_ref[dec, :], sin_ref[dec, :])
            k = _rot(p_scr[dec, _head(RET_HEADS + h)], cos_ref[dec, :], sin_ref[dec, :]) * K_SCALE
            qkt_ref[0, _head(h), :] = q.T
            qkt_ref[1, _head(h), :] = k.T
            km = _rot(p_scr[met, _head(RET_HEADS + h)], cos_ref[met, :], sin_ref[met, :]) * K_SCALE
            vm = p_scr[met, _head(2 * RET_HEADS + h)]
            r0_ref[h] = _dot_tn((km * de_ref[h]).astype(BF16), vm.astype(BF16))
        vgu_ref[:, 0:2 * RET_W] = p_scr[dec, 2 * RET_W:4 * RET_W]
        a = p_scr[:, 4 * RET_W:4 * RET_W + CONV_CH]
        b = p_scr[:, 4 * RET_W + CONV_CH:]
        u = a * jax.nn.sigmoid(b)
        vgu_ref[:, 2 * RET_W:] = u[dec, :]
        tail0_ref[...] = u[n + CHUNK - TAIL_ROWS:n + CHUNK, :]


def _small_inproj(x_dec, meta, nw, w_in, cos_s, sin_s, dec_end):
    n_dec = x_dec.shape[0]
    rows = n_dec + CHUNK
    nc = INPROJ_WEIGHT_CHUNKS
    last = nc - 1
    assert w_in.shape[0] % (nc * 16) == 0
    full = lambda shape: pl.BlockSpec(shape, lambda i: (0,) * len(shape))
    chunk = pl.BlockSpec((w_in.shape[0] // nc, IN_COLS), lambda i: (jnp.minimum(i, last), 0))
    out_shapes = (
        jax.ShapeDtypeStruct((n_dec, 2 * RET_W + CONV_CH), F32),
        jax.ShapeDtypeStruct((2, RET_W, n_dec), F32),
        jax.ShapeDtypeStruct((RET_HEADS, HEAD_DIM, HEAD_DIM), F32),
        jax.ShapeDtypeStruct((TAIL_ROWS, CONV_CH), F32),
        jax.ShapeDtypeStruct(w_in.shape, BF16),
    )
    return pl.pallas_call(
        _small_inproj_kernel,
        grid=(nc + 1,),
        in_specs=[full(x_dec.shape), full(meta.shape), full(nw.shape), chunk, full(cos_s.shape),
                  full(sin_s.shape), full(dec_end.shape)],
        out_specs=tuple(full(s.shape) for s in out_shapes[:-1]) + (chunk,),
        out_shape=out_shapes,
        scratch_shapes=[pltpu.VMEM((rows, IN_COLS), F32), pltpu.VMEM(w_in.shape, BF16),
                        pltpu.VMEM((rows, D_MODEL), F32)],
        compiler_params=pltpu.CompilerParams(
            dimension_semantics=("arbitrary",), vmem_limit_bytes=SMALL_INPROJ_VMEM),
        name="small_inproj",
    )(*_hbm(x_dec, meta, nw, w_in, cos_s, sin_s, dec_end))


def _prompt_mix_kernel(x_ref, nw_ref, w_ref, cos_ref, sin_ref, mask_ref, ds_ref, de_ref, cd_ref,
                       r0_ref, tail0_ref, cw_ref, cb_ref, lg_ref, lb_ref, woc_ref, wic_ref, wdc_ref,
                       mix_ref, retp_ref, convp_ref, wob_ref, wib_ref, wdb_ref,
                       p_scr, r_scr, u_scr, y_scr):
    tile = x_ref.shape[0]
    j = pl.program_id(1)

    wob_ref[...] = woc_ref[...].astype(BF16)
    wib_ref[...] = wic_ref[...].astype(BF16)
    wdb_ref[...] = wdc_ref[...].astype(BF16)

    n_slabs = CONV_CH // HEAD_DIM

    @pl.when(j == 0)
    def _():
        r_scr[...] = r0_ref[...]
        for c in range(n_slabs):
            u_scr[c, 0:TAIL_ROWS, :] = tail0_ref[:, _head(c)]

    xn = _rms(x_ref[...], nw_ref[...]).astype(BF16)

    ab = _dot(xn, w_ref[:, 4 * RET_W:])
    u = ab[:, :CONV_CH] * jax.nn.sigmoid(ab[:, CONV_CH:])
    for c in range(n_slabs):
        u_scr[c, TAIL_ROWS:TAIL_ROWS + tile, :] = u[:, _head(c)]
    p_scr[...] = _dot(xn, w_ref[:, :4 * RET_W])
    block = CONV_ROW_STRIDE * SUBLANES
    for c in range(n_slabs):
        taps = [cw_ref[t:t + 1, _head(c)] for t in range(CONV_WIDTH)]
        bias = cb_ref[:, _head(c)]
        for b0 in range(0, tile, block):
            acc = [None] * CONV_ROW_STRIDE
            for p in range(CONV_ROW_STRIDE + CONV_WIDTH - 1):
                win = u_scr[c, pl.ds(TAIL_SKIP + b0 + p, SUBLANES, stride=CONV_ROW_STRIDE), :]
                for r in range(CONV_ROW_STRIDE):
                    t = p - r
                    if 0 <= t < CONV_WIDTH:
                        term = taps[t] * win
                        acc[r] = term if acc[r] is None else acc[r] + term
            for r in range(CONV_ROW_STRIDE):
                y_scr[c, pl.ds(b0 + r, SUBLANES, stride=CONV_ROW_STRIDE), :] = acc[r] + bias
    ys = [y_scr[c] for c in range(n_slabs)]
    mu = sum(jnp.sum(y, axis=-1, keepdims=True) for y in ys) * (1.0 / CONV_CH)
    dev = [y - mu for y in ys]
    var = sum(jnp.sum(d * d, axis=-1, keepdims=True) for d in dev) * (1.0 / CONV_CH)
    inv = lax.rsqrt(var + EPS)
    for c in range(n_slabs):
        yn = dev[c] * inv * lg_ref[:, _head(c)] + lb_ref[:, _head(c)]
        mix_ref[:, RET_W + c * HEAD_DIM:RET_W + (c + 1) * HEAD_DIM] = jax.nn.silu(yn).astype(BF16)
        u_scr[c, 0:TAIL_ROWS, :] = u_scr[c, tile:tile + TAIL_ROWS, :]

    for c in range(tile // CHUNK):
        rows = slice(c * CHUNK, (c + 1) * CHUNK)
        cos2 = cos_ref[rows, :]
        sin2 = sin_ref[rows, :]
        for h in range(RET_HEADS):
            q = _rot(p_scr[rows, _head(h)], cos2, sin2)
            k = _rot(p_scr[rows, _head(RET_HEADS + h)], cos2, sin2) * K_SCALE
            vb = p_scr[rows, _head(2 * RET_HEADS + h)].astype(BF16)
            gate = p_scr[rows, _head(3 * RET_HEADS + h)]
            qb = q.astype(BF16)
            scores = _dot_nt(qb, k.astype(BF16)) * mask_ref[h]
            r_prev = r_scr[h]
            o = _dot(scores.astype(BF16), vb) + _dot(qb, r_prev.astype(BF16)) * ds_ref[h]
            kv = _dot_tn((k * de_ref[h]).astype(BF16), vb)
            r_scr[h] = cd_ref[h] * r_prev + kv
            o = o * lax.rsqrt(jnp.mean(o * o, axis=-1, keepdims=True) + EPS)
            mix_ref[rows, _head(h)] = (o * jax.nn.silu(gate)).astype(BF16)

    @pl.when(j == pl.num_programs(1) - 1)
    def _():
        retp_ref[...] = r_scr[...]
        for c in range(n_slabs):
            convp_ref[:, _head(c)] = u_scr[c, TAIL_SKIP:TAIL_ROWS, :]


def _prompt_mix(x2d, nw, w_in, cos_p, sin_p, mask, dec_start, dec_end, chunk_dec, r0, tail0,
                conv_w, conv_b, ln_g, ln_b, w_out, w_ffn_in, w_ffn_out, batch, seq):
    tile = PROMPT_TILE
    nt = seq // tile
    nc = WEIGHT_CHUNKS
    weights = (w_out, w_ffn_in, w_ffn_out)
    assert batch * nt >= nc and all(w.shape[0] % (nc * 16) == 0 for w in weights)
    const = lambda shape: pl.BlockSpec(shape, lambda b, j: (0,) * len(shape))
    chunk_of = lambda w: pl.BlockSpec((w.shape[0] // nc, w.shape[1]),
                                      lambda b, j: (jnp.minimum(b * nt + j, nc - 1), 0))
    out_shapes = (
        jax.ShapeDtypeStruct((batch * seq, D_MODEL), BF16),
        jax.ShapeDtypeStruct((batch, RET_HEADS, HEAD_DIM, HEAD_DIM), F32),
        jax.ShapeDtypeStruct((batch, CONV_TAIL, CONV_CH), F32),
    ) + tuple(jax.ShapeDtypeStruct(w.shape, BF16) for w in weights)
    return pl.pallas_call(
        _prompt_mix_kernel,
        grid=(batch, nt),
        in_specs=[
            pl.BlockSpec((tile, D_MODEL), lambda b, j: (b * nt + j, 0)),
            const(nw.shape), const(w_in.shape),
            pl.BlockSpec((tile, HEAD_DIM), lambda b, j: (j, 0)),
            pl.BlockSpec((tile, HEAD_DIM), lambda b, j: (j, 0)),
            const(mask.shape), const(dec_start.shape), const(dec_end.shape), const(chunk_dec.shape),
            const(r0.shape), const(tail0.shape), const(conv_w.shape), const(conv_b.shape),
            const(ln_g.shape), const(ln_b.shape),
        ] + [chunk_of(w) for w in weights],
        out_specs=(
            pl.BlockSpec((tile, D_MODEL), lambda b, j: (b * nt + j, 0)),
            pl.BlockSpec((None, RET_HEADS, HEAD_DIM, HEAD_DIM), lambda b, j: (b, 0, 0, 0)),
            pl.BlockSpec((None, CONV_TAIL, CONV_CH), lambda b, j: (b, 0, 0)),
        ) + tuple(chunk_of(w) for w in weights),
        out_shape=out_shapes,
        scratch_shapes=[
            pltpu.VMEM((tile, 4 * RET_W), F32),
            pltpu.VMEM((RET_HEADS, HEAD_DIM, HEAD_DIM), F32),
            pltpu.VMEM((CONV_CH // HEAD_DIM, TAIL_ROWS + tile, HEAD_DIM), F32),
            pltpu.VMEM((CONV_CH // HEAD_DIM, tile, HEAD_DIM), F32),
        ],
        compiler_params=pltpu.CompilerParams(
            dimension_semantics=("arbitrary", "arbitrary"), vmem_limit_bytes=PROMPT_MIX_VMEM),
        name="prompt_mix",
    )(*_hbm(x2d, nw, w_in, cos_p, sin_p, mask, dec_start, dec_end, chunk_dec, r0, tail0,
            conv_w, conv_b, ln_g, ln_b, w_out, w_ffn_in, w_ffn_out))


def _sample_mix_kernel(qt_ref, kt_ref, vgu_ref, gam_ref, sr_ref, sc_ref,
                       cw_ref, cb_ref, lg_ref, lb_ref,
                       mix_ref, so_ref, sco_ref, o_scr):
    nb = vgu_ref.shape[0]
    for t in range(nb):
        for h in range(RET_HEADS):
            kc = kt_ref[_head(h), t:t + 1]
            qc = qt_ref[_head(h), t:t + 1]
            vr = vgu_ref[t:t + 1, _head(h)]
            s_new = gam_ref[h:h + 1, :] * sr_ref[t, h] + kc * vr
            so_ref[t, h] = s_new
            o_scr[t:t + 1, _head(h)] = jnp.sum(qc * s_new, axis=0, keepdims=True)
    for h in range(RET_HEADS):
        o = o_scr[:, _head(h)]
        o = o * lax.rsqrt(jnp.mean(o * o, axis=-1, keepdims=True) + EPS)
        mix_ref[:, _head(h)] = o * jax.nn.silu(vgu_ref[:, _head(RET_HEADS + h)])
    u = vgu_ref[:, 2 * RET_W:]
    y = u * cw_ref[CONV_TAIL:CONV_WIDTH, :] + cb_ref[...]
    for t in range(CONV_TAIL):
        y = y + sc_ref[t] * cw_ref[t:t + 1, :]
    mix_ref[:, RET_W:] = _layer_norm_swish(y, lg_ref[...], lb_ref[...])
    for t in range(CONV_TAIL - 1):
        sco_ref[t] = sc_ref[t + 1]
    sco_ref[CONV_TAIL - 1] = u


def _out_ffn_kernel(mix_ref, x_ref, wo_ref, wi_ref, wd_ref, npost_ref, nfpre_ref, nfpost_ref, y_ref):
    tile = x_ref.shape[0]
    n_groups = max(1, tile // FFN_GROUP_ROWS)
    groups = [slice(i * tile // n_groups, (i + 1) * tile // n_groups) for i in range(n_groups)]
    a = [_dot(mix_ref[r, :].astype(BF16), wo_ref[...]) for r in groups]
    h1 = [x_ref[r, :] + _rms(a[i], npost_ref[...]) for i, r in enumerate(groups)]
    xf = [_rms(h, nfpre_ref[...]).astype(BF16) for h in h1]
    dn = [None] * n_groups
    for c0 in range(0, D_FF, FFN_CHUNK):
        c1 = min(c0 + FFN_CHUNK, D_FF)
        for i in range(n_groups):
            gate = _dot(xf[i], wi_ref[:, c0:c1])
            up = _dot(xf[i], wi_ref[:, D_FF + c0:D_FF + c1])
            part = _dot((jax.nn.silu(gate) * up).astype(BF16), wd_ref[c0:c1, :])
            dn[i] = part if dn[i] is None else dn[i] + part
    for i, r in enumerate(groups):
        y_ref[r, :] = h1[i] + _rms(dn[i], nfpost_ref[...])


def _prompt_ffn_kernel(mix_ref, x_ref, wo_ref, wi_ref, wd_ref, npost_ref, nfpre_ref, nfpost_ref,
                       qt_ref, kt_ref, vgu_ref, gam_ref, sr_ref, sc_ref,
                       cw_ref, cb_ref, lg_ref, lb_ref,
                       y_ref, mixs_ref, so_ref, sco_ref, o_scr):
    _sample_mix_kernel(qt_ref, kt_ref, vgu_ref, gam_ref, sr_ref, sc_ref,
                       cw_ref, cb_ref, lg_ref, lb_ref, mixs_ref, so_ref, sco_ref, o_scr)
    _out_ffn_kernel(mix_ref, x_ref, wo_ref, wi_ref, wd_ref, npost_ref, nfpre_ref, nfpost_ref, y_ref)


def _prompt_ffn(mix, x2d, w_out, w_ffn_in, w_ffn_out, n_post, n_fpre, n_fpost,
                qkt4, vgu, gam, state_ret, state_conv, conv_w, conv_b, ln_g, ln_b, tile):
    n = x2d.shape[0]
    steps = n // tile
    n_dec = vgu.shape[0]
    nb = n_dec // steps
    assert nb * steps == n_dec and nb % SUBLANES == 0 and qkt4.shape == (2, steps, RET_W, nb)
    const = lambda shape: pl.BlockSpec(shape, lambda i: (0,) * len(shape),
                                       pipeline_mode=pl.Buffered(1))
    rows = pl.BlockSpec((tile, D_MODEL), lambda i: (i, 0))
    dec_rows = lambda width: pl.BlockSpec((nb, width), lambda i: (i, 0))
    dec_cols = lambda which: pl.BlockSpec((None, None, RET_W, nb), lambda i: (which, i, 0, 0))
    st_spec = pl.BlockSpec((nb, RET_HEADS, HEAD_DIM, HEAD_DIM), lambda i: (i, 0, 0, 0))
    cv_spec = pl.BlockSpec((CONV_TAIL, nb, CONV_CH), lambda i: (0, i, 0))
    return pl.pallas_call(
        _prompt_ffn_kernel,
        grid=(steps,),
        in_specs=[rows, rows, const(w_out.shape), const(w_ffn_in.shape), const(w_ffn_out.shape),
                  const(n_post.shape), const(n_fpre.shape), const(n_fpost.shape),
                  dec_cols(0), dec_cols(1), dec_rows(vgu.shape[1]),
                  const(gam.shape), st_spec, cv_spec,
                  const(conv_w.shape), const(conv_b.shape), const(ln_g.shape), const(ln_b.shape)],
        out_specs=(rows, dec_rows(D_MODEL), st_spec, cv_spec),
        out_shape=(jax.ShapeDtypeStruct((n, D_MODEL), F32),
                   jax.ShapeDtypeStruct((n_dec, D_MODEL), F32),
                   jax.ShapeDtypeStruct(state_ret.shape, F32),
                   jax.ShapeDtypeStruct(state_conv.shape, F32)),
        scratch_shapes=[pltpu.VMEM((nb, RET_W), F32)],
        compiler_params=pltpu.CompilerParams(
            dimension_semantics=("arbitrary",), vmem_limit_bytes=OUT_FFN_VMEM),
        name="out_ffn",
    )(*_hbm(mix, x2d, w_out, w_ffn_in, w_ffn_out, n_post, n_fpre, n_fpost,
            qkt4, qkt4, vgu, gam, state_ret, state_conv, conv_w, conv_b, ln_g, ln_b))


def _decode_ffn(mix, x2d, w_out, w_ffn_in, w_ffn_out, n_post, n_fpre, n_fpost):
    n = x2d.shape[0]
    full = lambda shape: pl.BlockSpec(shape, lambda i: (0,) * len(shape), pipeline_mode=pl.Buffered(1))
    return pl.pallas_call(
        _out_ffn_kernel,
        grid=(1,),
        in_specs=[full(a.shape) for a in (mix, x2d, w_out, w_ffn_in, w_ffn_out, n_post, n_fpre, n_fpost)],
        out_specs=pl.BlockSpec((n, D_MODEL), lambda i: (0, 0)),
        out_shape=jax.ShapeDtypeStruct((n, D_MODEL), F32),
        compiler_params=pltpu.CompilerParams(
            dimension_semantics=("arbitrary",), vmem_limit_bytes=DECODE_FFN_VMEM),
        name="decode_ffn",
    )(*_hbm(mix, x2d, w_out, w_ffn_in, w_ffn_out, n_post, n_fpre, n_fpost))


def _rotary_tables(pos):
    half = HEAD_DIM // 2
    inv = 1.0 / (ROPE_BASE ** np.linspace(0.0, 1.0, half))
    ang = np.asarray(pos, np.float64)[:, None] * inv[None, :]
    cos, sin = np.cos(ang), np.sin(ang)
    return (np.concatenate([cos, cos], axis=-1).astype(np.float32),
            np.concatenate([-sin, sin], axis=-1).astype(np.float32))


def _decay_tables():
    lg = np.log(1.0 - 2.0 ** (-5.0 - np.arange(RET_HEADS, dtype=np.float64)))
    idx = np.arange(CHUNK, dtype=np.float64)
    diff = idx[:, None] - idx[None, :]
    mask = np.where(diff[None] >= 0, np.exp(lg[:, None, None] * np.maximum(diff, 0.0)[None]), 0.0)
    wide = lambda col: np.broadcast_to(col[:, :, None], (RET_HEADS, CHUNK, HEAD_DIM))
    dec_end = wide(np.exp(lg[:, None] * (CHUNK - 1 - idx)[None, :]))
    dec_start = wide(np.exp(lg[:, None] * (idx + 1.0)[None, :]))
    chunk_dec = np.broadcast_to(np.exp(lg * CHUNK)[:, None, None], (RET_HEADS, HEAD_DIM, HEAD_DIM))
    gam = np.broadcast_to(np.exp(lg)[:, None], (RET_HEADS, HEAD_DIM))
    return tuple(np.ascontiguousarray(t, np.float32) for t in (mask, dec_start, dec_end, chunk_dec, gam))


def kernel(x_prompt, x_sample, state_ret, state_conv, meta_tokens, norm_mix_pre, norm_mix_post,
           norm_ffn_pre, norm_ffn_post, w_in, conv_w, conv_b, conv_ln_g, conv_ln_b, w_out,
           w_ffn_in, w_ffn_out):
    batch, seq, _ = x_prompt.shape
    n_dec, dec_seq, _ = x_sample.shape
    depth = w_in.shape[0]
    assert depth == 1 and dec_seq == 1 and seq % PROMPT_TILE == 0
    assert (batch * seq) % FFN_TILE == 0

    row = lambda v: v.reshape(1, -1)
    nw_pre, nw_post = row(norm_mix_pre[0]), row(norm_mix_post[0])
    nf_pre, nf_post = row(norm_ffn_pre[0]), row(norm_ffn_post[0])
    cw, cb, lg, lb = conv_w[0], row(conv_b[0]), row(conv_ln_g[0]), row(conv_ln_b[0])

    mask, dec_start, dec_end, chunk_dec, gam = _decay_tables()
    pos_small = np.concatenate([
        np.full((n_dec,), PAST_LEN), np.zeros((CHUNK - N_META,)), np.arange(N_META)])
    cos_s, sin_s = _rotary_tables(pos_small)
    cos_p, sin_p = _rotary_tables(N_META + np.arange(seq))

    x_dec = x_sample.reshape(n_dec, D_MODEL)
    vgu, qkt, r0, tail0, w_in_b = _small_inproj(
        x_dec, meta_tokens.astype(x_prompt.dtype), nw_pre, w_in[0], cos_s, sin_s, dec_end)

    x2d = x_prompt.reshape(batch * seq, D_MODEL)
    mix_p, ret_p, conv_p, w_out_b, w_fi_b, w_fo_b = _prompt_mix(
        x2d, nw_pre, w_in_b, cos_p, sin_p, mask, dec_start, dec_end, chunk_dec, r0, tail0,
        cw, cb, lg, lb, w_out[0], w_ffn_in[0], w_ffn_out[0], batch, seq)

    blocks = (batch * seq) // FFN_TILE
    qkt4 = qkt.reshape(2, RET_W, blocks, n_dec // blocks).transpose(0, 2, 1, 3)
    y_p, mix_s, ret_s, conv_s = _prompt_ffn(
        mix_p, x2d, w_out_b, w_fi_b, w_fo_b, nw_post, nf_pre, nf_post,
        qkt4, vgu, gam, state_ret[0], state_conv[0].transpose(1, 0, 2), cw, cb, lg, lb, FFN_TILE)
    conv_s = conv_s.transpose(1, 0, 2)
    y_s = _decode_ffn(mix_s, x_dec, w_out_b, w_fi_b, w_fo_b, nw_post, nf_pre, nf_post)

    return (y_p.reshape(batch, seq, D_MODEL), y_s.reshape(n_dec, 1, D_MODEL),
            ret_p[None], conv_p[None], ret_s[None], conv_s[None])
```
